```python
import jax, jax.numpy as jnp
from jax import lax
import numpy as np

D_MODEL = 1024
BATCH = 4
SEQ = 4096
DEPTH = 1

D_MIX = D_MODEL
D_ATT = D_MIX // 2
D_POOL = D_MIX - D_ATT
N_ATT_HEADS = 8
HEAD_DIM = D_ATT // N_ATT_HEADS
POOL_WINDOWS = (2, 4, 8, 16)
N_POOL_GROUPS = len(POOL_WINDOWS)
POOL_GROUP_DIM = D_POOL // N_POOL_GROUPS
Q_BLOCK = 128
LN_EPS = 1e-5
FORGET_BIAS_INIT = 3.0
DEEPNORM_ALPHA = (2.0 * DEPTH) ** 0.25
DEEPNORM_BETA = (8.0 * DEPTH) ** -0.25
D_IN = 3 * D_ATT + N_ATT_HEADS + D_POOL + D_ATT + D_POOL
SPLIT_POINTS = (D_ATT, 2 * D_ATT, 3 * D_ATT, 3 * D_ATT + N_ATT_HEADS,
                3 * D_ATT + N_ATT_HEADS + D_POOL, 3 * D_ATT + N_ATT_HEADS + D_POOL + D_ATT)

kernel_name = "hymba_fox_poolformer_deepnorm_adaln"


def _layer_norm(h, g, b):
    h32 = h.astype(jnp.float32)
    mu = jnp.mean(h32, axis=-1, keepdims=True)
    var = jnp.mean(jnp.square(h32 - mu), axis=-1, keepdims=True)
    y = (h32 - mu) * lax.rsqrt(var + LN_EPS)
    return (y * g.astype(jnp.float32) + b.astype(jnp.float32)).astype(h.dtype)


def _forgetting_attention(q, k, v, f_logit):
    B, S, _ = q.shape
    q = q.reshape(B, S, N_ATT_HEADS, HEAD_DIM)
    k = k.reshape(B, S, N_ATT_HEADS, HEAD_DIM)
    v = v.reshape(B, S, N_ATT_HEADS, HEAD_DIM)
    log_f = jax.nn.log_sigmoid(f_logit.astype(jnp.float32))
    cum = jnp.cumsum(log_f, axis=1)
    nb = S // Q_BLOCK
    q_blocks = q.reshape(B, nb, Q_BLOCK, N_ATT_HEADS, HEAD_DIM).transpose(1, 0, 2, 3, 4)
    cum_blocks = cum.reshape(B, nb, Q_BLOCK, N_ATT_HEADS).transpose(1, 0, 2, 3)
    cum_k = cum.transpose(0, 2, 1)[:, :, None, :]
    k_pos = jnp.arange(S)
    scale = HEAD_DIM ** -0.5

    def one_block(args):
        q_blk, cum_q, idx = args
        s = jnp.einsum('bqhd,bkhd->bhqk', q_blk, k).astype(jnp.float32) * scale
        s = s + cum_q.transpose(0, 2, 1)[..., None] - cum_k
        q_pos = idx * Q_BLOCK + jnp.arange(Q_BLOCK)
        causal = k_pos[None, :] <= q_pos[:, None]
        s = jnp.where(causal[None, None], s, -jnp.inf)
        p = jax.nn.softmax(s, axis=-1).astype(v.dtype)
        return jnp.einsum('bhqk,bkhd->bqhd', p, v)

    out = lax.map(one_block, (q_blocks, cum_blocks, jnp.arange(nb)))
    return out.transpose(1, 0, 2, 3, 4).reshape(B, S, D_ATT)


def _multiscale_pool(p, w_pool_mix, b_pool_mix, pool_scale):
    B, S, _ = p.shape
    p32 = p.astype(jnp.float32)
    cs = jnp.cumsum(p32, axis=1)
    count = jnp.arange(1, S + 1, dtype=jnp.float32)
    outs = []
    for g, w in enumerate(POOL_WINDOWS):
        sl = slice(g * POOL_GROUP_DIM, (g + 1) * POOL_GROUP_DIM)
        cs_g = cs[..., sl]
        lagged = jnp.pad(cs_g, ((0, 0), (w, 0), (0, 0)))[:, :S]
        mean = (cs_g - lagged) / jnp.minimum(count, float(w))[None, :, None]
        outs.append(mean - p32[..., sl])
    pooled = jnp.stack(outs, axis=2).astype(p.dtype)
    mixed = jnp.einsum('bsgc,gce->bsge', pooled, w_pool_mix) + b_pool_mix
    return mixed.reshape(B, S, D_POOL) * pool_scale


def _hybrid_layer(x, c, w_ada, b_ada, w_in, b_in, w_pool_mix, b_pool_mix, pool_scale,
                  w_out, b_out, ln_g, ln_b):
    ada = jnp.einsum('bd,de->be', jax.nn.silu(c), w_ada) + b_ada
    shift, scale, gate = jnp.split(ada, 3, axis=-1)
    u = x * (1 + scale[:, None, :]) + shift[:, None, :]
    proj = jnp.einsum('bsd,de->bse', u, w_in) + b_in
    q, k, v, f_logit, p, g_att, g_pool = jnp.split(proj, SPLIT_POINTS, axis=-1)
    att = _forgetting_attention(q, k, v, f_logit)
    pool = _multiscale_pool(p, w_pool_mix, b_pool_mix, pool_scale)
    y = jnp.concatenate([att * jax.nn.silu(g_att), pool * jax.nn.silu(g_pool)], axis=-1)
    y = jnp.einsum('bse,ed->bsd', y, w_out) + b_out
    h = DEEPNORM_ALPHA * x + gate[:, None, :] * y
    return _layer_norm(h, ln_g, ln_b)


def setup_inputs(seed: int = 0) -> dict:
    key = jax.random.key(seed)
    ks = jax.random.split(key, 20)
    D = D_MODEL
    s_d = D ** -0.5
    x = jax.random.normal(ks[0], (BATCH, SEQ, D), jnp.float32)
    c = jax.random.normal(ks[1], (BATCH, D), jnp.float32)
    w_ada = jax.random.normal(ks[2], (DEPTH, D, 3 * D), jnp.float32) * s_d
    b_ada = 0.01 * jax.random.normal(ks[3], (DEPTH, 3 * D), jnp.float32)
    w_q = jax.random.normal(ks[4], (DEPTH, D, D_ATT), jnp.float32) * s_d
    w_k = jax.random.normal(ks[5], (DEPTH, D, D_ATT), jnp.float32) * s_d
    w_v = jax.random.normal(ks[6], (DEPTH, D, D_ATT), jnp.float32) * (s_d * DEEPNORM_BETA)
    w_f = jax.random.normal(ks[7], (DEPTH, D, N_ATT_HEADS), jnp.float32) * (0.1 * s_d)
    w_p = jax.random.normal(ks[8], (DEPTH, D, D_POOL), jnp.float32) * s_d
    w_g = jax.random.normal(ks[9], (DEPTH, D, D_ATT + D_POOL), jnp.float32) * s_d
    w_in = jnp.concatenate([w_q, w_k, w_v, w_f, w_p, w_g], axis=-1)
    b_in = 0.01 * jax.random.normal(ks[10], (DEPTH, D_IN), jnp.float32)
    b_in = b_in.at[:, 3 * D_ATT:3 * D_ATT + N_ATT_HEADS].add(FORGET_BIAS_INIT)
    w_pool_mix = jax.random.normal(ks[11], (DEPTH, N_POOL_GROUPS, POOL_GROUP_DIM, POOL_GROUP_DIM),
                                   jnp.float32) * POOL_GROUP_DIM ** -0.5
    b_pool_mix = 0.01 * jax.random.normal(ks[12], (DEPTH, N_POOL_GROUPS, POOL_GROUP_DIM), jnp.float32)
    pool_scale = 1.0 + 0.02 * jax.random.normal(ks[13], (DEPTH, D_POOL), jnp.float32)
    w_out = jax.random.normal(ks[14], (DEPTH, D_MIX, D), jnp.float32) * (D_MIX ** -0.5 * DEEPNORM_BETA)
    b_out = 0.01 * jax.random.normal(ks[15], (DEPTH, D), jnp.float32)
    ln_g = 1.0 + 0.02 * jax.random.normal(ks[16], (DEPTH, D), jnp.float32)
    ln_b = 0.01 * jax.random.normal(ks[17], (DEPTH, D), jnp.float32)
    return {"x": x, "c": c, "w_ada": w_ada, "b_ada": b_ada, "w_in": w_in, "b_in": b_in,
            "w_pool_mix": w_pool_mix, "b_pool_mix": b_pool_mix, "pool_scale": pool_scale,
            "w_out": w_out, "b_out": b_out, "ln_g": ln_g, "ln_b": ln_b}


def reference(x, c, w_ada, b_ada, w_in, b_in, w_pool_mix, b_pool_mix, pool_scale,
              w_out, b_out, ln_g, ln_b):
    for layer in range(DEPTH):
        x = _hybrid_layer(x, c, w_ada[layer], b_ada[layer], w_in[layer], b_in[layer],
                          w_pool_mix[layer], b_pool_mix[layer], pool_scale[layer],
                          w_out[layer], b_out[layer], ln_g[layer], ln_b[layer])
    return x
```

```python
import functools
import math

import jax
import jax.numpy as jnp
from jax import lax
from jax.experimental import pallas as pl
from jax.experimental.pallas import tpu as pltpu

D_MODEL = 1024
D_ATT = 512
D_POOL = 512
N_HEADS = 8
HEAD_DIM = 64
POOL_WINDOWS = (2, 4, 8, 16)
POOL_GROUP_DIM = 128
POOL_HISTORY = 16
LN_EPS = 1e-5
DEEPNORM_ALPHA = 2.0 ** 0.25
LOG2E = math.log2(math.e)
NEG_BIG = -1e30

LANES = 128
HEADS_PER_BLOCK = LANES // HEAD_DIM

SEQ_TILE = 512
Q_TILE = 256
KV_TILE = 256
ADA_COL_TILE = 512
VMEM_LIMIT_BYTES = 56 * 1024 * 1024

F32 = jnp.float32
BF16 = jnp.bfloat16


def _silu(x):
    return x * jax.nn.sigmoid(x)


def _dot(a, b):
    return jnp.dot(a, b, preferred_element_type=F32)


def _ada_kernel(c_ref, w_ref, b_ref, o_ref):
    sc = _silu(c_ref[...])
    o_ref[...] = jnp.dot(sc, w_ref[...], preferred_element_type=F32,
                         precision=lax.Precision.HIGHEST) + b_ref[...]


def _ada_call(c_pad, w_ada, b_ada):
    rows, d = c_pad.shape
    n = w_ada.shape[1]
    return pl.pallas_call(
        _ada_kernel,
        grid=(n // ADA_COL_TILE,),
        in_specs=[
            pl.BlockSpec((rows, d), lambda j: (0, 0)),
            pl.BlockSpec((d, ADA_COL_TILE), lambda j: (0, j)),
            pl.BlockSpec((1, ADA_COL_TILE), lambda j: (0, j)),
        ],
        out_specs=pl.BlockSpec((rows, ADA_COL_TILE), lambda j: (0, j)),
        out_shape=jax.ShapeDtypeStruct((rows, n), F32),
        compiler_params=pltpu.CompilerParams(dimension_semantics=("parallel",)),
    )(c_pad, w_ada, b_ada)


def _split3_bf16(x):
    hi = x.astype(BF16)
    r = x - hi.astype(F32)
    mid = r.astype(BF16)
    lo = (r - mid.astype(F32)).astype(BF16)
    return hi, mid, lo


def _in_kernel(x_ref, ada_ref, wq_ref, wk_ref, wv_ref, wf_ref, wp_ref, wga_ref, wgp_ref,
               bq_ref, bk_ref, bv_ref, bf_ref, bp_ref, bga_ref, bgp_ref,
               wpm_ref, bpm_ref, ps_ref,
               q_ref, kx_ref, v_ref, g_ref, ga_ref, yp_ref,
               fcarry_ref, pcarry_ref):
    si = pl.program_id(1)
    ts = x_ref.shape[1]

    @pl.when(si == 0)
    def _():
        fcarry_ref[...] = jnp.zeros_like(fcarry_ref)
        pcarry_ref[...] = jnp.zeros_like(pcarry_ref)

    ada = ada_ref[0]
    u = (x_ref[0] * (1.0 + ada[1:2]) + ada[0:1]).astype(BF16)

    q = (_dot(u, wq_ref[...]) + bq_ref[...]) * (HEAD_DIM ** -0.5 * LOG2E)
    q_ref[0] = q.astype(BF16)

    k = _dot(u, wk_ref[...]) + bk_ref[...]
    lane = lax.broadcasted_iota(jnp.int32, (ts, LANES), 1)
    for h in range(N_HEADS):
        blk = h // HEADS_PER_BLOCK
        kp = k[:, blk * LANES:(blk + 1) * LANES]
        own = (lane // HEAD_DIM) == (h % HEADS_PER_BLOCK)
        kx_ref[0, :, h * LANES:(h + 1) * LANES] = jnp.where(own, kp, 0.0).astype(BF16)

    v_ref[0] = (_dot(u, wv_ref[...]) + bv_ref[...]).astype(BF16)

    fl = _dot(u, wf_ref[...]) + bf_ref[...]
    logf = jnp.minimum(fl, 0.0) - jnp.log1p(jnp.exp(-jnp.abs(fl)))
    row = lax.broadcasted_iota(jnp.int32, (ts, ts), 0)
    col = lax.broadcasted_iota(jnp.int32, (ts, ts), 1)
    tri = jnp.where(row >= col, 1.0, 0.0).astype(BF16)
    hi, mid, lo = _split3_bf16(logf)
    cum = _dot(tri, hi) + _dot(tri, mid) + _dot(tri, lo) + fcarry_ref[0:1, :]
    fcarry_ref[...] = jnp.broadcast_to(cum[ts - 1:ts, :], fcarry_ref.shape)
    g_ref[0] = (cum * (-LOG2E)).T[:N_HEADS, :]

    p = _dot(u, wp_ref[...]) + bp_ref[...]
    pe = jnp.concatenate([pcarry_ref[...], p], axis=0)
    pcarry_ref[...] = p[ts - POOL_HISTORY:, :]
    gp = _dot(u, wgp_ref[...]) + bgp_ref[...]
    t_glob = si * ts + lax.broadcasted_iota(jnp.int32, (ts, POOL_GROUP_DIM), 0)
    for g, w in enumerate(POOL_WINDOWS):
        sl = slice(g * POOL_GROUP_DIM, (g + 1) * POOL_GROUP_DIM)
        y = pe[:, sl]
        sh = 1
        while sh < w:
            y = y + pltpu.roll(y, sh, axis=0)
            sh *= 2
        cnt = jnp.minimum(t_glob + 1, w).astype(F32)
        pooled = y[POOL_HISTORY:, :] / cnt - p[:, sl]
        mixed = _dot(pooled.astype(BF16), wpm_ref[g]) + bpm_ref[g:g + 1, :]
        yp_ref[0, :, sl] = (mixed * ps_ref[:, sl] * _silu(gp[:, sl])).astype(BF16)

    ga_ref[0] = _silu(_dot(u, wga_ref[...]) + bga_ref[...]).astype(BF16)


def _in_call(x, ada3, wts, biases, w_pm, b_pm, pool_scale):
    B, S, D = x.shape
    ts = SEQ_TILE
    const2 = lambda b, s: (0, 0)
    w_specs = [pl.BlockSpec(w.shape, const2) for w in wts]
    b_specs = [pl.BlockSpec(bb.shape, const2) for bb in biases]
    tile = lambda width: pl.BlockSpec((1, ts, width), lambda b, s: (b, s, 0))
    return pl.pallas_call(
        _in_kernel,
        grid=(B, S // ts),
        in_specs=[tile(D), pl.BlockSpec((1, 3, D), lambda b, s: (b, 0, 0))]
                 + w_specs + b_specs
                 + [pl.BlockSpec(w_pm.shape, lambda b, s: (0, 0, 0)),
                    pl.BlockSpec(b_pm.shape, const2),
                    pl.BlockSpec(pool_scale.shape, const2)],
        out_specs=[tile(D_ATT), tile(N_HEADS * LANES), tile(D_ATT),
                   pl.BlockSpec((1, N_HEADS, ts), lambda b, s: (b, 0, s)),
                   tile(D_ATT), tile(D_POOL)],
        out_shape=[jax.ShapeDtypeStruct((B, S, D_ATT), BF16),
                   jax.ShapeDtypeStruct((B, S, N_HEADS * LANES), BF16),
                   jax.ShapeDtypeStruct((B, S, D_ATT), BF16),
                   jax.ShapeDtypeStruct((B, N_HEADS, S), F32),
                   jax.ShapeDtypeStruct((B, S, D_ATT), BF16),
                   jax.ShapeDtypeStruct((B, S, D_POOL), BF16)],
        scratch_shapes=[pltpu.VMEM((8, LANES), F32),
                        pltpu.VMEM((POOL_HISTORY, D_POOL), F32)],
        compiler_params=pltpu.CompilerParams(
            dimension_semantics=("arbitrary", "arbitrary"),
            vmem_limit_bytes=VMEM_LIMIT_BYTES),
    )(x, ada3, *wts, *biases, w_pm, b_pm, pool_scale)


def _attn_kernel(q_ref, kx_ref, v_ref, g_ref, ga_ref, o_ref, m_ref, l_ref, acc_ref):
    qi = pl.program_id(2)
    tq = q_ref.shape[1]
    tk = KV_TILE
    q = q_ref[0]

    m_ref[...] = jnp.full_like(m_ref, NEG_BIG)
    l_ref[...] = jnp.zeros_like(l_ref)
    acc_ref[...] = jnp.zeros_like(acc_ref)

    def block(j, masked):
        start = pl.multiple_of(j * tk, tk)
        vblk = v_ref[0, pl.ds(start, tk), :]
        for h in range(HEADS_PER_BLOCK):
            kblk = kx_ref[0, pl.ds(start, tk), h * LANES:(h + 1) * LANES]
            s = lax.dot_general(q, kblk, (((1,), (1,)), ((), ())),
                                preferred_element_type=F32)
            s = s + g_ref[0, 0, h:h + 1, pl.ds(start, tk)]
            if masked:
                r = lax.broadcasted_iota(jnp.int32, (tq, tk), 0)
                c = lax.broadcasted_iota(jnp.int32, (tq, tk), 1)
                s = jnp.where(c <= r, s, NEG_BIG)
            m_prev = m_ref[h]
            m_new = jnp.maximum(m_prev, jnp.max(s, axis=1, keepdims=True))
            alpha = jnp.exp2(m_prev - m_new)
            p = jnp.exp2(s - pltpu.repeat(m_new, tk // LANES, axis=1))
            psum = p[:, 0:LANES]
            for t in range(1, tk // LANES):
                psum = psum + p[:, t * LANES:(t + 1) * LANES]
            l_ref[h] = alpha * l_ref[h] + psum
            acc_ref[h] = alpha * acc_ref[h] + _dot(p.astype(BF16), vblk)
            m_ref[h] = m_new

    def body(j, carry):
        block(j, masked=False)
        return carry

    lax.fori_loop(0, qi, body, 0)
    block(qi, masked=True)

    lane = lax.broadcasted_iota(jnp.int32, (tq, LANES), 1)
    out = jnp.zeros((tq, LANES), F32)
    for h in range(HEADS_PER_BLOCK):
        l = jnp.sum(l_ref[h], axis=1, keepdims=True)
        out = jnp.where((lane // HEAD_DIM) == h, acc_ref[h] / l, out)
    o_ref[0] = (out * ga_ref[0].astype(F32)).astype(BF16)


def _attn_call(q, kx, v, g4, ga):
    B, S, _ = q.shape
    assert Q_TILE == KV_TILE
    nblk = D_ATT // LANES
    return pl.pallas_call(
        _attn_kernel,
        grid=(B, nblk, S // Q_TILE),
        in_specs=[
            pl.BlockSpec((1, Q_TILE, LANES), lambda b, p, i: (b, i, p)),
            pl.BlockSpec((1, S, HEADS_PER_BLOCK * LANES), lambda b, p, i: (b, 0, p)),
            pl.BlockSpec((1, S, LANES), lambda b, p, i: (b, 0, p)),
            pl.BlockSpec((1, 1, HEADS_PER_BLOCK, S), lambda b, p, i: (b, p, 0, 0)),
            pl.BlockSpec((1, Q_TILE, LANES), lambda b, p, i: (b, i, p)),
        ],
        out_specs=pl.BlockSpec((1, Q_TILE, LANES), lambda b, p, i: (b, i, p)),
        out_shape=jax.ShapeDtypeStruct((B, S, D_ATT), BF16),
        scratch_shapes=[pltpu.VMEM((HEADS_PER_BLOCK, Q_TILE, LANES), F32),
                        pltpu.VMEM((HEADS_PER_BLOCK, Q_TILE, LANES), F32),
                        pltpu.VMEM((HEADS_PER_BLOCK, Q_TILE, LANES), F32)],
        compiler_params=pltpu.CompilerParams(
            dimension_semantics=("parallel", "parallel", "parallel"),
            vmem_limit_bytes=VMEM_LIMIT_BYTES),
    )(q, kx, v, g4, ga)


def _out_kernel(x_ref, ada_ref, ya_ref, yp_ref, woa_ref, wop_ref, bo_ref, lg_ref, lb_ref,
                o_ref):
    gate = ada_ref[0][2:3]
    y = _dot(ya_ref[0], woa_ref[...]) + _dot(yp_ref[0], wop_ref[...]) + bo_ref[...]
    h = DEEPNORM_ALPHA * x_ref[0] + gate * y
    mu = jnp.mean(h, axis=-1, keepdims=True)
    d = h - mu
    var = jnp.mean(d * d, axis=-1, keepdims=True)
    o_ref[0] = d * lax.rsqrt(var + LN_EPS) * lg_ref[...] + lb_ref[...]


def _out_call(x, ada3, ya, yp, wo_a, wo_p, b_out, ln_g, ln_b):
    B, S, D = x.shape
    ts = SEQ_TILE
    const2 = lambda b, s: (0, 0)
    tile = lambda width: pl.BlockSpec((1, ts, width), lambda b, s: (b, s, 0))
    return pl.pallas_call(
        _out_kernel,
        grid=(B, S // ts),
        in_specs=[tile(D), pl.BlockSpec((1, 3, D), lambda b, s: (b, 0, 0)),
                  tile(D_ATT), tile(D_POOL),
                  pl.BlockSpec(wo_a.shape, const2), pl.BlockSpec(wo_p.shape, const2),
                  pl.BlockSpec(b_out.shape, const2), pl.BlockSpec(ln_g.shape, const2),
                  pl.BlockSpec(ln_b.shape, const2)],
        out_specs=tile(D),
        out_shape=jax.ShapeDtypeStruct((B, S, D), F32),
        compiler_params=pltpu.CompilerParams(
            dimension_semantics=("parallel", "parallel"),
            vmem_limit_bytes=VMEM_LIMIT_BYTES),
    )(x, ada3, ya, yp, wo_a, wo_p, b_out, ln_g, ln_b)


def _layer(x, c, w_ada, b_ada, w_in, b_in, w_pool_mix, b_pool_mix, pool_scale,
           w_out, b_out, ln_g, ln_b):
    B, S, D = x.shape
    c_pad = jnp.pad(c, ((0, 8 - B), (0, 0)))
    ada = _ada_call(c_pad, w_ada, b_ada[None, :])
    ada3 = ada[:B].reshape(B, 3, D)

    edges = (0, D_ATT, 2 * D_ATT, 3 * D_ATT, 3 * D_ATT + N_HEADS,
             3 * D_ATT + N_HEADS + D_POOL, 3 * D_ATT + N_HEADS + D_POOL + D_ATT,
             3 * D_ATT + N_HEADS + 2 * D_POOL + D_ATT)
    seg_w = [w_in[:, a:b] for a, b in zip(edges[:-1], edges[1:])]
    seg_b = [b_in[None, a:b] for a, b in zip(edges[:-1], edges[1:])]
    seg_w[3] = jnp.pad(seg_w[3], ((0, 0), (0, LANES - N_HEADS)))
    seg_b[3] = jnp.pad(seg_b[3], ((0, 0), (0, LANES - N_HEADS)))
    seg_w = [w.astype(BF16) for w in seg_w]

    q, kx, v, g, ga, yp = _in_call(x, ada3, seg_w, seg_b, w_pool_mix.astype(BF16),
                                   b_pool_mix, pool_scale[None, :])
    g4 = g.reshape(B, N_HEADS // HEADS_PER_BLOCK, HEADS_PER_BLOCK, S)
    ya = _attn_call(q, kx, v, g4, ga)
    wo = w_out.astype(BF16)
    return _out_call(x, ada3, ya, yp, wo[:D_ATT], wo[D_ATT:], b_out[None, :],
                     ln_g[None, :], ln_b[None, :])


def kernel(x, c, w_ada, b_ada, w_in, b_in, w_pool_mix, b_pool_mix, pool_scale, w_out, b_out,
           ln_g, ln_b):
    for layer in range(w_ada.shape[0]):
        x = _layer(x, c, w_ada[layer], b_ada[layer], w_in[layer], b_in[layer],
                   w_pool_mix[layer], b_pool_mix[layer], pool_scale[layer],
                   w_out[layer], b_out[layer], ln_g[layer], ln_b[layer])
    return x
```

```python
import functools
import math

import jax
import jax.numpy as jnp
from jax import lax
from jax.experimental import pallas as pl
from jax.experimental.pallas import tpu as pltpu

D_MODEL = 1024
D_ATT = 512
D_POOL = 512
N_HEADS = 8
HEAD_DIM = 64
POOL_WINDOWS = (2, 4, 8, 16)
POOL_GROUP_DIM = 128
POOL_HISTORY = 16
LN_EPS = 1e-5
DEEPNORM_ALPHA = 2.0 ** 0.25
LOG2E = math.log2(math.e)
NEG_BIG = -1e30

LANES = 128
HEADS_PER_BLOCK = LANES // HEAD_DIM

SEQ_TILE = 512
Q_TILE = 256
KV_TILE = 512
ADA_COL_TILE = 512
VMEM_LIMIT_BYTES = 56 * 1024 * 1024

F32 = jnp.float32
BF16 = jnp.bfloat16


def _silu(x):
    return x * jax.nn.sigmoid(x)


def _dot(a, b):
    return jnp.dot(a, b, preferred_element_type=F32)


def _ada_kernel(c_ref, w_ref, b_ref, o_ref):
    sc = _silu(c_ref[...])
    o_ref[...] = jnp.dot(sc, w_ref[...], preferred_element_type=F32,
                         precision=lax.Precision.HIGHEST) + b_ref[...]


def _ada_call(c_pad, w_ada, b_ada):
    rows, d = c_pad.shape
    n = w_ada.shape[1]
    return pl.pallas_call(
        _ada_kernel,
        name="ada_vector",
        grid=(n // ADA_COL_TILE,),
        in_specs=[
            pl.BlockSpec((rows, d), lambda j: (0, 0)),
            pl.BlockSpec((d, ADA_COL_TILE), lambda j: (0, j)),
            pl.BlockSpec((1, ADA_COL_TILE), lambda j: (0, j)),
        ],
        out_specs=pl.BlockSpec((rows, ADA_COL_TILE), lambda j: (0, j)),
        out_shape=jax.ShapeDtypeStruct((rows, n), F32),
        compiler_params=pltpu.CompilerParams(dimension_semantics=("parallel",)),
    )(c_pad, w_ada, b_ada)


def _split3_bf16(x):
    hi = x.astype(BF16)
    r = x - hi.astype(F32)
    mid = r.astype(BF16)
    lo = (r - mid.astype(F32)).astype(BF16)
    return hi, mid, lo


def _in_kernel(x_ref, ada_ref, wq_ref, wk_ref, wv_ref, wf_ref, wp_ref, wga_ref, wgp_ref,
               bq_ref, bk_ref, bv_ref, bf_ref, bp_ref, bga_ref, bgp_ref,
               wpm_ref, bpm_ref, ps_ref,
               q_ref, kx_ref, v_ref, g_ref, ga_ref, yp_ref,
               fcarry_ref, pcarry_ref):
    si = pl.program_id(1)
    ts = x_ref.shape[1]

    @pl.when(si == 0)
    def _():
        fcarry_ref[...] = jnp.zeros_like(fcarry_ref)
        pcarry_ref[...] = jnp.zeros_like(pcarry_ref)

    ada = ada_ref[0]
    u = (x_ref[0] * (1.0 + ada[1:2]) + ada[0:1]).astype(BF16)

    q = (_dot(u, wq_ref[...]) + bq_ref[...]) * (HEAD_DIM ** -0.5 * LOG2E)
    q_ref[0] = q.astype(BF16)

    k = _dot(u, wk_ref[...]) + bk_ref[...]
    lane = lax.broadcasted_iota(jnp.int32, (ts, LANES), 1)
    for h in range(N_HEADS):
        blk = h // HEADS_PER_BLOCK
        kp = k[:, blk * LANES:(blk + 1) * LANES]
        own = (lane // HEAD_DIM) == (h % HEADS_PER_BLOCK)
        kx_ref[0, :, h * LANES:(h + 1) * LANES] = jnp.where(own, kp, 0.0).astype(BF16)

    v_ref[0] = (_dot(u, wv_ref[...]) + bv_ref[...]).astype(BF16)

    fl = _dot(u, wf_ref[...]) + bf_ref[...]
    logf = jnp.minimum(fl, 0.0) - jnp.log1p(jnp.exp(-jnp.abs(fl)))
    row = lax.broadcasted_iota(jnp.int32, (ts, ts), 0)
    col = lax.broadcasted_iota(jnp.int32, (ts, ts), 1)
    tri = jnp.where(row >= col, 1.0, 0.0).astype(BF16)
    hi, mid, lo = _split3_bf16(logf)
    cum = _dot(tri, hi) + _dot(tri, mid) + _dot(tri, lo) + fcarry_ref[0:1, :]
    fcarry_ref[...] = jnp.broadcast_to(cum[ts - 1:ts, :], fcarry_ref.shape)
    g_ref[0] = (cum * (-LOG2E)).T[:N_HEADS, :]

    p = _dot(u, wp_ref[...]) + bp_ref[...]
    pe = jnp.concatenate([pcarry_ref[...], p], axis=0)
    pcarry_ref[...] = p[ts - POOL_HISTORY:, :]
    gp = _dot(u, wgp_ref[...]) + bgp_ref[...]
    t_glob = si * ts + lax.broadcasted_iota(jnp.int32, (ts, POOL_GROUP_DIM), 0)
    for g, w in enumerate(POOL_WINDOWS):
        sl = slice(g * POOL_GROUP_DIM, (g + 1) * POOL_GROUP_DIM)
        y = pe[:, sl]
        sh = 1
        while sh < w:
            y = y + pltpu.roll(y, sh, axis=0)
            sh *= 2
        cnt = jnp.minimum(t_glob + 1, w).astype(F32)
        pooled = y[POOL_HISTORY:, :] / cnt - p[:, sl]
        mixed = _dot(pooled.astype(BF16), wpm_ref[g]) + bpm_ref[g:g + 1, :]
        yp_ref[0, :, sl] = (mixed * ps_ref[:, sl] * _silu(gp[:, sl])).astype(BF16)

    ga_ref[0] = _silu(_dot(u, wga_ref[...]) + bga_ref[...]).astype(BF16)


def _in_call(x, ada3, wts, biases, w_pm, b_pm, pool_scale):
    B, S, D = x.shape
    ts = SEQ_TILE
    const2 = lambda b, s: (0, 0)
    w_specs = [pl.BlockSpec(w.shape, const2) for w in wts]
    b_specs = [pl.BlockSpec(bb.shape, const2) for bb in biases]
    tile = lambda width: pl.BlockSpec((1, ts, width), lambda b, s: (b, s, 0))
    return pl.pallas_call(
        _in_kernel,
        name="input_stage",
        grid=(B, S // ts),
        in_specs=[tile(D), pl.BlockSpec((1, 3, D), lambda b, s: (b, 0, 0))]
                 + w_specs + b_specs
                 + [pl.BlockSpec(w_pm.shape, lambda b, s: (0, 0, 0)),
                    pl.BlockSpec(b_pm.shape, const2),
                    pl.BlockSpec(pool_scale.shape, const2)],
        out_specs=[tile(D_ATT), tile(N_HEADS * LANES), tile(D_ATT),
                   pl.BlockSpec((1, N_HEADS, ts), lambda b, s: (b, 0, s)),
                   tile(D_ATT), tile(D_POOL)],
        out_shape=[jax.ShapeDtypeStruct((B, S, D_ATT), BF16),
                   jax.ShapeDtypeStruct((B, S, N_HEADS * LANES), BF16),
                   jax.ShapeDtypeStruct((B, S, D_ATT), BF16),
                   jax.ShapeDtypeStruct((B, N_HEADS, S), F32),
                   jax.ShapeDtypeStruct((B, S, D_ATT), BF16),
                   jax.ShapeDtypeStruct((B, S, D_POOL), BF16)],
        scratch_shapes=[pltpu.VMEM((8, LANES), F32),
                        pltpu.VMEM((POOL_HISTORY, D_POOL), F32)],
        compiler_params=pltpu.CompilerParams(
            dimension_semantics=("arbitrary", "arbitrary"),
            vmem_limit_bytes=VMEM_LIMIT_BYTES),
    )(x, ada3, *wts, *biases, w_pm, b_pm, pool_scale)


def _attn_kernel(q_ref, kx_ref, v_ref, g_ref, ga_ref, o_ref, m_ref, l_ref, acc_ref):
    qi = pl.program_id(1)
    tq = q_ref.shape[1]
    tk = KV_TILE

    m_ref[...] = jnp.full_like(m_ref, NEG_BIG)
    l_ref[...] = jnp.zeros_like(l_ref)
    acc_ref[...] = jnp.zeros_like(acc_ref)

    def block(start, width, masked):
        def scores(h):
            blk = h // HEADS_PER_BLOCK
            q = q_ref[0, :, blk * LANES:(blk + 1) * LANES]
            kblk = kx_ref[0, pl.ds(start, width), h * LANES:(h + 1) * LANES]
            s = lax.dot_general(q, kblk, (((1,), (1,)), ((), ())),
                                preferred_element_type=F32)
            s = s + g_ref[0, h:h + 1, pl.ds(start, width)]
            if masked:
                r = lax.broadcasted_iota(jnp.int32, (tq, width), 0)
                c = lax.broadcasted_iota(jnp.int32, (tq, width), 1)
                s = jnp.where(c <= r, s, NEG_BIG)
            return s

        def update(h, s):
            blk = h // HEADS_PER_BLOCK
            vblk = v_ref[0, pl.ds(start, width), blk * LANES:(blk + 1) * LANES]
            m_prev = m_ref[h]
            m_new = jnp.maximum(m_prev, jnp.max(s, axis=1, keepdims=True))
            alpha = jnp.exp2(m_prev - m_new)
            p = jnp.exp2(s - pltpu.repeat(m_new, width // LANES, axis=1))
            psum = p[:, 0:LANES]
            for t in range(1, width // LANES):
                psum = psum + p[:, t * LANES:(t + 1) * LANES]
            l_ref[h] = alpha * l_ref[h] + psum
            acc_ref[h] = alpha * acc_ref[h] + _dot(p.astype(BF16), vblk)
            m_ref[h] = m_new

        s_next = scores(0)
        for h in range(N_HEADS):
            s_cur = s_next
            if h + 1 < N_HEADS:
                s_next = scores(h + 1)
            update(h, s_cur)

    ratio = tk // tq

    def body(j, carry):
        block(pl.multiple_of(j * tk, tk), tk, masked=False)
        return carry

    lax.fori_loop(0, qi // ratio, body, 0)
    for r in range(1, ratio):
        @pl.when(qi % ratio >= r)
        def _():
            block(pl.multiple_of((qi - r) * tq, tq), tq, masked=False)
    block(pl.multiple_of(qi * tq, tq), tq, masked=True)

    lane = lax.broadcasted_iota(jnp.int32, (tq, LANES), 1)
    for blk in range(N_HEADS // HEADS_PER_BLOCK):
        out = jnp.zeros((tq, LANES), F32)
        for hh in range(HEADS_PER_BLOCK):
            h = blk * HEADS_PER_BLOCK + hh
            l = jnp.sum(l_ref[h], axis=1, keepdims=True)
            out = jnp.where((lane // HEAD_DIM) == hh, acc_ref[h] / l, out)
        sl = slice(blk * LANES, (blk + 1) * LANES)
        o_ref[0, :, sl] = (out * ga_ref[0, :, sl].astype(F32)).astype(BF16)


def _attn_call(q, kx, v, g, ga):
    B, S, _ = q.shape
    assert KV_TILE % Q_TILE == 0 and S % KV_TILE == 0
    tile = pl.BlockSpec((1, Q_TILE, D_ATT), lambda b, i: (b, i, 0))
    whole = lambda width: pl.BlockSpec((1, S, width), lambda b, i: (b, 0, 0))
    return pl.pallas_call(
        _attn_kernel,
        name="attention",
        grid=(B, S // Q_TILE),
        in_specs=[tile, whole(N_HEADS * LANES), whole(D_ATT),
                  pl.BlockSpec((1, N_HEADS, S), lambda b, i: (b, 0, 0)), tile],
        out_specs=tile,
        out_shape=jax.ShapeDtypeStruct((B, S, D_ATT), BF16),
        scratch_shapes=[pltpu.VMEM((N_HEADS, Q_TILE, LANES), F32),
                        pltpu.VMEM((N_HEADS, Q_TILE, LANES), F32),
                        pltpu.VMEM((N_HEADS, Q_TILE, LANES), F32)],
        compiler_params=pltpu.CompilerParams(
            dimension_semantics=("parallel", "parallel"),
            vmem_limit_bytes=VMEM_LIMIT_BYTES),
    )(q, kx, v, g, ga)


def _out_kernel(x_ref, ada_ref, ya_ref, yp_ref, woa_ref, wop_ref, bo_ref, lg_ref, lb_ref,
                o_ref):
    gate = ada_ref[0][2:3]
    y = _dot(ya_ref[0], woa_ref[...]) + _dot(yp_ref[0], wop_ref[...]) + bo_ref[...]
    h = DEEPNORM_ALPHA * x_ref[0] + gate * y
    mu = jnp.mean(h, axis=-1, keepdims=True)
    d = h - mu
    var = jnp.mean(d * d, axis=-1, keepdims=True)
    o_ref[0] = d * lax.rsqrt(var + LN_EPS) * lg_ref[...] + lb_ref[...]


def _out_call(x, ada3, ya, yp, wo_a, wo_p, b_out, ln_g, ln_b):
    B, S, D = x.shape
    ts = SEQ_TILE
    const2 = lambda b, s: (0, 0)
    tile = lambda width: pl.BlockSpec((1, ts, width), lambda b, s: (b, s, 0))
    return pl.pallas_call(
        _out_kernel,
        name="output_stage",
        grid=(B, S // ts),
        in_specs=[tile(D), pl.BlockSpec((1, 3, D), lambda b, s: (b, 0, 0)),
                  tile(D_ATT), tile(D_POOL),
                  pl.BlockSpec(wo_a.shape, const2), pl.BlockSpec(wo_p.shape, const2),
                  pl.BlockSpec(b_out.shape, const2), pl.BlockSpec(ln_g.shape, const2),
                  pl.BlockSpec(ln_b.shape, const2)],
        out_specs=tile(D),
        out_shape=jax.ShapeDtypeStruct((B, S, D), F32),
        compiler_params=pltpu.CompilerParams(
            dimension_semantics=("parallel", "parallel"),
            vmem_limit_bytes=VMEM_LIMIT_BYTES),
    )(x, ada3, ya, yp, wo_a, wo_p, b_out, ln_g, ln_b)


def _layer(x, c, w_ada, b_ada, w_in, b_in, w_pool_mix, b_pool_mix, pool_scale,
           w_out, b_out, ln_g, ln_b):
    B, S, D = x.shape
    c_pad = jnp.pad(c, ((0, 8 - B), (0, 0)))
    ada = _ada_call(c_pad, w_ada, b_ada[None, :])
    ada3 = ada[:B].reshape(B, 3, D)

    edges = (0, D_ATT, 2 * D_ATT, 3 * D_ATT, 3 * D_ATT + N_HEADS,
             3 * D_ATT + N_HEADS + D_POOL, 3 * D_ATT + N_HEADS + D_POOL + D_ATT,
             3 * D_ATT + N_HEADS + 2 * D_POOL + D_ATT)
    seg_w = [w_in[:, a:b] for a, b in zip(edges[:-1], edges[1:])]
    seg_b = [b_in[None, a:b] for a, b in zip(edges[:-1], edges[1:])]
    seg_w[3] = jnp.pad(seg_w[3], ((0, 0), (0, LANES - N_HEADS)))
    seg_b[3] = jnp.pad(seg_b[3], ((0, 0), (0, LANES - N_HEADS)))
    seg_w = [w.astype(BF16) for w in seg_w]

    q, kx, v, g, ga, yp = _in_call(x, ada3, seg_w, seg_b, w_pool_mix.astype(BF16),
                                   b_pool_mix, pool_scale[None, :])
    ya = _attn_call(q, kx, v, g, ga)
    wo = w_out.astype(BF16)
    return _out_call(x, ada3, ya, yp, wo[:D_ATT], wo[D_ATT:], b_out[None, :],
                     ln_g[None, :], ln_b[None, :])


def kernel(x, c, w_ada, b_ada, w_in, b_in, w_pool_mix, b_pool_mix, pool_scale, w_out, b_out,
           ln_g, ln_b):
    for layer in range(w_ada.shape[0]):
        x = _layer(x, c, w_ada[layer], b_ada[layer], w_in[layer], b_in[layer],
                   w_pool_mix[layer], b_pool_mix[layer], pool_scale[layer],
                   w_out[layer], b_out[layer], ln_g[layer], ln_b[layer])
    return x
```

```python
import functools
import math

import jax
import jax.numpy as jnp
from jax import lax
from jax.experimental import pallas as pl
from jax.experimental.pallas import tpu as pltpu

D_MODEL = 1024
D_ATT = 512
D_POOL = 512
N_HEADS = 8
HEAD_DIM = 64
POOL_WINDOWS = (2, 4, 8, 16)
POOL_GROUP_DIM = 128
POOL_HISTORY = 16
LN_EPS = 1e-5
DEEPNORM_ALPHA = 2.0 ** 0.25
LOG2E = math.log2(math.e)
NEG_BIG = -1e30

LANES = 128
HEADS_PER_BLOCK = LANES // HEAD_DIM
G_TERMS = 3
PV_ROWS = HEAD_DIM + 16

SEQ_TILE = 512
Q_TILE = 256
KV_TILE = 512
ADA_COL_TILE = 512
VMEM_LIMIT_BYTES = 56 * 1024 * 1024

F32 = jnp.float32
BF16 = jnp.bfloat16


def _silu(x):
    return x * jax.nn.sigmoid(x)


def _dot(a, b):
    return jnp.dot(a, b, preferred_element_type=F32)


def _ada_kernel(c_ref, w_ref, b_ref, o_ref):
    sc = _silu(c_ref[...])
    o_ref[...] = jnp.dot(sc, w_ref[...], preferred_element_type=F32,
                         precision=lax.Precision.HIGHEST) + b_ref[...]


def _ada_call(c_pad, w_ada, b_ada):
    rows, d = c_pad.shape
    n = w_ada.shape[1]
    return pl.pallas_call(
        _ada_kernel,
        name="ada_vector",
        grid=(n // ADA_COL_TILE,),
        in_specs=[
            pl.BlockSpec((rows, d), lambda j: (0, 0)),
            pl.BlockSpec((d, ADA_COL_TILE), lambda j: (0, j)),
            pl.BlockSpec((1, ADA_COL_TILE), lambda j: (0, j)),
        ],
        out_specs=pl.BlockSpec((rows, ADA_COL_TILE), lambda j: (0, j)),
        out_shape=jax.ShapeDtypeStruct((rows, n), F32),
        compiler_params=pltpu.CompilerParams(dimension_semantics=("parallel",)),
    )(c_pad, w_ada, b_ada)


def _split3_bf16(x):
    hi = x.astype(BF16)
    r = x - hi.astype(F32)
    mid = r.astype(BF16)
    lo = (r - mid.astype(F32)).astype(BF16)
    return hi, mid, lo


def _in_kernel(x_ref, ada_ref, wq_ref, wk_ref, wv_ref, wf_ref, wp_ref, wga_ref, wgp_ref,
               bq_ref, bk_ref, bv_ref, bf_ref, bp_ref, bga_ref, bgp_ref,
               wpm_ref, bpm_ref, ps_ref,
               q_ref, kx_ref, v_ref, g_ref, ga_ref, yp_ref,
               fcarry_ref, pcarry_ref):
    si = pl.program_id(1)
    ts = x_ref.shape[1]

    @pl.when(si == 0)
    def _():
        fcarry_ref[...] = jnp.zeros_like(fcarry_ref)
        pcarry_ref[...] = jnp.zeros_like(pcarry_ref)

    ada = ada_ref[0]
    u = (x_ref[0] * (1.0 + ada[1:2]) + ada[0:1]).astype(BF16)

    q = (_dot(u, wq_ref[...]) + bq_ref[...]) * (HEAD_DIM ** -0.5 * LOG2E)
    for blk in range(D_ATT // LANES):
        q_ref[0, blk] = q[:, blk * LANES:(blk + 1) * LANES].T.astype(BF16)

    k = _dot(u, wk_ref[...]) + bk_ref[...]
    lane = lax.broadcasted_iota(jnp.int32, (ts, LANES), 1)
    for h in range(N_HEADS):
        blk = h // HEADS_PER_BLOCK
        kp = k[:, blk * LANES:(blk + 1) * LANES]
        own = (lane // HEAD_DIM) == (h % HEADS_PER_BLOCK)
        kx_ref[0, :, h * LANES:(h + 1) * LANES] = jnp.where(own, kp, 0.0).astype(BF16)

    v = _dot(u, wv_ref[...]) + bv_ref[...]
    for blk in range(D_ATT // LANES):
        v_ref[0, blk] = v[:, blk * LANES:(blk + 1) * LANES].T.astype(BF16)

    fl = _dot(u, wf_ref[...]) + bf_ref[...]
    logf = jnp.minimum(fl, 0.0) - jnp.log1p(jnp.exp(-jnp.abs(fl)))
    row = lax.broadcasted_iota(jnp.int32, (ts, ts), 0)
    col = lax.broadcasted_iota(jnp.int32, (ts, ts), 1)
    tri = jnp.where(row >= col, 1.0, 0.0).astype(BF16)
    hi, mid, lo = _split3_bf16(logf)
    cum = _dot(tri, hi) + _dot(tri, mid) + _dot(tri, lo) + fcarry_ref[0:1, :]
    fcarry_ref[...] = jnp.broadcast_to(cum[ts - 1:ts, :], fcarry_ref.shape)
    parts = jnp.concatenate(_split3_bf16(cum * (-LOG2E)), axis=1)
    src = lax.broadcasted_iota(jnp.int32, (G_TERMS * LANES, LANES), 0)
    dst = lax.broadcasted_iota(jnp.int32, (G_TERMS * LANES, LANES), 1)
    head, term = src % LANES, src // LANES
    place = jnp.where((head < N_HEADS) & (dst == G_TERMS * head + term), 1.0, 0.0)
    g_ref[0] = _dot(parts, place.astype(BF16)).astype(BF16)

    p = _dot(u, wp_ref[...]) + bp_ref[...]
    pe = jnp.concatenate([pcarry_ref[...], p], axis=0)
    pcarry_ref[...] = p[ts - POOL_HISTORY:, :]
    gp = _dot(u, wgp_ref[...]) + bgp_ref[...]
    t_glob = si * ts + lax.broadcasted_iota(jnp.int32, (ts, POOL_GROUP_DIM), 0)
    for g, w in enumerate(POOL_WINDOWS):
        sl = slice(g * POOL_GROUP_DIM, (g + 1) * POOL_GROUP_DIM)
        y = pe[:, sl]
        sh = 1
        while sh < w:
            y = y + pltpu.roll(y, sh, axis=0)
            sh *= 2
        cnt = jnp.minimum(t_glob + 1, w).astype(F32)
        pooled = y[POOL_HISTORY:, :] / cnt - p[:, sl]
        mixed = _dot(pooled.astype(BF16), wpm_ref[g]) + bpm_ref[g:g + 1, :]
        yp_ref[0, :, sl] = (mixed * ps_ref[:, sl] * _silu(gp[:, sl])).astype(BF16)

    ga_ref[0] = _silu(_dot(u, wga_ref[...]) + bga_ref[...]).astype(BF16)


def _in_call(x, ada3, wts, biases, w_pm, b_pm, pool_scale):
    B, S, D = x.shape
    ts = SEQ_TILE
    const2 = lambda b, s: (0, 0)
    w_specs = [pl.BlockSpec(w.shape, const2) for w in wts]
    b_specs = [pl.BlockSpec(bb.shape, const2) for bb in biases]
    tile = lambda width: pl.BlockSpec((1, ts, width), lambda b, s: (b, s, 0))
    n_pairs = D_ATT // LANES
    tile_t = pl.BlockSpec((1, n_pairs, LANES, ts), lambda b, s: (b, 0, 0, s))
    return pl.pallas_call(
        _in_kernel,
        name="input_stage",
        grid=(B, S // ts),
        in_specs=[tile(D), pl.BlockSpec((1, 3, D), lambda b, s: (b, 0, 0))]
                 + w_specs + b_specs
                 + [pl.BlockSpec(w_pm.shape, lambda b, s: (0, 0, 0)),
                    pl.BlockSpec(b_pm.shape, const2),
                    pl.BlockSpec(pool_scale.shape, const2)],
        out_specs=[tile_t, tile(N_HEADS * LANES), tile_t, tile(LANES),
                   tile(D_ATT), tile(D_POOL)],
        out_shape=[jax.ShapeDtypeStruct((B, n_pairs, LANES, S), BF16),
                   jax.ShapeDtypeStruct((B, S, N_HEADS * LANES), BF16),
                   jax.ShapeDtypeStruct((B, n_pairs, LANES, S), BF16),
                   jax.ShapeDtypeStruct((B, S, LANES), BF16),
                   jax.ShapeDtypeStruct((B, S, D_ATT), BF16),
                   jax.ShapeDtypeStruct((B, S, D_POOL), BF16)],
        scratch_shapes=[pltpu.VMEM((8, LANES), F32),
                        pltpu.VMEM((POOL_HISTORY, D_POOL), F32)],
        compiler_params=pltpu.CompilerParams(
            dimension_semantics=("arbitrary", "arbitrary"),
            vmem_limit_bytes=VMEM_LIMIT_BYTES),
    )(x, ada3, *wts, *biases, w_pm, b_pm, pool_scale)


def _attn_kernel(qt_ref, kx_ref, vt_ref, g_ref, ga_ref, o_ref,
                 m_ref, acc_ref, s0_ref, s1_ref, mx0_ref, mx1_ref):
    qi = pl.program_id(1)
    tq = qt_ref.shape[3]
    tk = KV_TILE
    n_main = (qi * tq) // tk
    diag_start = pl.multiple_of(n_main * tk, tk)
    diag_shift = qi * tq - diag_start

    m_ref[...] = jnp.full_like(m_ref, NEG_BIG)
    acc_ref[...] = jnp.zeros_like(acc_ref)

    sel_row = lax.broadcasted_iota(jnp.int32, (LANES, tq), 0)
    ones_row = lax.broadcasted_iota(jnp.int32, (PV_ROWS - HEAD_DIM, tk), 0) == 0

    def block_start(k):
        return pl.multiple_of(jnp.where(k == 0, diag_start, (k - 1) * tk), tk)

    def scores(k, s_ref, mx_ref, masked=False):
        start = block_start(k)
        gblk = g_ref[0, pl.ds(start, tk), :]
        for h in range(N_HEADS):
            kblk = kx_ref[0, pl.ds(start, tk), h * LANES:(h + 1) * LANES]
            sel = (sel_row >= G_TERMS * h) & (sel_row < G_TERMS * (h + 1))
            lhs = jnp.concatenate([kblk, gblk], axis=1)
            rhs = jnp.concatenate([qt_ref[0, h // HEADS_PER_BLOCK],
                                   jnp.where(sel, 1.0, 0.0).astype(BF16)], axis=0)
            s = _dot(lhs, rhs)
            if masked:
                key = lax.broadcasted_iota(jnp.int32, (tk, tq), 0)
                qry = lax.broadcasted_iota(jnp.int32, (tk, tq), 1)
                s = jnp.where(key <= qry + diag_shift, s, NEG_BIG)
            s_ref[h] = s
            mx_ref[h] = jnp.max(s.reshape(tk // 8, 8, tq), axis=0)

    def softmax_pv(k, s_ref, mx_ref):
        start = block_start(k)

        def probs(h):
            m_prev = m_ref[h]
            m_new = jnp.maximum(m_prev, jnp.max(mx_ref[h], axis=0, keepdims=True))
            m_ref[h] = m_new
            alpha = jnp.exp2(m_prev - m_new)
            p = jnp.exp2(s_ref[h] - m_new[0:1, :]).astype(BF16)
            return alpha, p

        def accumulate(h, alpha, p):
            pair, half = divmod(h, HEADS_PER_BLOCK)
            vt = vt_ref[0, pair, half * HEAD_DIM:(half + 1) * HEAD_DIM, pl.ds(start, tk)]
            ones = jnp.where(ones_row, 1.0, 0.0).astype(BF16)
            pv = _dot(jnp.concatenate([vt, ones], axis=0), p)
            acc_ref[h] = acc_ref[h] * alpha[0:1, :] + pv

        ap = {}
        for t in range(N_HEADS + 1):
            if t < N_HEADS:
                ap[t] = probs(t)
            if t >= 1:
                accumulate(t - 1, *ap.pop(t - 1))

    n_blocks = n_main + 1
    scores(0, s0_ref, mx0_ref, masked=True)

    def pair_of_blocks(i, carry):
        k = 2 * i
        scores(k + 1, s1_ref, mx1_ref)
        softmax_pv(k, s0_ref, mx0_ref)
        scores(k + 2, s0_ref, mx0_ref)
        softmax_pv(k + 1, s1_ref, mx1_ref)
        return carry

    n_pairs = (n_blocks - 1) // 2
    lax.fori_loop(0, n_pairs, pair_of_blocks, 0)
    k_last = 2 * n_pairs

    @pl.when(n_blocks - k_last == 2)
    def _():
        scores(k_last + 1, s1_ref, mx1_ref)
        softmax_pv(k_last, s0_ref, mx0_ref)
        softmax_pv(k_last + 1, s1_ref, mx1_ref)

    @pl.when(n_blocks - k_last == 1)
    def _():
        softmax_pv(k_last, s0_ref, mx0_ref)


    for pair in range(N_HEADS // HEADS_PER_BLOCK):
        outs = []
        for half in range(HEADS_PER_BLOCK):
            acc = acc_ref[pair * HEADS_PER_BLOCK + half]
            outs.append(acc[:HEAD_DIM, :] / acc[HEAD_DIM:HEAD_DIM + 1, :])
        att = jnp.concatenate(outs, axis=0).T
        sl = slice(pair * LANES, (pair + 1) * LANES)
        o_ref[0, :, sl] = (att * ga_ref[0, :, sl].astype(F32)).astype(BF16)


def _attn_call(qt, kx, vt, g, ga):
    B, n_pairs, _, S = qt.shape
    assert KV_TILE % Q_TILE == 0 and S % KV_TILE == 0
    tile = pl.BlockSpec((1, Q_TILE, D_ATT), lambda b, i: (b, i, 0))
    whole = lambda width: pl.BlockSpec((1, S, width), lambda b, i: (b, 0, 0))
    score_buf = pltpu.VMEM((N_HEADS, KV_TILE, Q_TILE), F32)
    stat_buf = pltpu.VMEM((N_HEADS, 8, Q_TILE), F32)
    return pl.pallas_call(
        _attn_kernel,
        name="attention",
        grid=(B, S // Q_TILE),
        in_specs=[pl.BlockSpec((1, n_pairs, LANES, Q_TILE), lambda b, i: (b, 0, 0, i)),
                  whole(N_HEADS * LANES),
                  pl.BlockSpec((1, n_pairs, LANES, S), lambda b, i: (b, 0, 0, 0)),
                  whole(LANES), tile],
        out_specs=tile,
        out_shape=jax.ShapeDtypeStruct((B, S, D_ATT), BF16),
        scratch_shapes=[stat_buf, pltpu.VMEM((N_HEADS, PV_ROWS, Q_TILE), F32),
                        score_buf, score_buf, stat_buf, stat_buf],
        compiler_params=pltpu.CompilerParams(
            dimension_semantics=("parallel", "parallel"),
            vmem_limit_bytes=VMEM_LIMIT_BYTES),
    )(qt, kx, vt, g, ga)


def _out_kernel(x_ref, ada_ref, ya_ref, yp_ref, woa_ref, wop_ref, bo_ref, lg_ref, lb_ref,
                o_ref):
    gate = ada_ref[0][2:3]
    y = _dot(ya_ref[0], woa_ref[...]) + _dot(yp_ref[0], wop_ref[...]) + bo_ref[...]
    h = DEEPNORM_ALPHA * x_ref[0] + gate * y
    mu = jnp.mean(h, axis=-1, keepdims=True)
    d = h - mu
    var = jnp.mean(d * d, axis=-1, keepdims=True)
    o_ref[0] = d * lax.rsqrt(var + LN_EPS) * lg_ref[...] + lb_ref[...]


def _out_call(x, ada3, ya, yp, wo_a, wo_p, b_out, ln_g, ln_b):
    B, S, D = x.shape
    ts = SEQ_TILE
    const2 = lambda b, s: (0, 0)
    tile = lambda width: pl.BlockSpec((1, ts, width), lambda b, s: (b, s, 0))
    return pl.pallas_call(
        _out_kernel,
        name="output_stage",
        grid=(B, S // ts),
        in_specs=[tile(D), pl.BlockSpec((1, 3, D), lambda b, s: (b, 0, 0)),
                  tile(D_ATT), tile(D_POOL),
                  pl.BlockSpec(wo_a.shape, const2), pl.BlockSpec(wo_p.shape, const2),
                  pl.BlockSpec(b_out.shape, const2), pl.BlockSpec(ln_g.shape, const2),
                  pl.BlockSpec(ln_b.shape, const2)],
        out_specs=tile(D),
        out_shape=jax.ShapeDtypeStruct((B, S, D), F32),
        compiler_params=pltpu.CompilerParams(
            dimension_semantics=("parallel", "parallel"),
            vmem_limit_bytes=VMEM_LIMIT_BYTES),
    )(x, ada3, ya, yp, wo_a, wo_p, b_out, ln_g, ln_b)


def _layer(x, c, w_ada, b_ada, w_in, b_in, w_pool_mix, b_pool_mix, pool_scale,
           w_out, b_out, ln_g, ln_b):
    B, S, D = x.shape
    c_pad = jnp.pad(c, ((0, 8 - B), (0, 0)))
    ada = _ada_call(c_pad, w_ada, b_ada[None, :])
    ada3 = ada[:B].reshape(B, 3, D)

    edges = (0, D_ATT, 2 * D_ATT, 3 * D_ATT, 3 * D_ATT + N_HEADS,
             3 * D_ATT + N_HEADS + D_POOL, 3 * D_ATT + N_HEADS + D_POOL + D_ATT,
             3 * D_ATT + N_HEADS + 2 * D_POOL + D_ATT)
    seg_w = [w_in[:, a:b] for a, b in zip(edges[:-1], edges[1:])]
    seg_b = [b_in[None, a:b] for a, b in zip(edges[:-1], edges[1:])]
    seg_w[3] = jnp.pad(seg_w[3], ((0, 0), (0, LANES - N_HEADS)))
    seg_b[3] = jnp.pad(seg_b[3], ((0, 0), (0, LANES - N_HEADS)))
    seg_w = [w.astype(BF16) for w in seg_w]

    q, kx, v, g, ga, yp = _in_call(x, ada3, seg_w, seg_b, w_pool_mix.astype(BF16),
                                   b_pool_mix, pool_scale[None, :])
    ya = _attn_call(q, kx, v, g, ga)
    wo = w_out.astype(BF16)
    return _out_call(x, ada3, ya, yp, wo[:D_ATT], wo[D_ATT:], b_out[None, :],
                     ln_g[None, :], ln_b[None, :])


def kernel(x, c, w_ada, b_ada, w_in, b_in, w_pool_mix, b_pool_mix, pool_scale, w_out, b_out,
           ln_g, ln_b):
    for layer in range(w_ada.shape[0]):
        x = _layer(x, c, w_ada[layer], b_ada[layer], w_in[layer], b_in[layer],
                   w_pool_mix[layer], b_pool_mix[layer], pool_scale[layer],
                   w_out[layer], b_out[layer], ln_g[layer], ln_b[layer])
    return x
```

```python
import functools
import math

import jax
import jax.numpy as jnp
from jax import lax
from jax.experimental import pallas as pl
from jax.experimental.pallas import tpu as pltpu

D_MODEL = 1024
D_ATT = 512
D_POOL = 512
N_HEADS = 8
HEAD_DIM = 64
POOL_WINDOWS = (2, 4, 8, 16)
POOL_GROUP_DIM = 128
POOL_HISTORY = 16
LN_EPS = 1e-5
DEEPNORM_ALPHA = 2.0 ** 0.25
LOG2E = math.log2(math.e)
NEG_BIG = -1e30

LANES = 128
HEADS_PER_BLOCK = LANES // HEAD_DIM
G_TERMS = 3
CUM_ROWS = 128
PV_ROWS = HEAD_DIM + 16

SEQ_TILE = 512
Q_TILE = 256
KV_TILE = 512
ADA_COL_TILE = 512
VMEM_LIMIT_BYTES = 56 * 1024 * 1024

F32 = jnp.float32
BF16 = jnp.bfloat16


def _silu(x):
    return x * jax.nn.sigmoid(x)


def _dot(a, b):
    return jnp.dot(a, b, preferred_element_type=F32)


def _ada_kernel(c_ref, w_ref, b_ref, o_ref):
    sc = _silu(c_ref[...])
    o_ref[...] = jnp.dot(sc, w_ref[...], preferred_element_type=F32,
                         precision=lax.Precision.HIGHEST) + b_ref[...]


def _ada_call(c_pad, w_ada, b_ada):
    rows, d = c_pad.shape
    n = w_ada.shape[1]
    return pl.pallas_call(
        _ada_kernel,
        name="ada_vector",
        grid=(n // ADA_COL_TILE,),
        in_specs=[
            pl.BlockSpec((rows, d), lambda j: (0, 0)),
            pl.BlockSpec((d, ADA_COL_TILE), lambda j: (0, j)),
            pl.BlockSpec((1, ADA_COL_TILE), lambda j: (0, j)),
        ],
        out_specs=pl.BlockSpec((rows, ADA_COL_TILE), lambda j: (0, j)),
        out_shape=jax.ShapeDtypeStruct((rows, n), F32),
        compiler_params=pltpu.CompilerParams(dimension_semantics=("parallel",)),
    )(c_pad, w_ada, b_ada)


def _split3(x):
    hi = x.astype(BF16).astype(F32)
    r = x - hi
    mid = r.astype(BF16).astype(F32)
    lo = (r - mid).astype(BF16).astype(F32)
    return hi, mid, lo


def _in_kernel(x_ref, ada_ref, wq_ref, wk_ref, wv_ref, wf_ref, wp_ref, wga_ref, wgp_ref,
               bq_ref, bk_ref, bv_ref, bf_ref, bp_ref, bga_ref, bgp_ref,
               wpm_ref, bpm_ref, ps_ref,
               q_ref, k_ref, v_ref, g_ref, ga_ref, yp_ref,
               fcarry_ref, pcarry_ref):
    si = pl.program_id(1)
    ts = x_ref.shape[1]

    @pl.when(si == 0)
    def _():
        fcarry_ref[...] = jnp.zeros_like(fcarry_ref)
        pcarry_ref[...] = jnp.zeros_like(pcarry_ref)

    ada = ada_ref[0]
    u = (x_ref[0] * (1.0 + ada[1:2]) + ada[0:1]).astype(BF16)

    q = (_dot(u, wq_ref[...]) + bq_ref[...]) * (HEAD_DIM ** -0.5 * LOG2E)
    for blk in range(D_ATT // LANES):
        q_ref[0, blk] = q[:, blk * LANES:(blk + 1) * LANES].T.astype(BF16)

    k_ref[0] = (_dot(u, wk_ref[...]) + bk_ref[...]).astype(BF16)

    v = _dot(u, wv_ref[...]) + bv_ref[...]
    for blk in range(D_ATT // LANES):
        v_ref[0, blk] = v[:, blk * LANES:(blk + 1) * LANES].T.astype(BF16)

    fl = _dot(u, wf_ref[...]) + bf_ref[...]
    logf = jnp.minimum(fl, 0.0) - jnp.log1p(jnp.exp(-jnp.abs(fl)))
    row = lax.broadcasted_iota(jnp.int32, (CUM_ROWS, CUM_ROWS), 0)
    col = lax.broadcasted_iota(jnp.int32, (CUM_ROWS, CUM_ROWS), 1)
    tri = jnp.where(row >= col, 1.0, 0.0).astype(BF16)
    terms = jnp.concatenate([t.astype(BF16) for t in _split3(logf)], axis=1)
    offset = fcarry_ref[0:1, :]
    cum_blocks = []
    for r in range(ts // CUM_ROWS):
        part = _dot(tri, terms[r * CUM_ROWS:(r + 1) * CUM_ROWS, :])
        local = part[:, :LANES] + part[:, LANES:2 * LANES] + part[:, 2 * LANES:]
        cum_blocks.append(local + offset)
        offset = cum_blocks[-1][CUM_ROWS - 1:CUM_ROWS, :]
    fcarry_ref[...] = jnp.broadcast_to(offset, fcarry_ref.shape)
    cum = jnp.concatenate(cum_blocks, axis=0)

    g_hi, g_mid, g_lo = _split3(cum * (-LOG2E))
    lane = lax.broadcasted_iota(jnp.int32, (ts, LANES), 1)
    g = jnp.where(lane < N_HEADS, g_hi,
                  jnp.where(lane < 2 * N_HEADS, pltpu.roll(g_mid, N_HEADS, axis=1),
                            jnp.where(lane < 3 * N_HEADS, pltpu.roll(g_lo, 2 * N_HEADS, axis=1),
                                      0.0)))
    g_ref[0] = g.astype(BF16)

    p = _dot(u, wp_ref[...]) + bp_ref[...]
    pe = jnp.concatenate([pcarry_ref[...], p], axis=0)
    pcarry_ref[...] = p[ts - POOL_HISTORY:, :]
    gp = _dot(u, wgp_ref[...]) + bgp_ref[...]
    t_glob = si * ts + lax.broadcasted_iota(jnp.int32, (ts, POOL_GROUP_DIM), 0)
    pooled = []
    for g, w in enumerate(POOL_WINDOWS):
        sl = slice(g * POOL_GROUP_DIM, (g + 1) * POOL_GROUP_DIM)
        y = pe[:, sl]
        sh = 1
        while sh < w:
            y = y + pltpu.roll(y, sh, axis=0)
            sh *= 2
        cnt = jnp.minimum(t_glob + 1, w).astype(F32)
        pooled.append((y[POOL_HISTORY:, :] / cnt - p[:, sl]).astype(BF16))
    for j in range(len(POOL_WINDOWS) // 2):
        sl = slice(2 * j * POOL_GROUP_DIM, (2 * j + 2) * POOL_GROUP_DIM)
        mixed = _dot(jnp.concatenate(pooled[2 * j:2 * j + 2], axis=1), wpm_ref[j]) + bpm_ref[:, sl]
        yp_ref[0, :, sl] = (mixed * ps_ref[:, sl] * _silu(gp[:, sl])).astype(BF16)

    ga_ref[0] = _silu(_dot(u, wga_ref[...]) + bga_ref[...]).astype(BF16)


def _in_call(x, ada3, wts, biases, w_pm, b_pm, pool_scale):
    B, S, D = x.shape
    ts = SEQ_TILE
    const2 = lambda b, s: (0, 0)
    w_specs = [pl.BlockSpec(w.shape, const2) for w in wts]
    b_specs = [pl.BlockSpec(bb.shape, const2) for bb in biases]
    tile = lambda width: pl.BlockSpec((1, ts, width), lambda b, s: (b, s, 0))
    n_pairs = D_ATT // LANES
    tile_t = pl.BlockSpec((1, n_pairs, LANES, ts), lambda b, s: (b, 0, 0, s))
    return pl.pallas_call(
        _in_kernel,
        name="input_stage",
        grid=(B, S // ts),
        in_specs=[tile(D), pl.BlockSpec((1, 3, D), lambda b, s: (b, 0, 0))]
                 + w_specs + b_specs
                 + [pl.BlockSpec(w_pm.shape, lambda b, s: (0, 0, 0)),
                    pl.BlockSpec(b_pm.shape, const2),
                    pl.BlockSpec(pool_scale.shape, const2)],
        out_specs=[tile_t, tile(D_ATT), tile_t, tile(LANES),
                   tile(D_ATT), tile(D_POOL)],
        out_shape=[jax.ShapeDtypeStruct((B, n_pairs, LANES, S), BF16),
                   jax.ShapeDtypeStruct((B, S, D_ATT), BF16),
                   jax.ShapeDtypeStruct((B, n_pairs, LANES, S), BF16),
                   jax.ShapeDtypeStruct((B, S, LANES), BF16),
                   jax.ShapeDtypeStruct((B, S, D_ATT), BF16),
                   jax.ShapeDtypeStruct((B, S, D_POOL), BF16)],
        scratch_shapes=[pltpu.VMEM((8, LANES), F32),
                        pltpu.VMEM((POOL_HISTORY, D_POOL), F32)],
        compiler_params=pltpu.CompilerParams(
            dimension_semantics=("arbitrary", "arbitrary"),
            vmem_limit_bytes=VMEM_LIMIT_BYTES),
    )(x, ada3, *wts, *biases, w_pm, b_pm, pool_scale)


def _attn_kernel(qt_ref, k_ref, vt_ref, g_ref, ga_ref, o_ref,
                 m_ref, acc_ref, s0_ref, s1_ref, mx0_ref, mx1_ref, rhs_ref):
    qi = pl.program_id(1)
    tq = qt_ref.shape[3]
    tk = KV_TILE
    n_main = (qi * tq) // tk
    diag_start = pl.multiple_of(n_main * tk, tk)
    diag_shift = qi * tq - diag_start

    m_ref[...] = jnp.full_like(m_ref, NEG_BIG)
    acc_ref[...] = jnp.zeros_like(acc_ref)

    row = lax.broadcasted_iota(jnp.int32, (LANES, tq), 0)
    for h in range(N_HEADS):
        pair, half = divmod(h, HEADS_PER_BLOCK)
        own = (row // HEAD_DIM) == half
        rhs_ref[h, :LANES, :] = jnp.where(own, qt_ref[0, pair], jnp.zeros((), BF16))
        pick = (row < G_TERMS * N_HEADS) & (row % N_HEADS == h)
        rhs_ref[h, LANES:, :] = jnp.where(pick, 1.0, 0.0).astype(BF16)
    ones_row = lax.broadcasted_iota(jnp.int32, (PV_ROWS - HEAD_DIM, tk), 0) == 0

    def block_start(k):
        return pl.multiple_of(jnp.where(k == 0, diag_start, (k - 1) * tk), tk)

    def scores(k, s_ref, mx_ref, masked=False):
        start = block_start(k)
        gblk = g_ref[0, pl.ds(start, tk), :]
        for h in range(N_HEADS):
            pair = h // HEADS_PER_BLOCK
            kblk = k_ref[0, pl.ds(start, tk), pair * LANES:(pair + 1) * LANES]
            lhs = jnp.concatenate([kblk, gblk], axis=1)
            s = _dot(lhs, rhs_ref[h])
            if masked:
                key = lax.broadcasted_iota(jnp.int32, (tk, tq), 0)
                qry = lax.broadcasted_iota(jnp.int32, (tk, tq), 1)
                s = jnp.where(key <= qry + diag_shift, s, NEG_BIG)
            s_ref[h] = s
            mx_ref[h] = jnp.max(s.reshape(tk // 8, 8, tq), axis=0)

    def softmax_pv(k, s_ref, mx_ref):
        start = block_start(k)

        def probs(h):
            m_prev = m_ref[h]
            m_new = jnp.maximum(m_prev, jnp.max(mx_ref[h], axis=0, keepdims=True))
            m_ref[h] = m_new
            alpha = jnp.exp2(m_prev - m_new)
            p = jnp.exp2(s_ref[h] - m_new[0:1, :]).astype(BF16)
            return alpha, p

        def accumulate(h, alpha, p):
            pair, half = divmod(h, HEADS_PER_BLOCK)
            vt = vt_ref[0, pair, half * HEAD_DIM:(half + 1) * HEAD_DIM, pl.ds(start, tk)]
            ones = jnp.where(ones_row, 1.0, 0.0).astype(BF16)
            pv = _dot(jnp.concatenate([vt, ones], axis=0), p)
            acc_ref[h] = acc_ref[h] * alpha[0:1, :] + pv

        ap = {}
        for t in range(N_HEADS + 1):
            if t < N_HEADS:
                ap[t] = probs(t)
            if t >= 1:
                accumulate(t - 1, *ap.pop(t - 1))

    n_blocks = n_main + 1
    scores(0, s0_ref, mx0_ref, masked=True)

    def pair_of_blocks(i, carry):
        k = 2 * i
        scores(k + 1, s1_ref, mx1_ref)
        softmax_pv(k, s0_ref, mx0_ref)
        scores(k + 2, s0_ref, mx0_ref)
        softmax_pv(k + 1, s1_ref, mx1_ref)
        return carry

    n_pairs = (n_blocks - 1) // 2
    lax.fori_loop(0, n_pairs, pair_of_blocks, 0)
    k_last = 2 * n_pairs

    @pl.when(n_blocks - k_last == 2)
    def _():
        scores(k_last + 1, s1_ref, mx1_ref)
        softmax_pv(k_last, s0_ref, mx0_ref)
        softmax_pv(k_last + 1, s1_ref, mx1_ref)

    @pl.when(n_blocks - k_last == 1)
    def _():
        softmax_pv(k_last, s0_ref, mx0_ref)


    for pair in range(N_HEADS // HEADS_PER_BLOCK):
        outs = []
        for half in range(HEADS_PER_BLOCK):
            acc = acc_ref[pair * HEADS_PER_BLOCK + half]
            outs.append(acc[:HEAD_DIM, :] / acc[HEAD_DIM:HEAD_DIM + 1, :])
        att = jnp.concatenate(outs, axis=0).T
        sl = slice(pair * LANES, (pair + 1) * LANES)
        o_ref[0, :, sl] = (att * ga_ref[0, :, sl].astype(F32)).astype(BF16)


def _attn_call(qt, k, vt, g, ga):
    B, n_pairs, _, S = qt.shape
    assert KV_TILE % Q_TILE == 0 and S % KV_TILE == 0
    tile = pl.BlockSpec((1, Q_TILE, D_ATT), lambda b, i: (b, i, 0))
    whole = lambda width: pl.BlockSpec((1, S, width), lambda b, i: (b, 0, 0))
    score_buf = pltpu.VMEM((N_HEADS, KV_TILE, Q_TILE), F32)
    stat_buf = pltpu.VMEM((N_HEADS, 8, Q_TILE), F32)
    return pl.pallas_call(
        _attn_kernel,
        name="attention",
        grid=(B, S // Q_TILE),
        in_specs=[pl.BlockSpec((1, n_pairs, LANES, Q_TILE), lambda b, i: (b, 0, 0, i)),
                  whole(D_ATT),
                  pl.BlockSpec((1, n_pairs, LANES, S), lambda b, i: (b, 0, 0, 0)),
                  whole(LANES), tile],
        out_specs=tile,
        out_shape=jax.ShapeDtypeStruct((B, S, D_ATT), BF16),
        scratch_shapes=[stat_buf, pltpu.VMEM((N_HEADS, PV_ROWS, Q_TILE), F32),
                        score_buf, score_buf, stat_buf, stat_buf,
                        pltpu.VMEM((N_HEADS, 2 * LANES, Q_TILE), BF16)],
        compiler_params=pltpu.CompilerParams(
            dimension_semantics=("parallel", "parallel"),
            vmem_limit_bytes=VMEM_LIMIT_BYTES),
    )(qt, k, vt, g, ga)


def _out_kernel(x_ref, ada_ref, ya_ref, yp_ref, woa_ref, wop_ref, bo_ref, lg_ref, lb_ref,
                o_ref):
    gate = ada_ref[0][2:3]
    y = _dot(ya_ref[0], woa_ref[...]) + _dot(yp_ref[0], wop_ref[...]) + bo_ref[...]
    h = DEEPNORM_ALPHA * x_ref[0] + gate * y
    mu = jnp.mean(h, axis=-1, keepdims=True)
    d = h - mu
    var = jnp.mean(d * d, axis=-1, keepdims=True)
    o_ref[0] = d * lax.rsqrt(var + LN_EPS) * lg_ref[...] + lb_ref[...]


def _out_call(x, ada3, ya, yp, wo_a, wo_p, b_out, ln_g, ln_b):
    B, S, D = x.shape
    ts = SEQ_TILE
    const2 = lambda b, s: (0, 0)
    tile = lambda width: pl.BlockSpec((1, ts, width), lambda b, s: (b, s, 0))
    return pl.pallas_call(
        _out_kernel,
        name="output_stage",
        grid=(B, S // ts),
        in_specs=[tile(D), pl.BlockSpec((1, 3, D), lambda b, s: (b, 0, 0)),
                  tile(D_ATT), tile(D_POOL),
                  pl.BlockSpec(wo_a.shape, const2), pl.BlockSpec(wo_p.shape, const2),
                  pl.BlockSpec(b_out.shape, const2), pl.BlockSpec(ln_g.shape, const2),
                  pl.BlockSpec(ln_b.shape, const2)],
        out_specs=tile(D),
        out_shape=jax.ShapeDtypeStruct((B, S, D), F32),
        compiler_params=pltpu.CompilerParams(
            dimension_semantics=("parallel", "parallel"),
            vmem_limit_bytes=VMEM_LIMIT_BYTES),
    )(x, ada3, ya, yp, wo_a, wo_p, b_out, ln_g, ln_b)


def _layer(x, c, w_ada, b_ada, w_in, b_in, w_pool_mix, b_pool_mix, pool_scale,
           w_out, b_out, ln_g, ln_b):
    B, S, D = x.shape
    c_pad = jnp.pad(c, ((0, 8 - B), (0, 0)))
    ada = _ada_call(c_pad, w_ada, b_ada[None, :])
    ada3 = ada[:B].reshape(B, 3, D)

    edges = (0, D_ATT, 2 * D_ATT, 3 * D_ATT, 3 * D_ATT + N_HEADS,
             3 * D_ATT + N_HEADS + D_POOL, 3 * D_ATT + N_HEADS + D_POOL + D_ATT,
             3 * D_ATT + N_HEADS + 2 * D_POOL + D_ATT)
    seg_w = [w_in[:, a:b] for a, b in zip(edges[:-1], edges[1:])]
    seg_b = [b_in[None, a:b] for a, b in zip(edges[:-1], edges[1:])]
    seg_w[3] = jnp.pad(seg_w[3], ((0, 0), (0, LANES - N_HEADS)))
    seg_b[3] = jnp.pad(seg_b[3], ((0, 0), (0, LANES - N_HEADS)))
    seg_w = [w.astype(BF16) for w in seg_w]

    gd = POOL_GROUP_DIM
    w_pm = jnp.zeros((len(POOL_WINDOWS) // 2, 2 * gd, 2 * gd), BF16)
    for g in range(len(POOL_WINDOWS)):
        o = (g % 2) * gd
        w_pm = w_pm.at[g // 2, o:o + gd, o:o + gd].set(w_pool_mix[g].astype(BF16))

    qt, k, vt, g, ga, yp = _in_call(x, ada3, seg_w, seg_b, w_pm,
                                    b_pool_mix.reshape(1, D_POOL), pool_scale[None, :])
    ya = _attn_call(qt, k, vt, g, ga)
    wo = w_out.astype(BF16)
    return _out_call(x, ada3, ya, yp, wo[:D_ATT], wo[D_ATT:], b_out[None, :],
                     ln_g[None, :], ln_b[None, :])


def kernel(x, c, w_ada, b_ada, w_in, b_in, w_pool_mix, b_pool_mix, pool_scale, w_out, b_out,
           ln_g, ln_b):
    for layer in range(w_ada.shape[0]):
        x = _layer(x, c, w_ada[layer], b_ada[layer], w_in[layer], b_in[layer],
                   w_pool_mix[layer], b_pool_mix[layer], pool_scale[layer],
                   w_out[layer], b_out[layer], ln_g[layer], ln_b[layer])
    return x
```

```python
import functools
import math

import jax
import jax.numpy as jnp
from jax import lax
from jax.experimental import pallas as pl
from jax.experimental.pallas import tpu as pltpu

D_MODEL = 1024
D_ATT = 512
D_POOL = 512
N_HEADS = 8
HEAD_DIM = 64
POOL_WINDOWS = (2, 4, 8, 16)
POOL_GROUP_DIM = 128
POOL_HISTORY = 16
LN_EPS = 1e-5
DEEPNORM_ALPHA = 2.0 ** 0.25
LOG2E = math.log2(math.e)
NEG_BIG = -1e30

LANES = 128
HEADS_PER_BLOCK = LANES // HEAD_DIM
G_TERMS = 3
CUM_ROWS = 128
PV_ROWS = HEAD_DIM + 16

SEQ_TILE = 512
Q_TILE = 256
KV_TILE = 512
ADA_COL_TILE = 512
VMEM_LIMIT_BYTES = 56 * 1024 * 1024

F32 = jnp.float32
BF16 = jnp.bfloat16


def _silu(x):
    return x * jax.nn.sigmoid(x)


def _dot(a, b):
    return jnp.dot(a, b, preferred_element_type=F32)


def _ada_kernel(c_ref, w_ref, b_ref, o_ref):
    sc = _silu(c_ref[...])
    o_ref[...] = jnp.dot(sc, w_ref[...], preferred_element_type=F32,
                         precision=lax.Precision.HIGHEST) + b_ref[...]


def _ada_call(c_pad, w_ada, b_ada):
    rows, d = c_pad.shape
    n = w_ada.shape[1]
    return pl.pallas_call(
        _ada_kernel,
        name="ada_vector",
        grid=(n // ADA_COL_TILE,),
        in_specs=[
            pl.BlockSpec((rows, d), lambda j: (0, 0)),
            pl.BlockSpec((d, ADA_COL_TILE), lambda j: (0, j)),
            pl.BlockSpec((1, ADA_COL_TILE), lambda j: (0, j)),
        ],
        out_specs=pl.BlockSpec((rows, ADA_COL_TILE), lambda j: (0, j)),
        out_shape=jax.ShapeDtypeStruct((rows, n), F32),
        compiler_params=pltpu.CompilerParams(dimension_semantics=("parallel",)),
    )(c_pad, w_ada, b_ada)


def _split3(x):
    hi = x.astype(BF16).astype(F32)
    r = x - hi
    mid = r.astype(BF16).astype(F32)
    lo = (r - mid).astype(BF16).astype(F32)
    return hi, mid, lo


def _in_kernel(x_ref, ada_ref, wq_ref, wk_ref, wv_ref, wf_ref, wp_ref, wga_ref, wgp_ref,
               bq_ref, bk_ref, bv_ref, bf_ref, bp_ref, bga_ref, bgp_ref,
               wpm_ref, bpm_ref, ps_ref,
               q_ref, k_ref, v_ref, g_ref, ga_ref, yp_ref,
               fcarry_ref, pcarry_ref):
    si = pl.program_id(1)
    ts = x_ref.shape[1]

    @pl.when(si == 0)
    def _():
        fcarry_ref[...] = jnp.zeros_like(fcarry_ref)
        pcarry_ref[...] = jnp.zeros_like(pcarry_ref)

    ada = ada_ref[0]
    u = (x_ref[0] * (1.0 + ada[1:2]) + ada[0:1]).astype(BF16)

    q = (_dot(u, wq_ref[...]) + bq_ref[...]) * (HEAD_DIM ** -0.5 * LOG2E)
    for blk in range(D_ATT // LANES):
        q_ref[0, blk] = q[:, blk * LANES:(blk + 1) * LANES].T.astype(BF16)

    k_ref[0] = (_dot(u, wk_ref[...]) + bk_ref[...]).astype(BF16)

    v = _dot(u, wv_ref[...]) + bv_ref[...]
    for blk in range(D_ATT // LANES):
        v_ref[0, blk] = v[:, blk * LANES:(blk + 1) * LANES].T.astype(BF16)

    fl = _dot(u, wf_ref[...]) + bf_ref[...]
    logf = jnp.minimum(fl, 0.0) - jnp.log1p(jnp.exp(-jnp.abs(fl)))
    row = lax.broadcasted_iota(jnp.int32, (CUM_ROWS, CUM_ROWS), 0)
    col = lax.broadcasted_iota(jnp.int32, (CUM_ROWS, CUM_ROWS), 1)
    tri = jnp.where(row >= col, 1.0, 0.0).astype(BF16)
    terms = jnp.concatenate([t.astype(BF16) for t in _split3(logf)], axis=1)
    offset = fcarry_ref[0:1, :]
    cum_blocks = []
    for r in range(ts // CUM_ROWS):
        part = _dot(tri, terms[r * CUM_ROWS:(r + 1) * CUM_ROWS, :])
        local = part[:, :LANES] + part[:, LANES:2 * LANES] + part[:, 2 * LANES:]
        cum_blocks.append(local + offset)
        offset = cum_blocks[-1][CUM_ROWS - 1:CUM_ROWS, :]
    fcarry_ref[...] = jnp.broadcast_to(offset, fcarry_ref.shape)
    cum = jnp.concatenate(cum_blocks, axis=0)

    g_hi, g_mid, g_lo = _split3(cum * (-LOG2E))
    lane = lax.broadcasted_iota(jnp.int32, (ts, LANES), 1)
    g = jnp.where(lane < N_HEADS, g_hi,
                  jnp.where(lane < 2 * N_HEADS, pltpu.roll(g_mid, N_HEADS, axis=1),
                            jnp.where(lane < 3 * N_HEADS, pltpu.roll(g_lo, 2 * N_HEADS, axis=1),
                                      0.0)))
    g_ref[0] = g.astype(BF16)

    p = _dot(u, wp_ref[...]) + bp_ref[...]
    pe = jnp.concatenate([pcarry_ref[...], p], axis=0)
    pcarry_ref[...] = p[ts - POOL_HISTORY:, :]
    gp = _dot(u, wgp_ref[...]) + bgp_ref[...]
    t_glob = si * ts + lax.broadcasted_iota(jnp.int32, (ts, POOL_GROUP_DIM), 0)
    pooled = []
    for g, w in enumerate(POOL_WINDOWS):
        sl = slice(g * POOL_GROUP_DIM, (g + 1) * POOL_GROUP_DIM)
        y = pe[:, sl]
        sh = 1
        while sh < w:
            y = y + pltpu.roll(y, sh, axis=0)
            sh *= 2
        cnt = jnp.minimum(t_glob + 1, w).astype(F32)
        pooled.append((y[POOL_HISTORY:, :] / cnt - p[:, sl]).astype(BF16))
    for j in range(len(POOL_WINDOWS) // 2):
        sl = slice(2 * j * POOL_GROUP_DIM, (2 * j + 2) * POOL_GROUP_DIM)
        mixed = _dot(jnp.concatenate(pooled[2 * j:2 * j + 2], axis=1), wpm_ref[j]) + bpm_ref[:, sl]
        yp_ref[0, :, sl] = (mixed * ps_ref[:, sl] * _silu(gp[:, sl])).astype(BF16)

    ga_ref[0] = _silu(_dot(u, wga_ref[...]) + bga_ref[...]).astype(BF16)


def _in_call(x, ada3, wts, biases, w_pm, b_pm, pool_scale):
    B, S, D = x.shape
    ts = SEQ_TILE
    const2 = lambda b, s: (0, 0)
    w_specs = [pl.BlockSpec(w.shape, const2) for w in wts]
    b_specs = [pl.BlockSpec(bb.shape, const2) for bb in biases]
    tile = lambda width: pl.BlockSpec((1, ts, width), lambda b, s: (b, s, 0))
    n_pairs = D_ATT // LANES
    tile_t = pl.BlockSpec((1, n_pairs, LANES, ts), lambda b, s: (b, 0, 0, s))
    return pl.pallas_call(
        _in_kernel,
        name="input_stage",
        grid=(B, S // ts),
        in_specs=[tile(D), pl.BlockSpec((1, 3, D), lambda b, s: (b, 0, 0))]
                 + w_specs + b_specs
                 + [pl.BlockSpec(w_pm.shape, lambda b, s: (0, 0, 0)),
                    pl.BlockSpec(b_pm.shape, const2),
                    pl.BlockSpec(pool_scale.shape, const2)],
        out_specs=[tile_t, tile(D_ATT), tile_t, tile(LANES),
                   tile(D_ATT), tile(D_POOL)],
        out_shape=[jax.ShapeDtypeStruct((B, n_pairs, LANES, S), BF16),
                   jax.ShapeDtypeStruct((B, S, D_ATT), BF16),
                   jax.ShapeDtypeStruct((B, n_pairs, LANES, S), BF16),
                   jax.ShapeDtypeStruct((B, S, LANES), BF16),
                   jax.ShapeDtypeStruct((B, S, D_ATT), BF16),
                   jax.ShapeDtypeStruct((B, S, D_POOL), BF16)],
        scratch_shapes=[pltpu.VMEM((8, LANES), F32),
                        pltpu.VMEM((POOL_HISTORY, D_POOL), F32)],
        compiler_params=pltpu.CompilerParams(
            dimension_semantics=("arbitrary", "arbitrary"),
            vmem_limit_bytes=VMEM_LIMIT_BYTES),
    )(x, ada3, *wts, *biases, w_pm, b_pm, pool_scale)


def _attn_kernel(qt_ref, k_ref, vt_ref, g_ref, ga_ref, yp_ref, x_ref, ada_ref,
                 woa_ref, wop_ref, bo_ref, lg_ref, lb_ref, o_ref,
                 m_ref, acc_ref, s0_ref, s1_ref, mx0_ref, mx1_ref, rhs_ref):
    qi = pl.program_id(1)
    tq = qt_ref.shape[3]
    tk = KV_TILE
    n_main = (qi * tq) // tk
    diag_start = pl.multiple_of(n_main * tk, tk)
    diag_shift = qi * tq - diag_start

    m_ref[...] = jnp.full_like(m_ref, NEG_BIG)
    acc_ref[...] = jnp.zeros_like(acc_ref)

    row = lax.broadcasted_iota(jnp.int32, (LANES, tq), 0)
    for h in range(N_HEADS):
        pair, half = divmod(h, HEADS_PER_BLOCK)
        own = (row // HEAD_DIM) == half
        rhs_ref[h, :LANES, :] = jnp.where(own, qt_ref[0, pair], jnp.zeros((), BF16))
        pick = (row < G_TERMS * N_HEADS) & (row % N_HEADS == h)
        rhs_ref[h, LANES:, :] = jnp.where(pick, 1.0, 0.0).astype(BF16)
    ones_row = lax.broadcasted_iota(jnp.int32, (PV_ROWS - HEAD_DIM, tk), 0) == 0

    def block_start(k):
        return pl.multiple_of(jnp.where(k == 0, diag_start, (k - 1) * tk), tk)

    def scores(k, s_ref, mx_ref, masked=False):
        start = block_start(k)
        gblk = g_ref[0, pl.ds(start, tk), :]
        for h in range(N_HEADS):
            pair = h // HEADS_PER_BLOCK
            kblk = k_ref[0, pl.ds(start, tk), pair * LANES:(pair + 1) * LANES]
            lhs = jnp.concatenate([kblk, gblk], axis=1)
            s = _dot(lhs, rhs_ref[h])
            if masked:
                key = lax.broadcasted_iota(jnp.int32, (tk, tq), 0)
                qry = lax.broadcasted_iota(jnp.int32, (tk, tq), 1)
                s = jnp.where(key <= qry + diag_shift, s, NEG_BIG)
            s_ref[h] = s
            mx_ref[h] = jnp.max(s.reshape(tk // 8, 8, tq), axis=0)

    def softmax_pv(k, s_ref, mx_ref):
        start = block_start(k)

        def probs(h):
            m_prev = m_ref[h]
            m_new = jnp.maximum(m_prev, jnp.max(mx_ref[h], axis=0, keepdims=True))
            m_ref[h] = m_new
            alpha = jnp.exp2(m_prev - m_new)
            p = jnp.exp2(s_ref[h] - m_new[0:1, :]).astype(BF16)
            return alpha, p

        def accumulate(h, alpha, p):
            pair, half = divmod(h, HEADS_PER_BLOCK)
            vt = vt_ref[0, pair, half * HEAD_DIM:(half + 1) * HEAD_DIM, pl.ds(start, tk)]
            ones = jnp.where(ones_row, 1.0, 0.0).astype(BF16)
            pv = _dot(jnp.concatenate([vt, ones], axis=0), p)
            acc_ref[h] = acc_ref[h] * alpha[0:1, :] + pv

        ap = {}
        for t in range(N_HEADS + 1):
            if t < N_HEADS:
                ap[t] = probs(t)
            if t >= 1:
                accumulate(t - 1, *ap.pop(t - 1))

    n_blocks = n_main + 1
    scores(0, s0_ref, mx0_ref, masked=True)

    def pair_of_blocks(i, carry):
        k = 2 * i
        scores(k + 1, s1_ref, mx1_ref)
        softmax_pv(k, s0_ref, mx0_ref)
        scores(k + 2, s0_ref, mx0_ref)
        softmax_pv(k + 1, s1_ref, mx1_ref)
        return carry

    n_pairs = (n_blocks - 1) // 2
    lax.fori_loop(0, n_pairs, pair_of_blocks, 0)
    k_last = 2 * n_pairs

    @pl.when(n_blocks - k_last == 2)
    def _():
        scores(k_last + 1, s1_ref, mx1_ref)
        softmax_pv(k_last, s0_ref, mx0_ref)
        softmax_pv(k_last + 1, s1_ref, mx1_ref)

    @pl.when(n_blocks - k_last == 1)
    def _():
        softmax_pv(k_last, s0_ref, mx0_ref)


    gated = []
    for pair in range(N_HEADS // HEADS_PER_BLOCK):
        outs = []
        for half in range(HEADS_PER_BLOCK):
            acc = acc_ref[pair * HEADS_PER_BLOCK + half]
            outs.append(acc[:HEAD_DIM, :] / acc[HEAD_DIM:HEAD_DIM + 1, :])
        att = jnp.concatenate(outs, axis=0).T
        sl = slice(pair * LANES, (pair + 1) * LANES)
        gated.append((att * ga_ref[0, :, sl].astype(F32)).astype(BF16))
    ya = jnp.concatenate(gated, axis=1)
    y = _dot(ya, woa_ref[...]) + _dot(yp_ref[0], wop_ref[...]) + bo_ref[...]
    hres = DEEPNORM_ALPHA * x_ref[0] + ada_ref[0][2:3] * y
    mu = jnp.mean(hres, axis=-1, keepdims=True)
    d = hres - mu
    var = jnp.mean(d * d, axis=-1, keepdims=True)
    o_ref[0] = d * lax.rsqrt(var + LN_EPS) * lg_ref[...] + lb_ref[...]


def _attn_call(qt, k, vt, g, ga, yp, x, ada3, wo_a, wo_p, b_out, ln_g, ln_b):
    B, n_pairs, _, S = qt.shape
    D = x.shape[2]
    assert KV_TILE % Q_TILE == 0 and S % KV_TILE == 0
    tile = lambda width: pl.BlockSpec((1, Q_TILE, width), lambda b, i: (b, i, 0))
    whole = lambda width: pl.BlockSpec((1, S, width), lambda b, i: (b, 0, 0))
    const2 = lambda a: pl.BlockSpec(a.shape, lambda b, i: (0, 0))
    score_buf = pltpu.VMEM((N_HEADS, KV_TILE, Q_TILE), F32)
    stat_buf = pltpu.VMEM((N_HEADS, 8, Q_TILE), F32)
    return pl.pallas_call(
        _attn_kernel,
        name="attention",
        grid=(B, S // Q_TILE),
        in_specs=[pl.BlockSpec((1, n_pairs, LANES, Q_TILE), lambda b, i: (b, 0, 0, i)),
                  whole(D_ATT),
                  pl.BlockSpec((1, n_pairs, LANES, S), lambda b, i: (b, 0, 0, 0)),
                  whole(LANES), tile(D_ATT), tile(D_POOL), tile(D),
                  pl.BlockSpec((1, 3, D), lambda b, i: (b, 0, 0)),
                  const2(wo_a), const2(wo_p), const2(b_out), const2(ln_g), const2(ln_b)],
        out_specs=tile(D),
        out_shape=jax.ShapeDtypeStruct((B, S, D), F32),
        scratch_shapes=[stat_buf, pltpu.VMEM((N_HEADS, PV_ROWS, Q_TILE), F32),
                        score_buf, score_buf, stat_buf, stat_buf,
                        pltpu.VMEM((N_HEADS, 2 * LANES, Q_TILE), BF16)],
        compiler_params=pltpu.CompilerParams(
            dimension_semantics=("parallel", "parallel"),
            vmem_limit_bytes=VMEM_LIMIT_BYTES),
    )(qt, k, vt, g, ga, yp, x, ada3, wo_a, wo_p, b_out, ln_g, ln_b)


def _layer(x, c, w_ada, b_ada, w_in, b_in, w_pool_mix, b_pool_mix, pool_scale,
           w_out, b_out, ln_g, ln_b):
    B, S, D = x.shape
    c_pad = jnp.pad(c, ((0, 8 - B), (0, 0)))
    ada = _ada_call(c_pad, w_ada, b_ada[None, :])
    ada3 = ada[:B].reshape(B, 3, D)

    edges = (0, D_ATT, 2 * D_ATT, 3 * D_ATT, 3 * D_ATT + N_HEADS,
             3 * D_ATT + N_HEADS + D_POOL, 3 * D_ATT + N_HEADS + D_POOL + D_ATT,
             3 * D_ATT + N_HEADS + 2 * D_POOL + D_ATT)
    seg_w = [w_in[:, a:b] for a, b in zip(edges[:-1], edges[1:])]
    seg_b = [b_in[None, a:b] for a, b in zip(edges[:-1], edges[1:])]
    seg_w[3] = jnp.pad(seg_w[3], ((0, 0), (0, LANES - N_HEADS)))
    seg_b[3] = jnp.pad(seg_b[3], ((0, 0), (0, LANES - N_HEADS)))
    seg_w = [w.astype(BF16) for w in seg_w]

    gd = POOL_GROUP_DIM
    w_pm = jnp.zeros((len(POOL_WINDOWS) // 2, 2 * gd, 2 * gd), BF16)
    for g in range(len(POOL_WINDOWS)):
        o = (g % 2) * gd
        w_pm = w_pm.at[g // 2, o:o + gd, o:o + gd].set(w_pool_mix[g].astype(BF16))

    qt, k, vt, g, ga, yp = _in_call(x, ada3, seg_w, seg_b, w_pm,
                                    b_pool_mix.reshape(1, D_POOL), pool_scale[None, :])
    wo = w_out.astype(BF16)
    return _attn_call(qt, k, vt, g, ga, yp, x, ada3, wo[:D_ATT], wo[D_ATT:], b_out[None, :],
                      ln_g[None, :], ln_b[None, :])


def kernel(x, c, w_ada, b_ada, w_in, b_in, w_pool_mix, b_pool_mix, pool_scale, w_out, b_out,
           ln_g, ln_b):
    for layer in range(w_ada.shape[0]):
        x = _layer(x, c, w_ada[layer], b_ada[layer], w_in[layer], b_in[layer],
                   w_pool_mix[layer], b_pool_mix[layer], pool_scale[layer],
                   w_out[layer], b_out[layer], ln_g[layer], ln_b[layer])
    return x
```

```python
import functools
import math

import jax
import jax.numpy as jnp
from jax import lax
from jax.experimental import pallas as pl
from jax.experimental.pallas import tpu as pltpu

D_MODEL = 1024
D_ATT = 512
D_POOL = 512
N_HEADS = 8
HEAD_DIM = 64
POOL_WINDOWS = (2, 4, 8, 16)
POOL_GROUP_DIM = 128
POOL_HISTORY = 16
LN_EPS = 1e-5
DEEPNORM_ALPHA = 2.0 ** 0.25
LOG2E = math.log2(math.e)
NEG_BIG = -1e30

LANES = 128
HEADS_PER_BLOCK = LANES // HEAD_DIM
G_TERMS = 3
CUM_ROWS = 128
PV_ROWS = HEAD_DIM + 16

IN_WIDTH = {"f": LANES, "p": D_POOL, "gp": D_POOL, "q": D_ATT, "v": D_ATT, "k": D_ATT,
            "ga": D_ATT}
IN_OFFSET = dict(zip(IN_WIDTH, (sum(list(IN_WIDTH.values())[:i]) for i in range(len(IN_WIDTH)))))

SEQ_TILE = 512
Q_TILE = 256
KV_TILE = 512
ADA_COL_TILE = 512
VMEM_LIMIT_BYTES = 56 * 1024 * 1024

F32 = jnp.float32
BF16 = jnp.bfloat16


def _silu(x):
    return x * jax.nn.sigmoid(x)


def _dot(a, b):
    return jnp.dot(a, b, preferred_element_type=F32)


def _ada_kernel(ct_ref, w_ref, b_ref, o_ref):
    sc = _silu(ct_ref[...])
    for b in range(o_ref.shape[0]):
        o_ref[b:b + 1, :] = (jnp.sum(w_ref[...] * sc[:, b:b + 1], axis=0, keepdims=True)
                             + b_ref[...])


def _ada_call(c_t, n_rows, w_ada, b_ada):
    d, n = w_ada.shape
    return pl.pallas_call(
        _ada_kernel,
        name="ada_vector",
        grid=(n // ADA_COL_TILE,),
        in_specs=[
            pl.BlockSpec(c_t.shape, lambda j: (0, 0)),
            pl.BlockSpec((d, ADA_COL_TILE), lambda j: (0, j)),
            pl.BlockSpec((1, ADA_COL_TILE), lambda j: (0, j)),
        ],
        out_specs=pl.BlockSpec((n_rows, ADA_COL_TILE), lambda j: (0, j)),
        out_shape=jax.ShapeDtypeStruct((n_rows, n), F32),
        compiler_params=pltpu.CompilerParams(dimension_semantics=("parallel",)),
    )(c_t, w_ada, b_ada)


def _split3(x):
    hi = x.astype(BF16).astype(F32)
    r = x - hi
    mid = r.astype(BF16).astype(F32)
    lo = (r - mid).astype(BF16).astype(F32)
    return hi, mid, lo


def _in_kernel(x_ref, ada_ref, w_ref, b_ref, wpm_ref, bpm_ref, ps_ref,
               q_ref, k_ref, v_ref, g_ref, ga_ref, yp_ref,
               fcarry_ref, pcarry_ref):
    si = pl.program_id(1)
    ts = x_ref.shape[1]

    @pl.when(si == 0)
    def _():
        fcarry_ref[...] = jnp.zeros_like(fcarry_ref)
        pcarry_ref[...] = jnp.zeros_like(pcarry_ref)

    ada = ada_ref[0]
    u = (x_ref[0] * (1.0 + ada[1:2]) + ada[0:1]).astype(BF16)
    proj = _dot(u, w_ref[...]) + b_ref[...]
    seg = lambda name: proj[:, IN_OFFSET[name]:IN_OFFSET[name] + IN_WIDTH[name]]

    q = seg("q") * (HEAD_DIM ** -0.5 * LOG2E)
    for blk in range(D_ATT // LANES):
        q_ref[0, blk] = q[:, blk * LANES:(blk + 1) * LANES].T.astype(BF16)

    k_ref[0] = seg("k").astype(BF16)

    v = seg("v")
    for blk in range(D_ATT // LANES):
        v_ref[0, blk] = v[:, blk * LANES:(blk + 1) * LANES].T.astype(BF16)

    fl = seg("f")
    logf = jnp.minimum(fl, 0.0) - jnp.log1p(jnp.exp(-jnp.abs(fl)))
    row = lax.broadcasted_iota(jnp.int32, (CUM_ROWS, CUM_ROWS), 0)
    col = lax.broadcasted_iota(jnp.int32, (CUM_ROWS, CUM_ROWS), 1)
    tri = jnp.where(row >= col, 1.0, 0.0).astype(BF16)
    terms = jnp.concatenate([t.astype(BF16) for t in _split3(logf)], axis=1)
    offset = fcarry_ref[0:1, :]
    cum_blocks = []
    for r in range(ts // CUM_ROWS):
        part = _dot(tri, terms[r * CUM_ROWS:(r + 1) * CUM_ROWS, :])
        local = part[:, :LANES] + part[:, LANES:2 * LANES] + part[:, 2 * LANES:]
        cum_blocks.append(local + offset)
        offset = cum_blocks[-1][CUM_ROWS - 1:CUM_ROWS, :]
    fcarry_ref[...] = jnp.broadcast_to(offset, fcarry_ref.shape)
    cum = jnp.concatenate(cum_blocks, axis=0)

    g_hi, g_mid, g_lo = _split3(cum * (-LOG2E))
    lane = lax.broadcasted_iota(jnp.int32, (ts, LANES), 1)
    g = jnp.where(lane < N_HEADS, g_hi,
                  jnp.where(lane < 2 * N_HEADS, pltpu.roll(g_mid, N_HEADS, axis=1),
                            jnp.where(lane < 3 * N_HEADS, pltpu.roll(g_lo, 2 * N_HEADS, axis=1),
                                      0.0)))
    g_ref[0] = g.astype(BF16)

    p = seg("p")
    pe = jnp.concatenate([pcarry_ref[...], p], axis=0)
    pcarry_ref[...] = p[ts - POOL_HISTORY:, :]
    gp = seg("gp")
    t_glob = si * ts + lax.broadcasted_iota(jnp.int32, (ts, POOL_GROUP_DIM), 0)
    pooled = []
    for g, w in enumerate(POOL_WINDOWS):
        sl = slice(g * POOL_GROUP_DIM, (g + 1) * POOL_GROUP_DIM)
        y = pe[:, sl]
        sh = 1
        while sh < w:
            y = y + pltpu.roll(y, sh, axis=0)
            sh *= 2
        cnt = jnp.minimum(t_glob + 1, w).astype(F32)
        pooled.append((y[POOL_HISTORY:, :] / cnt - p[:, sl]).astype(BF16))
    for j in range(len(POOL_WINDOWS) // 2):
        sl = slice(2 * j * POOL_GROUP_DIM, (2 * j + 2) * POOL_GROUP_DIM)
        mixed = _dot(jnp.concatenate(pooled[2 * j:2 * j + 2], axis=1), wpm_ref[j]) + bpm_ref[:, sl]
        yp_ref[0, :, sl] = (mixed * ps_ref[:, sl] * _silu(gp[:, sl])).astype(BF16)

    ga_ref[0] = _silu(seg("ga")).astype(BF16)


def _in_call(x, ada3, w_cat, b_cat, w_pm, b_pm, pool_scale):
    B, S, D = x.shape
    ts = SEQ_TILE
    const2 = lambda b, s: (0, 0)
    tile = lambda width: pl.BlockSpec((1, ts, width), lambda b, s: (b, s, 0))
    n_pairs = D_ATT // LANES
    tile_t = pl.BlockSpec((1, n_pairs, LANES, ts), lambda b, s: (b, 0, 0, s))
    return pl.pallas_call(
        _in_kernel,
        name="input_stage",
        grid=(B, S // ts),
        in_specs=[tile(D), pl.BlockSpec((1, 3, D), lambda b, s: (b, 0, 0)),
                  pl.BlockSpec(w_cat.shape, const2), pl.BlockSpec(b_cat.shape, const2),
                  pl.BlockSpec(w_pm.shape, lambda b, s: (0, 0, 0)),
                  pl.BlockSpec(b_pm.shape, const2),
                  pl.BlockSpec(pool_scale.shape, const2)],
        out_specs=[tile_t, tile(D_ATT), tile_t, tile(LANES),
                   tile(D_ATT), tile(D_POOL)],
        out_shape=[jax.ShapeDtypeStruct((B, n_pairs, LANES, S), BF16),
                   jax.ShapeDtypeStruct((B, S, D_ATT), BF16),
                   jax.ShapeDtypeStruct((B, n_pairs, LANES, S), BF16),
                   jax.ShapeDtypeStruct((B, S, LANES), BF16),
                   jax.ShapeDtypeStruct((B, S, D_ATT), BF16),
                   jax.ShapeDtypeStruct((B, S, D_POOL), BF16)],
        scratch_shapes=[pltpu.VMEM((8, LANES), F32),
                        pltpu.VMEM((POOL_HISTORY, D_POOL), F32)],
        compiler_params=pltpu.CompilerParams(
            dimension_semantics=("arbitrary", "arbitrary"),
            vmem_limit_bytes=VMEM_LIMIT_BYTES),
    )(x, ada3, w_cat, b_cat, w_pm, b_pm, pool_scale)


def _attn_kernel(qt_ref, k_ref, vt_ref, g_ref, ga_ref, yp_ref, x_ref, ada_ref,
                 wo_ref, bo_ref, lg_ref, lb_ref, o_ref,
                 m_ref, acc_ref, s0_ref, s1_ref, mx0_ref, mx1_ref, rhs_ref):
    qi = pl.program_id(1)
    tq = qt_ref.shape[3]
    tk = KV_TILE
    n_main = (qi * tq) // tk
    diag_start = pl.multiple_of(n_main * tk, tk)
    diag_shift = qi * tq - diag_start

    m_ref[...] = jnp.full_like(m_ref, NEG_BIG)
    acc_ref[...] = jnp.zeros_like(acc_ref)

    row = lax.broadcasted_iota(jnp.int32, (LANES, tq), 0)
    for h in range(N_HEADS):
        pair, half = divmod(h, HEADS_PER_BLOCK)
        own = (row // HEAD_DIM) == half
        rhs_ref[h, :LANES, :] = jnp.where(own, qt_ref[0, pair], jnp.zeros((), BF16))
        pick = (row < G_TERMS * N_HEADS) & (row % N_HEADS == h)
        rhs_ref[h, LANES:, :] = jnp.where(pick, 1.0, 0.0).astype(BF16)
    ones_row = lax.broadcasted_iota(jnp.int32, (PV_ROWS - HEAD_DIM, tk), 0) == 0

    def block_start(k):
        return pl.multiple_of(jnp.where(k == 0, diag_start, (k - 1) * tk), tk)

    def scores(k, s_ref, mx_ref, masked=False):
        start = block_start(k)
        gblk = g_ref[0, pl.ds(start, tk), :]
        for h in range(N_HEADS):
            pair = h // HEADS_PER_BLOCK
            kblk = k_ref[0, pl.ds(start, tk), pair * LANES:(pair + 1) * LANES]
            lhs = jnp.concatenate([kblk, gblk], axis=1)
            s = _dot(lhs, rhs_ref[h])
            if masked:
                key = lax.broadcasted_iota(jnp.int32, (tk, tq), 0)
                qry = lax.broadcasted_iota(jnp.int32, (tk, tq), 1)
                s = jnp.where(key <= qry + diag_shift, s, NEG_BIG)
            s_ref[h] = s
            mx_ref[h] = jnp.max(s.reshape(tk // 8, 8, tq), axis=0)

    def softmax_pv(k, s_ref, mx_ref):
        start = block_start(k)

        def probs(h):
            m_prev = m_ref[h]
            m_new = jnp.maximum(m_prev, jnp.max(mx_ref[h], axis=0, keepdims=True))
            m_ref[h] = m_new
            alpha = jnp.exp2(m_prev - m_new)
            p = jnp.exp2(s_ref[h] - m_new[0:1, :]).astype(BF16)
            return alpha, p

        def accumulate(h, alpha, p):
            pair, half = divmod(h, HEADS_PER_BLOCK)
            vt = vt_ref[0, pair, half * HEAD_DIM:(half + 1) * HEAD_DIM, pl.ds(start, tk)]
            ones = jnp.where(ones_row, 1.0, 0.0).astype(BF16)
            pv = _dot(jnp.concatenate([vt, ones], axis=0), p)
            acc_ref[h] = acc_ref[h] * alpha[0:1, :] + pv

        ap = {}
        for t in range(N_HEADS + 1):
            if t < N_HEADS:
                ap[t] = probs(t)
            if t >= 1:
                accumulate(t - 1, *ap.pop(t - 1))

    n_blocks = n_main + 1
    scores(0, s0_ref, mx0_ref, masked=True)

    def pair_of_blocks(i, carry):
        k = 2 * i
        scores(k + 1, s1_ref, mx1_ref)
        softmax_pv(k, s0_ref, mx0_ref)
        scores(k + 2, s0_ref, mx0_ref)
        softmax_pv(k + 1, s1_ref, mx1_ref)
        return carry

    n_pairs = (n_blocks - 1) // 2
    lax.fori_loop(0, n_pairs, pair_of_blocks, 0)
    k_last = 2 * n_pairs

    @pl.when(n_blocks - k_last == 2)
    def _():
        scores(k_last + 1, s1_ref, mx1_ref)
        softmax_pv(k_last, s0_ref, mx0_ref)
        softmax_pv(k_last + 1, s1_ref, mx1_ref)

    @pl.when(n_blocks - k_last == 1)
    def _():
        softmax_pv(k_last, s0_ref, mx0_ref)


    gated = []
    for pair in range(N_HEADS // HEADS_PER_BLOCK):
        outs = []
        for half in range(HEADS_PER_BLOCK):
            acc = acc_ref[pair * HEADS_PER_BLOCK + half]
            outs.append(acc[:HEAD_DIM, :] / acc[HEAD_DIM:HEAD_DIM + 1, :])
        att = jnp.concatenate(outs, axis=0).T
        sl = slice(pair * LANES, (pair + 1) * LANES)
        gated.append((att * ga_ref[0, :, sl].astype(F32)).astype(BF16))
    ya = jnp.concatenate(gated, axis=1)
    y = _dot(jnp.concatenate([ya, yp_ref[0]], axis=1), wo_ref[...]) + bo_ref[...]
    hres = DEEPNORM_ALPHA * x_ref[0] + ada_ref[0][2:3] * y
    mu = jnp.mean(hres, axis=-1, keepdims=True)
    d = hres - mu
    var = jnp.mean(d * d, axis=-1, keepdims=True)
    o_ref[0] = d * lax.rsqrt(var + LN_EPS) * lg_ref[...] + lb_ref[...]


def _attn_call(qt, k, vt, g, ga, yp, x, ada3, wo, b_out, ln_g, ln_b):
    B, n_pairs, _, S = qt.shape
    D = x.shape[2]
    assert KV_TILE % Q_TILE == 0 and S % KV_TILE == 0
    tile = lambda width: pl.BlockSpec((1, Q_TILE, width), lambda b, i: (b, i, 0))
    whole = lambda width: pl.BlockSpec((1, S, width), lambda b, i: (b, 0, 0))
    const2 = lambda a: pl.BlockSpec(a.shape, lambda b, i: (0, 0))
    score_buf = pltpu.VMEM((N_HEADS, KV_TILE, Q_TILE), F32)
    stat_buf = pltpu.VMEM((N_HEADS, 8, Q_TILE), F32)
    return pl.pallas_call(
        _attn_kernel,
        name="attention",
        grid=(B, S // Q_TILE),
        in_specs=[pl.BlockSpec((1, n_pairs, LANES, Q_TILE), lambda b, i: (b, 0, 0, i)),
                  whole(D_ATT),
                  pl.BlockSpec((1, n_pairs, LANES, S), lambda b, i: (b, 0, 0, 0)),
                  whole(LANES), tile(D_ATT), tile(D_POOL), tile(D),
                  pl.BlockSpec((1, 3, D), lambda b, i: (b, 0, 0)),
                  const2(wo), const2(b_out), const2(ln_g), const2(ln_b)],
        out_specs=tile(D),
        out_shape=jax.ShapeDtypeStruct((B, S, D), F32),
        scratch_shapes=[stat_buf, pltpu.VMEM((N_HEADS, PV_ROWS, Q_TILE), F32),
                        score_buf, score_buf, stat_buf, stat_buf,
                        pltpu.VMEM((N_HEADS, 2 * LANES, Q_TILE), BF16)],
        compiler_params=pltpu.CompilerParams(
            dimension_semantics=("parallel", "parallel"),
            vmem_limit_bytes=VMEM_LIMIT_BYTES),
    )(qt, k, vt, g, ga, yp, x, ada3, wo, b_out, ln_g, ln_b)


def _layer(x, c, w_ada, b_ada, w_in, b_in, w_pool_mix, b_pool_mix, pool_scale,
           w_out, b_out, ln_g, ln_b):
    B, S, D = x.shape
    c_t = jnp.pad(c.T, ((0, 0), (0, LANES - B)))
    ada3 = _ada_call(c_t, B, w_ada, b_ada[None, :]).reshape(B, 3, D)

    src, w_parts, b_parts = 0, {}, {}
    for name, width in (("q", D_ATT), ("k", D_ATT), ("v", D_ATT), ("f", N_HEADS),
                        ("p", D_POOL), ("ga", D_ATT), ("gp", D_POOL)):
        pad = ((0, 0), (0, IN_WIDTH[name] - width))
        w_parts[name] = jnp.pad(w_in[:, src:src + width], pad)
        b_parts[name] = jnp.pad(b_in[None, src:src + width], pad)
        src += width
    w_cat = jnp.concatenate([w_parts[n] for n in IN_WIDTH], axis=1).astype(BF16)
    b_cat = jnp.concatenate([b_parts[n] for n in IN_WIDTH], axis=1)

    gd = POOL_GROUP_DIM
    w_pm = jnp.zeros((len(POOL_WINDOWS) // 2, 2 * gd, 2 * gd), BF16)
    for g in range(len(POOL_WINDOWS)):
        o = (g % 2) * gd
        w_pm = w_pm.at[g // 2, o:o + gd, o:o + gd].set(w_pool_mix[g].astype(BF16))

    qt, k, vt, g, ga, yp = _in_call(x, ada3, w_cat, b_cat, w_pm,
                                    b_pool_mix.reshape(1, D_POOL), pool_scale[None, :])
    return _attn_call(qt, k, vt, g, ga, yp, x, ada3, w_out.astype(BF16), b_out[None, :],
                      ln_g[None, :], ln_b[None, :])


def kernel(x, c, w_ada, b_ada, w_in, b_in, w_pool_mix, b_pool_mix, pool_scale, w_out, b_out,
           ln_g, ln_b):
    for layer in range(w_ada.shape[0]):
        x = _layer(x, c, w_ada[layer], b_ada[layer], w_in[layer], b_in[layer],
                   w_pool_mix[layer], b_pool_mix[layer], pool_scale[layer],
                   w_out[layer], b_out[layer], ln_g[layer], ln_b[layer])
    return x
```

```python
import functools
import math

import jax
import jax.numpy as jnp
from jax import lax
from jax.experimental import pallas as pl
from jax.experimental.pallas import tpu as pltpu

D_MODEL = 1024
D_ATT = 512
D_POOL = 512
N_HEADS = 8
HEAD_DIM = 64
POOL_WINDOWS = (2, 4, 8, 16)
POOL_GROUP_DIM = 128
POOL_HISTORY = 16
LN_EPS = 1e-5
DEEPNORM_ALPHA = 2.0 ** 0.25
LOG2E = math.log2(math.e)
NEG_BIG = -1e30

LANES = 128
HEADS_PER_BLOCK = LANES // HEAD_DIM
G_TERMS = 3
CUM_ROWS = 128
PV_ROWS = HEAD_DIM + 16

IN_WIDTH = {"f": LANES, "p": D_POOL, "gp": D_POOL, "q": D_ATT, "v": D_ATT, "k": D_ATT,
            "ga": D_ATT}
IN_OFFSET = dict(zip(IN_WIDTH, (sum(list(IN_WIDTH.values())[:i]) for i in range(len(IN_WIDTH)))))

SEQ_TILE = 512
Q_TILE = 256
KV_TILE = 512
ADA_COL_TILE = 512
VMEM_LIMIT_BYTES = 56 * 1024 * 1024

F32 = jnp.float32
BF16 = jnp.bfloat16


def _silu(x):
    return x * jax.nn.sigmoid(x)


def _dot(a, b):
    return jnp.dot(a, b, preferred_element_type=F32)


def _ada_kernel(ct_ref, w_ref, b_ref, o_ref):
    sc = _silu(ct_ref[...])
    for b in range(o_ref.shape[0]):
        o_ref[b:b + 1, :] = (jnp.sum(w_ref[...] * sc[:, b:b + 1], axis=0, keepdims=True)
                             + b_ref[...])


def _ada_call(c_t, n_rows, w_ada, b_ada):
    d, n = w_ada.shape
    return pl.pallas_call(
        _ada_kernel,
        name="ada_vector",
        grid=(n // ADA_COL_TILE,),
        in_specs=[
            pl.BlockSpec(c_t.shape, lambda j: (0, 0)),
            pl.BlockSpec((d, ADA_COL_TILE), lambda j: (0, j)),
            pl.BlockSpec((1, ADA_COL_TILE), lambda j: (0, j)),
        ],
        out_specs=pl.BlockSpec((n_rows, ADA_COL_TILE), lambda j: (0, j)),
        out_shape=jax.ShapeDtypeStruct((n_rows, n), F32),
        compiler_params=pltpu.CompilerParams(dimension_semantics=("parallel",)),
    )(c_t, w_ada, b_ada)


def _split3(x):
    hi = x.astype(BF16).astype(F32)
    r = x - hi
    mid = r.astype(BF16).astype(F32)
    lo = (r - mid).astype(BF16).astype(F32)
    return hi, mid, lo


def _in_kernel(x_ref, ada_ref, w_ref, b_ref, wpm_ref, bpm_ref, ps_ref,
               q_ref, k_ref, v_ref, g_ref, ga_ref, yp_ref,
               fcarry_ref, pcarry_ref):
    si = pl.program_id(1)
    ts = x_ref.shape[1]

    @pl.when(si == 0)
    def _():
        fcarry_ref[...] = jnp.zeros_like(fcarry_ref)
        pcarry_ref[...] = jnp.zeros_like(pcarry_ref)

    ada = ada_ref[0]
    u = (x_ref[0] * (1.0 + ada[1:2]) + ada[0:1]).astype(BF16)
    proj = _dot(u, w_ref[...]) + b_ref[...]
    seg = lambda name: proj[:, IN_OFFSET[name]:IN_OFFSET[name] + IN_WIDTH[name]]

    q = seg("q") * (HEAD_DIM ** -0.5 * LOG2E)
    for blk in range(D_ATT // LANES):
        q_ref[0, blk] = q[:, blk * LANES:(blk + 1) * LANES].T.astype(BF16)

    k_ref[0] = seg("k").astype(BF16)

    v = seg("v")
    for blk in range(D_ATT // LANES):
        v_ref[0, blk] = v[:, blk * LANES:(blk + 1) * LANES].T.astype(BF16)

    fl = seg("f")
    logf = jnp.minimum(fl, 0.0) - jnp.log1p(jnp.exp(-jnp.abs(fl)))
    row = lax.broadcasted_iota(jnp.int32, (CUM_ROWS, CUM_ROWS), 0)
    col = lax.broadcasted_iota(jnp.int32, (CUM_ROWS, CUM_ROWS), 1)
    tri = jnp.where(row >= col, 1.0, 0.0).astype(BF16)
    terms = jnp.concatenate([t.astype(BF16) for t in _split3(logf)], axis=1)
    offset = fcarry_ref[0:1, :]
    cum_blocks = []
    for r in range(ts // CUM_ROWS):
        part = _dot(tri, terms[r * CUM_ROWS:(r + 1) * CUM_ROWS, :])
        local = part[:, :LANES] + part[:, LANES:2 * LANES] + part[:, 2 * LANES:]
        cum_blocks.append(local + offset)
        offset = cum_blocks[-1][CUM_ROWS - 1:CUM_ROWS, :]
    fcarry_ref[...] = jnp.broadcast_to(offset, fcarry_ref.shape)
    cum = jnp.concatenate(cum_blocks, axis=0)

    g_hi, g_mid, g_lo = _split3(cum * (-LOG2E))
    lane = lax.broadcasted_iota(jnp.int32, (ts, LANES), 1)
    g = jnp.where(lane < N_HEADS, g_hi,
                  jnp.where(lane < 2 * N_HEADS, pltpu.roll(g_mid, N_HEADS, axis=1),
                            jnp.where(lane < 3 * N_HEADS, pltpu.roll(g_lo, 2 * N_HEADS, axis=1),
                                      0.0)))
    g_ref[0] = g.astype(BF16)

    p = seg("p")
    pe = jnp.concatenate([pcarry_ref[...], p], axis=0)
    pcarry_ref[...] = p[ts - POOL_HISTORY:, :]
    gp = seg("gp")
    t_glob = si * ts + lax.broadcasted_iota(jnp.int32, (ts, POOL_GROUP_DIM), 0)
    pooled = []
    for g, w in enumerate(POOL_WINDOWS):
        sl = slice(g * POOL_GROUP_DIM, (g + 1) * POOL_GROUP_DIM)
        y = pe[:, sl]
        sh = 1
        while sh < w:
            y = y + pltpu.roll(y, sh, axis=0)
            sh *= 2
        cnt = jnp.minimum(t_glob + 1, w).astype(F32)
        pooled.append((y[POOL_HISTORY:, :] / cnt - p[:, sl]).astype(BF16))
    for j in range(len(POOL_WINDOWS) // 2):
        sl = slice(2 * j * POOL_GROUP_DIM, (2 * j + 2) * POOL_GROUP_DIM)
        mixed = _dot(jnp.concatenate(pooled[2 * j:2 * j + 2], axis=1), wpm_ref[j]) + bpm_ref[:, sl]
        yp_ref[0, :, sl] = (mixed * ps_ref[:, sl] * _silu(gp[:, sl])).astype(BF16)

    ga_ref[0] = _silu(seg("ga")).astype(BF16)


def _in_call(x, ada3, w_cat, b_cat, w_pm, b_pm, pool_scale):
    B, S, D = x.shape
    ts = SEQ_TILE
    const2 = lambda b, s: (0, 0)
    tile = lambda width: pl.BlockSpec((1, ts, width), lambda b, s: (b, s, 0))
    n_pairs = D_ATT // LANES
    tile_t = pl.BlockSpec((1, n_pairs, LANES, ts), lambda b, s: (b, 0, 0, s))
    return pl.pallas_call(
        _in_kernel,
        name="input_stage",
        grid=(B, S // ts),
        in_specs=[tile(D), pl.BlockSpec((1, 3, D), lambda b, s: (b, 0, 0)),
                  pl.BlockSpec(w_cat.shape, const2), pl.BlockSpec(b_cat.shape, const2),
                  pl.BlockSpec(w_pm.shape, lambda b, s: (0, 0, 0)),
                  pl.BlockSpec(b_pm.shape, const2),
                  pl.BlockSpec(pool_scale.shape, const2)],
        out_specs=[tile_t, tile(D_ATT), tile_t, tile(LANES),
                   tile(D_ATT), tile(D_POOL)],
        out_shape=[jax.ShapeDtypeStruct((B, n_pairs, LANES, S), BF16),
                   jax.ShapeDtypeStruct((B, S, D_ATT), BF16),
                   jax.ShapeDtypeStruct((B, n_pairs, LANES, S), BF16),
                   jax.ShapeDtypeStruct((B, S, LANES), BF16),
                   jax.ShapeDtypeStruct((B, S, D_ATT), BF16),
                   jax.ShapeDtypeStruct((B, S, D_POOL), BF16)],
        scratch_shapes=[pltpu.VMEM((8, LANES), F32),
                        pltpu.VMEM((POOL_HISTORY, D_POOL), F32)],
        compiler_params=pltpu.CompilerParams(
            dimension_semantics=("arbitrary", "arbitrary"),
            vmem_limit_bytes=VMEM_LIMIT_BYTES),
    )(x, ada3, w_cat, b_cat, w_pm, b_pm, pool_scale)


def _attn_kernel(qt_ref, qtn_ref, k_ref, vt_ref, g_ref, ga_ref, yp_ref, x_ref, ada_ref,
                 wo_ref, bo_ref, lg_ref, lb_ref, mask_ref, o_ref,
                 m_ref, acc_ref, s0_ref, s1_ref, mx0_ref, mx1_ref, rhs_ref):
    qi = pl.program_id(1)
    n_tiles = pl.num_programs(1)
    tq = qt_ref.shape[3]
    tk = KV_TILE

    def diag_block(tile):
        start = pl.multiple_of(((tile * tq) // tk) * tk, tk)
        return start, tile * tq - start

    n_main = (qi * tq) // tk
    diag_start, _ = diag_block(qi)
    ones_row = lax.broadcasted_iota(jnp.int32, (PV_ROWS - HEAD_DIM, tk), 0) == 0

    def block_start(k):
        return pl.multiple_of(jnp.where(k == 0, diag_start, (k - 1) * tk), tk)

    def scores(start, s_ref, mx_ref, diag_shift=None):
        gblk = g_ref[0, pl.ds(start, tk), :]
        for h in range(N_HEADS):
            pair = h // HEADS_PER_BLOCK
            kblk = k_ref[0, pl.ds(start, tk), pair * LANES:(pair + 1) * LANES]
            lhs = jnp.concatenate([kblk, gblk], axis=1)
            s = _dot(lhs, rhs_ref[h])
            if diag_shift is not None:
                s = s + mask_ref[diag_shift // tq]
            s_ref[h] = s
            mx_ref[h] = jnp.max(s.reshape(tk // 8, 8, tq), axis=0)

    def first_block(q_src_ref, tile):
        row = lax.broadcasted_iota(jnp.int32, (LANES, tq), 0)
        for h in range(N_HEADS):
            pair, half = divmod(h, HEADS_PER_BLOCK)
            own = (row // HEAD_DIM) == half
            rhs_ref[h, :LANES, :] = jnp.where(own, q_src_ref[0, pair], jnp.zeros((), BF16))
            pick = (row < G_TERMS * N_HEADS) & (row % N_HEADS == h)
            rhs_ref[h, LANES:, :] = jnp.where(pick, 1.0, 0.0).astype(BF16)
        start, shift = diag_block(tile)
        scores(start, s0_ref, mx0_ref, diag_shift=shift)

    def softmax_pv(k, s_ref, mx_ref):
        start = block_start(k)

        def probs(h):
            m_prev = m_ref[h]
            m_new = jnp.maximum(m_prev, jnp.max(mx_ref[h], axis=0, keepdims=True))
            m_ref[h] = m_new
            alpha = jnp.exp2(m_prev - m_new)
            p = jnp.exp2(s_ref[h] - m_new[0:1, :]).astype(BF16)
            return alpha, p

        def accumulate(h, alpha, p):
            pair, half = divmod(h, HEADS_PER_BLOCK)
            vt = vt_ref[0, pair, half * HEAD_DIM:(half + 1) * HEAD_DIM, pl.ds(start, tk)]
            ones = jnp.where(ones_row, 1.0, 0.0).astype(BF16)
            pv = _dot(jnp.concatenate([vt, ones], axis=0), p)
            acc_ref[h] = acc_ref[h] * alpha[0:1, :] + pv

        ap = {}
        for t in range(N_HEADS + 1):
            if t < N_HEADS:
                ap[t] = probs(t)
            if t >= 1:
                accumulate(t - 1, *ap.pop(t - 1))

    @pl.when(qi == 0)
    def _():
        first_block(qt_ref, qi)

    m_ref[...] = jnp.full_like(m_ref, NEG_BIG)
    acc_ref[...] = jnp.zeros_like(acc_ref)
    n_blocks = n_main + 1

    def pair_of_blocks(i, carry):
        k = 2 * i
        scores(block_start(k + 1), s1_ref, mx1_ref)
        softmax_pv(k, s0_ref, mx0_ref)
        scores(block_start(k + 2), s0_ref, mx0_ref)
        softmax_pv(k + 1, s1_ref, mx1_ref)
        return carry

    n_pairs = (n_blocks - 1) // 2
    lax.fori_loop(0, n_pairs, pair_of_blocks, 0)
    k_last = 2 * n_pairs

    @pl.when(n_blocks - k_last == 2)
    def _():
        scores(block_start(k_last + 1), s1_ref, mx1_ref)
        softmax_pv(k_last, s0_ref, mx0_ref)
        softmax_pv(k_last + 1, s1_ref, mx1_ref)

    @pl.when(n_blocks - k_last == 1)
    def _():
        softmax_pv(k_last, s0_ref, mx0_ref)

    gated = []
    for pair in range(N_HEADS // HEADS_PER_BLOCK):
        outs = []
        for half in range(HEADS_PER_BLOCK):
            acc = acc_ref[pair * HEADS_PER_BLOCK + half]
            outs.append(acc[:HEAD_DIM, :] / acc[HEAD_DIM:HEAD_DIM + 1, :])
        att = jnp.concatenate(outs, axis=0).T
        sl = slice(pair * LANES, (pair + 1) * LANES)
        gated.append((att * ga_ref[0, :, sl].astype(F32)).astype(BF16))
    ya = jnp.concatenate(gated, axis=1)

    def output_stage():
        y = _dot(jnp.concatenate([ya, yp_ref[0]], axis=1), wo_ref[...]) + bo_ref[...]
        hres = DEEPNORM_ALPHA * x_ref[0] + ada_ref[0][2:3] * y
        mu = jnp.mean(hres, axis=-1, keepdims=True)
        d = hres - mu
        var = jnp.mean(d * d, axis=-1, keepdims=True)
        o_ref[0] = d * lax.rsqrt(var + LN_EPS) * lg_ref[...] + lb_ref[...]

    @pl.when(qi + 1 < n_tiles)
    def _():
        output_stage()
        first_block(qtn_ref, qi + 1)

    @pl.when(qi + 1 == n_tiles)
    def _():
        output_stage()


def _attn_call(qt, k, vt, g, ga, yp, x, ada3, wo, b_out, ln_g, ln_b):
    B, n_pairs, _, S = qt.shape
    D = x.shape[2]
    assert KV_TILE % Q_TILE == 0 and S % KV_TILE == 0
    n_tiles = S // Q_TILE
    tile = lambda width: pl.BlockSpec((1, Q_TILE, width), lambda b, i: (b, i, 0))
    whole = lambda width: pl.BlockSpec((1, S, width), lambda b, i: (b, 0, 0))
    const2 = lambda a: pl.BlockSpec(a.shape, lambda b, i: (0, 0))
    qt_tile = lambda index: pl.BlockSpec((1, n_pairs, LANES, Q_TILE),
                                         lambda b, i: (b, 0, 0, index(i)))
    key = lax.broadcasted_iota(jnp.int32, (KV_TILE // Q_TILE, KV_TILE, Q_TILE), 1)
    qry = lax.broadcasted_iota(jnp.int32, (KV_TILE // Q_TILE, KV_TILE, Q_TILE), 2)
    shift = Q_TILE * lax.broadcasted_iota(jnp.int32, (KV_TILE // Q_TILE, KV_TILE, Q_TILE), 0)
    diag_mask = jnp.where(key <= qry + shift, 0.0, NEG_BIG).astype(F32)
    score_buf = pltpu.VMEM((N_HEADS, KV_TILE, Q_TILE), F32)
    stat_buf = pltpu.VMEM((N_HEADS, 8, Q_TILE), F32)
    return pl.pallas_call(
        _attn_kernel,
        name="attention",
        grid=(B, n_tiles),
        in_specs=[qt_tile(lambda i: i), qt_tile(lambda i: jnp.minimum(i + 1, n_tiles - 1)),
                  whole(D_ATT),
                  pl.BlockSpec((1, n_pairs, LANES, S), lambda b, i: (b, 0, 0, 0)),
                  whole(LANES), tile(D_ATT), tile(D_POOL), tile(D),
                  pl.BlockSpec((1, 3, D), lambda b, i: (b, 0, 0)),
                  const2(wo), const2(b_out), const2(ln_g), const2(ln_b),
                  pl.BlockSpec(diag_mask.shape, lambda b, i: (0, 0, 0))],
        out_specs=tile(D),
        out_shape=jax.ShapeDtypeStruct((B, S, D), F32),
        scratch_shapes=[stat_buf, pltpu.VMEM((N_HEADS, PV_ROWS, Q_TILE), F32),
                        score_buf, score_buf, stat_buf, stat_buf,
                        pltpu.VMEM((N_HEADS, 2 * LANES, Q_TILE), BF16)],
        compiler_params=pltpu.CompilerParams(
            dimension_semantics=("arbitrary", "arbitrary"),
            vmem_limit_bytes=VMEM_LIMIT_BYTES),
    )(qt, qt, k, vt, g, ga, yp, x, ada3, wo, b_out, ln_g, ln_b, diag_mask)


def _layer(x, c, w_ada, b_ada, w_in, b_in, w_pool_mix, b_pool_mix, pool_scale,
           w_out, b_out, ln_g, ln_b):
    B, S, D = x.shape
    c_t = jnp.pad(c.T, ((0, 0), (0, LANES - B)))
    ada3 = _ada_call(c_t, B, w_ada, b_ada[None, :]).reshape(B, 3, D)

    src, w_parts, b_parts = 0, {}, {}
    for name, width in (("q", D_ATT), ("k", D_ATT), ("v", D_ATT), ("f", N_HEADS),
                        ("p", D_POOL), ("ga", D_ATT), ("gp", D_POOL)):
        pad = ((0, 0), (0, IN_WIDTH[name] - width))
        w_parts[name] = jnp.pad(w_in[:, src:src + width], pad)
        b_parts[name] = jnp.pad(b_in[None, src:src + width], pad)
        src += width
    w_cat = jnp.concatenate([w_parts[n] for n in IN_WIDTH], axis=1).astype(BF16)
    b_cat = jnp.concatenate([b_parts[n] for n in IN_WIDTH], axis=1)

    gd = POOL_GROUP_DIM
    w_pm = jnp.zeros((len(POOL_WINDOWS) // 2, 2 * gd, 2 * gd), BF16)
    for g in range(len(POOL_WINDOWS)):
        o = (g % 2) * gd
        w_pm = w_pm.at[g // 2, o:o + gd, o:o + gd].set(w_pool_mix[g].astype(BF16))

    qt, k, vt, g, ga, yp = _in_call(x, ada3, w_cat, b_cat, w_pm,
                                    b_pool_mix.reshape(1, D_POOL), pool_scale[None, :])
    return _attn_call(qt, k, vt, g, ga, yp, x, ada3, w_out.astype(BF16), b_out[None, :],
                      ln_g[None, :], ln_b[None, :])


def kernel(x, c, w_ada, b_ada, w_in, b_in, w_pool_mix, b_pool_mix, pool_scale, w_out, b_out,
           ln_g, ln_b):
    for layer in range(w_ada.shape[0]):
        x = _layer(x, c, w_ada[layer], b_ada[layer], w_in[layer], b_in[layer],
                   w_pool_mix[layer], b_pool_mix[layer], pool_scale[layer],
                   w_out[layer], b_out[layer], ln_g[layer], ln_b[layer])
    return x
```

```python
import math

import jax
import jax.numpy as jnp
from jax import lax
from jax.experimental import pallas as pl
from jax.experimental.pallas import tpu as pltpu

D_MODEL = 1024
D_ATT = 512
D_POOL = 512
N_HEADS = 8
HEAD_DIM = 64
POOL_WINDOWS = (2, 4, 8, 16)
POOL_GROUP_DIM = 128
POOL_HISTORY = 16
LN_EPS = 1e-5
DEEPNORM_ALPHA = 2.0 ** 0.25
LOG2E = math.log2(math.e)
NEG_BIG = -1e30

LANES = 128
HEADS_PER_BLOCK = LANES // HEAD_DIM
G_TERMS = 3
CUM_ROWS = 128
PV_ROWS = HEAD_DIM + 16

IN_SOURCE = (("q", D_ATT), ("k", D_ATT), ("v", D_ATT), ("f", N_HEADS), ("p", D_POOL),
             ("ga", D_ATT), ("gp", D_POOL))
IN_WIDTH = {"f": LANES, "p": D_POOL, "gp": D_POOL, "q": D_ATT, "v": D_ATT, "k": D_ATT,
            "ga": D_ATT}
IN_OFFSET = dict(zip(IN_WIDTH, (sum(list(IN_WIDTH.values())[:i]) for i in range(len(IN_WIDTH)))))
IN_COLS = sum(IN_WIDTH.values())

SEQ_TILE = 512
Q_TILE = 256
KV_TILE = 512
VMEM_LIMIT_BYTES = 56 * 1024 * 1024

F32 = jnp.float32
BF16 = jnp.bfloat16


def _silu(x):
    return x * jax.nn.sigmoid(x)


def _dot(a, b):
    return jnp.dot(a, b, preferred_element_type=F32)


def _resident(shape, index_map):
    return pl.BlockSpec(shape, index_map, pipeline_mode=pl.Buffered(1))


def _ada_kernel(c_ref, w_ref, b_ref, o_ref, cpad_ref):
    n_rows = c_ref.shape[0]
    cpad_ref[...] = jnp.zeros_like(cpad_ref)
    cpad_ref[0:n_rows, :] = c_ref[...]
    sc = _silu(cpad_ref[...].T)
    for b in range(n_rows):
        o_ref[0, b:b + 1, :] = (jnp.sum(w_ref[...] * sc[:, b:b + 1], axis=0, keepdims=True)
                                + b_ref[...])


def _ada_call(c, w_ada, b_ada):
    n_rows, d = c.shape
    n_chunks = w_ada.shape[1] // d
    return pl.pallas_call(
        _ada_kernel,
        name="ada_vector",
        grid=(n_chunks,),
        in_specs=[
            pl.BlockSpec((n_rows, d), lambda j: (0, 0)),
            pl.BlockSpec((d, d), lambda j: (0, j)),
            pl.BlockSpec((1, d), lambda j: (0, j)),
        ],
        out_specs=pl.BlockSpec((1, n_rows, d), lambda j: (j, 0, 0)),
        out_shape=jax.ShapeDtypeStruct((n_chunks, n_rows, d), F32),
        scratch_shapes=[pltpu.VMEM((LANES, d), F32)],
        compiler_params=pltpu.CompilerParams(dimension_semantics=("parallel",),
                                             vmem_limit_bytes=VMEM_LIMIT_BYTES),
    )(c, w_ada, b_ada)


def _split3(x):
    hi = x.astype(BF16).astype(F32)
    r = x - hi
    mid = r.astype(BF16).astype(F32)
    lo = (r - mid).astype(BF16).astype(F32)
    return hi, mid, lo


def _in_kernel(x_ref, ada_ref, wt_ref, b_ref, wpm_ref, bpm_ref, ps_ref,
               q_ref, k_ref, v_ref, g_ref, ga_ref, yp_ref,
               fcarry_ref, pcarry_ref, wcat_ref, bcat_ref, wmix_ref):
    bi = pl.program_id(0)
    si = pl.program_id(1)
    ts = x_ref.shape[1]

    @pl.when((bi == 0) & (si == 0))
    def _():
        src = 0
        for name, width in IN_SOURCE:
            dst, wide = IN_OFFSET[name], IN_WIDTH[name]
            w_seg = wt_ref[src:src + wide, :].T
            b_seg = b_ref[:, src:src + wide]
            if width < wide:
                w_seg = jnp.where(lax.broadcasted_iota(jnp.int32, w_seg.shape, 1) < width,
                                  w_seg, 0.0)
                b_seg = jnp.where(lax.broadcasted_iota(jnp.int32, b_seg.shape, 1) < width,
                                  b_seg, 0.0)
            wcat_ref[:, dst:dst + wide] = w_seg.astype(BF16)
            bcat_ref[:, dst:dst + wide] = b_seg
            src += width
        gd = POOL_GROUP_DIM
        wmix_ref[...] = jnp.zeros_like(wmix_ref)
        for g in range(len(POOL_WINDOWS)):
            o = (g % 2) * gd
            wmix_ref[g // 2, o:o + gd, o:o + gd] = wpm_ref[g].astype(BF16)

    @pl.when(si == 0)
    def _():
        fcarry_ref[...] = jnp.zeros_like(fcarry_ref)
        pcarry_ref[...] = jnp.zeros_like(pcarry_ref)

    shift = ada_ref[0, pl.ds(bi, 1), :]
    scale = ada_ref[1, pl.ds(bi, 1), :]
    u = (x_ref[0] * (1.0 + scale) + shift).astype(BF16)
    proj = _dot(u, wcat_ref[...]) + bcat_ref[...]
    seg = lambda name: proj[:, IN_OFFSET[name]:IN_OFFSET[name] + IN_WIDTH[name]]

    q = seg("q") * (HEAD_DIM ** -0.5 * LOG2E)
    for blk in range(D_ATT // LANES):
        q_ref[0, blk] = q[:, blk * LANES:(blk + 1) * LANES].T.astype(BF16)

    k_ref[0] = seg("k").astype(BF16)

    v = seg("v")
    for blk in range(D_ATT // LANES):
        v_ref[0, blk] = v[:, blk * LANES:(blk + 1) * LANES].T.astype(BF16)

    fl = seg("f")
    logf = jnp.minimum(fl, 0.0) - jnp.log1p(jnp.exp(-jnp.abs(fl)))
    row = lax.broadcasted_iota(jnp.int32, (CUM_ROWS, CUM_ROWS), 0)
    col = lax.broadcasted_iota(jnp.int32, (CUM_ROWS, CUM_ROWS), 1)
    tri = jnp.where(row >= col, 1.0, 0.0).astype(BF16)
    terms = jnp.concatenate([t.astype(BF16) for t in _split3(logf)], axis=1)
    offset = fcarry_ref[0:1, :]
    cum_blocks = []
    for r in range(ts // CUM_ROWS):
        part = _dot(tri, terms[r * CUM_ROWS:(r + 1) * CUM_ROWS, :])
        local = part[:, :LANES] + part[:, LANES:2 * LANES] + part[:, 2 * LANES:]
        cum_blocks.append(local + offset)
        offset = cum_blocks[-1][CUM_ROWS - 1:CUM_ROWS, :]
    fcarry_ref[...] = jnp.broadcast_to(offset, fcarry_ref.shape)
    cum = jnp.concatenate(cum_blocks, axis=0)

    g_hi, g_mid, g_lo = _split3(cum * (-LOG2E))
    lane = lax.broadcasted_iota(jnp.int32, (ts, LANES), 1)
    g = jnp.where(lane < N_HEADS, g_hi,
                  jnp.where(lane < 2 * N_HEADS, pltpu.roll(g_mid, N_HEADS, axis=1),
                            jnp.where(lane < 3 * N_HEADS, pltpu.roll(g_lo, 2 * N_HEADS, axis=1),
                                      0.0)))
    g_ref[0] = g.astype(BF16)

    p = seg("p")
    pe = jnp.concatenate([pcarry_ref[...], p], axis=0)
    pcarry_ref[...] = p[ts - POOL_HISTORY:, :]
    gp = seg("gp")
    t_glob = si * ts + lax.broadcasted_iota(jnp.int32, (ts, POOL_GROUP_DIM), 0)
    pooled = []
    for g, w in enumerate(POOL_WINDOWS):
        sl = slice(g * POOL_GROUP_DIM, (g + 1) * POOL_GROUP_DIM)
        y = pe[:, sl]
        sh = 1
        while sh < w:
            y = y + pltpu.roll(y, sh, axis=0)
            sh *= 2
        cnt = jnp.minimum(t_glob + 1, w).astype(F32)
        pooled.append((y[POOL_HISTORY:, :] / cnt - p[:, sl]).astype(BF16))
    for j in range(len(POOL_WINDOWS) // 2):
        sl = slice(2 * j * POOL_GROUP_DIM, (2 * j + 2) * POOL_GROUP_DIM)
        bias = jnp.concatenate([bpm_ref[2 * j:2 * j + 1, :], bpm_ref[2 * j + 1:2 * j + 2, :]],
                               axis=1)
        mixed = _dot(jnp.concatenate(pooled[2 * j:2 * j + 2], axis=1), wmix_ref[j]) + bias
        yp_ref[0, :, sl] = (mixed * ps_ref[:, sl] * _silu(gp[:, sl])).astype(BF16)

    ga_ref[0] = _silu(seg("ga")).astype(BF16)


def _in_call(x, ada, w_in_t, b_in, w_pm, b_pm, pool_scale):
    B, S, D = x.shape
    ts = SEQ_TILE
    const2 = lambda b, s: (0, 0)
    const3 = lambda b, s: (0, 0, 0)
    tile = lambda width: pl.BlockSpec((1, ts, width), lambda b, s: (b, s, 0))
    n_pairs = D_ATT // LANES
    tile_t = pl.BlockSpec((1, n_pairs, LANES, ts), lambda b, s: (b, 0, 0, s))
    return pl.pallas_call(
        _in_kernel,
        name="input_stage",
        grid=(B, S // ts),
        in_specs=[tile(D), pl.BlockSpec(ada.shape, const3),
                  _resident(w_in_t.shape, const2), pl.BlockSpec(b_in.shape, const2),
                  pl.BlockSpec(w_pm.shape, const3), pl.BlockSpec(b_pm.shape, const2),
                  pl.BlockSpec(pool_scale.shape, const2)],
        out_specs=[tile_t, tile(D_ATT), tile_t, tile(LANES),
                   tile(D_ATT), tile(D_POOL)],
        out_shape=[jax.ShapeDtypeStruct((B, n_pairs, LANES, S), BF16),
                   jax.ShapeDtypeStruct((B, S, D_ATT), BF16),
                   jax.ShapeDtypeStruct((B, n_pairs, LANES, S), BF16),
                   jax.ShapeDtypeStruct((B, S, LANES), BF16),
                   jax.ShapeDtypeStruct((B, S, D_ATT), BF16),
                   jax.ShapeDtypeStruct((B, S, D_POOL), BF16)],
        scratch_shapes=[pltpu.VMEM((8, LANES), F32),
                        pltpu.VMEM((POOL_HISTORY, D_POOL), F32),
                        pltpu.VMEM((D, IN_COLS), BF16),
                        pltpu.VMEM((1, IN_COLS), F32),
                        pltpu.VMEM((len(POOL_WINDOWS) // 2, 2 * POOL_GROUP_DIM,
                                    2 * POOL_GROUP_DIM), BF16)],
        compiler_params=pltpu.CompilerParams(
            dimension_semantics=("arbitrary", "arbitrary"),
            vmem_limit_bytes=VMEM_LIMIT_BYTES),
    )(x, ada, w_in_t, b_in, w_pm, b_pm, pool_scale)


def _attn_kernel(qt_ref, k_ref, vt_ref, g_ref, ga_ref, yp_ref, x_ref, ada_ref,
                 wo_ref, bo_ref, lg_ref, lb_ref, o_ref,
                 m_ref, acc_ref, s0_ref, s1_ref, mx0_ref, mx1_ref, rhs_ref, wob_ref):
    bi = pl.program_id(0)
    qi = pl.program_id(1)
    tq = qt_ref.shape[3]
    tk = KV_TILE
    n_main = (qi * tq) // tk
    diag_start = pl.multiple_of(n_main * tk, tk)
    diag_shift = qi * tq - diag_start

    @pl.when((bi == 0) & (qi == 0))
    def _():
        wob_ref[...] = wo_ref[...].astype(BF16)

    m_ref[...] = jnp.full_like(m_ref, NEG_BIG)
    acc_ref[...] = jnp.zeros_like(acc_ref)

    row = lax.broadcasted_iota(jnp.int32, (LANES, tq), 0)
    for h in range(N_HEADS):
        pair, half = divmod(h, HEADS_PER_BLOCK)
        own = (row // HEAD_DIM) == half
        rhs_ref[h, :LANES, :] = jnp.where(own, qt_ref[0, pair], jnp.zeros((), BF16))
        pick = (row < G_TERMS * N_HEADS) & (row % N_HEADS == h)
        rhs_ref[h, LANES:, :] = jnp.where(pick, 1.0, 0.0).astype(BF16)
    ones_row = lax.broadcasted_iota(jnp.int32, (PV_ROWS - HEAD_DIM, tk), 0) == 0

    def block_start(k):
        return pl.multiple_of(jnp.where(k == 0, diag_start, (k - 1) * tk), tk)

    def scores(k, s_ref, mx_ref, masked=False):
        start = block_start(k)
        gblk = g_ref[0, pl.ds(start, tk), :]
        for h in range(N_HEADS):
            pair = h // HEADS_PER_BLOCK
            kblk = k_ref[0, pl.ds(start, tk), pair * LANES:(pair + 1) * LANES]
            lhs = jnp.concatenate([kblk, gblk], axis=1)
            s = _dot(lhs, rhs_ref[h])
            if masked:
                key = lax.broadcasted_iota(jnp.int32, (tk, tq), 0)
                qry = lax.broadcasted_iota(jnp.int32, (tk, tq), 1)
                s = jnp.where(key <= qry + diag_shift, s, NEG_BIG)
            s_ref[h] = s
            mx_ref[h] = jnp.max(s.reshape(tk // 8, 8, tq), axis=0)

    def softmax_pv(k, s_ref, mx_ref):
        start = block_start(k)

        def probs(h):
            m_prev = m_ref[h]
            m_new = jnp.maximum(m_prev, jnp.max(mx_ref[h], axis=0, keepdims=True))
            m_ref[h] = m_new
            alpha = jnp.exp2(m_prev - m_new)
            p = jnp.exp2(s_ref[h] - m_new[0:1, :]).astype(BF16)
            return alpha, p

        def accumulate(h, alpha, p):
            pair, half = divmod(h, HEADS_PER_BLOCK)
            vt = vt_ref[0, pair, half * HEAD_DIM:(half + 1) * HEAD_DIM, pl.ds(start, tk)]
            ones = jnp.where(ones_row, 1.0, 0.0).astype(BF16)
            pv = _dot(jnp.concatenate([vt, ones], axis=0), p)
            acc_ref[h] = acc_ref[h] * alpha[0:1, :] + pv

        ap = {}
        for t in range(N_HEADS + 1):
            if t < N_HEADS:
                ap[t] = probs(t)
            if t >= 1:
                accumulate(t - 1, *ap.pop(t - 1))

    n_blocks = n_main + 1
    scores(0, s0_ref, mx0_ref, masked=True)

    def pair_of_blocks(i, carry):
        k = 2 * i
        scores(k + 1, s1_ref, mx1_ref)
        softmax_pv(k, s0_ref, mx0_ref)
        scores(k + 2, s0_ref, mx0_ref)
        softmax_pv(k + 1, s1_ref, mx1_ref)
        return carry

    n_pairs = (n_blocks - 1) // 2
    lax.fori_loop(0, n_pairs, pair_of_blocks, 0)
    k_last = 2 * n_pairs

    @pl.when(n_blocks - k_last == 2)
    def _():
        scores(k_last + 1, s1_ref, mx1_ref)
        softmax_pv(k_last, s0_ref, mx0_ref)
        softmax_pv(k_last + 1, s1_ref, mx1_ref)

    @pl.when(n_blocks - k_last == 1)
    def _():
        softmax_pv(k_last, s0_ref, mx0_ref)

    gated = []
    for pair in range(N_HEADS // HEADS_PER_BLOCK):
        outs = []
        for half in range(HEADS_PER_BLOCK):
            acc = acc_ref[pair * HEADS_PER_BLOCK + half]
            outs.append(acc[:HEAD_DIM, :] / acc[HEAD_DIM:HEAD_DIM + 1, :])
        att = jnp.concatenate(outs, axis=0).T
        sl = slice(pair * LANES, (pair + 1) * LANES)
        gated.append((att * ga_ref[0, :, sl].astype(F32)).astype(BF16))
    ya = jnp.concatenate(gated, axis=1)
    y = _dot(jnp.concatenate([ya, yp_ref[0]], axis=1), wob_ref[...]) + bo_ref[...]
    gate = ada_ref[2, pl.ds(bi, 1), :]
    hres = DEEPNORM_ALPHA * x_ref[0] + gate * y
    mu = jnp.mean(hres, axis=-1, keepdims=True)
    d = hres - mu
    var = jnp.mean(d * d, axis=-1, keepdims=True)
    o_ref[0] = d * lax.rsqrt(var + LN_EPS) * lg_ref[...] + lb_ref[...]


def _attn_call(qt, k, vt, g, ga, yp, x, ada, w_out, b_out, ln_g, ln_b):
    B, n_pairs, _, S = qt.shape
    D = x.shape[2]
    assert KV_TILE % Q_TILE == 0 and S % KV_TILE == 0
    tile = lambda width: pl.BlockSpec((1, Q_TILE, width), lambda b, i: (b, i, 0))
    whole = lambda width: pl.BlockSpec((1, S, width), lambda b, i: (b, 0, 0))
    const2 = lambda a: pl.BlockSpec(a.shape, lambda b, i: (0, 0))
    score_buf = pltpu.VMEM((N_HEADS, KV_TILE, Q_TILE), F32)
    stat_buf = pltpu.VMEM((N_HEADS, 8, Q_TILE), F32)
    return pl.pallas_call(
        _attn_kernel,
        name="attention",
        grid=(B, S // Q_TILE),
        in_specs=[pl.BlockSpec((1, n_pairs, LANES, Q_TILE), lambda b, i: (b, 0, 0, i)),
                  whole(D_ATT),
                  pl.BlockSpec((1, n_pairs, LANES, S), lambda b, i: (b, 0, 0, 0)),
                  whole(LANES), tile(D_ATT), tile(D_POOL), tile(D),
                  pl.BlockSpec(ada.shape, lambda b, i: (0, 0, 0)),
                  _resident(w_out.shape, lambda b, i: (0, 0)),
                  const2(b_out), const2(ln_g), const2(ln_b)],
        out_specs=tile(D),
        out_shape=jax.ShapeDtypeStruct((B, S, D), F32),
        scratch_shapes=[stat_buf, pltpu.VMEM((N_HEADS, PV_ROWS, Q_TILE), F32),
                        score_buf, score_buf, stat_buf, stat_buf,
                        pltpu.VMEM((N_HEADS, 2 * LANES, Q_TILE), BF16),
                        pltpu.VMEM(w_out.shape, BF16)],
        compiler_params=pltpu.CompilerParams(
            dimension_semantics=("arbitrary", "arbitrary"),
            vmem_limit_bytes=VMEM_LIMIT_BYTES),
    )(qt, k, vt, g, ga, yp, x, ada, w_out, b_out, ln_g, ln_b)


def _layer(x, c, w_ada, b_ada, w_in, b_in, w_pool_mix, b_pool_mix, pool_scale,
           w_out, b_out, ln_g, ln_b):
    ada = _ada_call(c, w_ada, b_ada[None, :])
    qt, k, vt, g, ga, yp = _in_call(x, ada, w_in.T, b_in[None, :], w_pool_mix, b_pool_mix,
                                    pool_scale[None, :])
    return _attn_call(qt, k, vt, g, ga, yp, x, ada, w_out, b_out[None, :], ln_g[None, :],
                      ln_b[None, :])


def kernel(x, c, w_ada, b_ada, w_in, b_in, w_pool_mix, b_pool_mix, pool_scale, w_out, b_out,
           ln_g, ln_b):
    for layer in range(w_ada.shape[0]):
        x = _layer(x, c, w_ada[layer], b_ada[layer], w_in[layer], b_in[layer],
                   w_pool_mix[layer], b_pool_mix[layer], pool_scale[layer],
                   w_out[layer], b_out[layer], ln_g[layer], ln_b[layer])
    return x
```

```python
import math

import jax
import jax.numpy as jnp
import numpy as np
from jax import lax
from jax.experimental import pallas as pl
from jax.experimental.pallas import tpu as pltpu

D_MODEL = 1024
D_ATT = 512
D_POOL = 512
N_HEADS = 8
HEAD_DIM = 64
POOL_WINDOWS = (2, 4, 8, 16)
POOL_GROUP_DIM = 128
POOL_HISTORY = 16
LN_EPS = 1e-5
DEEPNORM_ALPHA = 2.0 ** 0.25
LOG2E = math.log2(math.e)
NEG_BIG = -1e30

LANES = 128
HEADS_PER_BLOCK = LANES // HEAD_DIM
G_TERMS = 3
CUM_ROWS = 128
PV_ROWS = HEAD_DIM + 16

IN_SOURCE = (("q", D_ATT), ("k", D_ATT), ("v", D_ATT), ("f", N_HEADS), ("p", D_POOL),
             ("ga", D_ATT), ("gp", D_POOL))
IN_WIDTH = {"f": LANES, "p": D_POOL, "gp": D_POOL, "q": D_ATT, "v": D_ATT, "k": D_ATT,
            "ga": D_ATT}
IN_OFFSET = dict(zip(IN_WIDTH, (sum(list(IN_WIDTH.values())[:i]) for i in range(len(IN_WIDTH)))))
IN_COLS = sum(IN_WIDTH.values())

SEQ_TILE = 512
Q_TILE = 256
KV_TILE = 512
VMEM_LIMIT_BYTES = 56 * 1024 * 1024

F32 = jnp.float32
BF16 = jnp.bfloat16


def _silu(x):
    return x * jax.nn.sigmoid(x)


def _dot(a, b):
    return jnp.dot(a, b, preferred_element_type=F32)


def _resident(shape, index_map):
    return pl.BlockSpec(shape, index_map, pipeline_mode=pl.Buffered(1))


def _ada_kernel(c_ref, w_ref, b_ref, o_ref, cpad_ref):
    n_rows = c_ref.shape[0]
    cpad_ref[...] = jnp.zeros_like(cpad_ref)
    cpad_ref[0:n_rows, :] = c_ref[...]
    sc = _silu(cpad_ref[...].T)
    for b in range(n_rows):
        o_ref[0, b:b + 1, :] = (jnp.sum(w_ref[...] * sc[:, b:b + 1], axis=0, keepdims=True)
                                + b_ref[...])


def _ada_call(c, w_ada, b_ada):
    n_rows, d = c.shape
    n_chunks = w_ada.shape[1] // d
    return pl.pallas_call(
        _ada_kernel,
        name="ada_vector",
        grid=(n_chunks,),
        in_specs=[
            pl.BlockSpec((n_rows, d), lambda j: (0, 0)),
            pl.BlockSpec((d, d), lambda j: (0, j)),
            pl.BlockSpec((1, d), lambda j: (0, j)),
        ],
        out_specs=pl.BlockSpec((1, n_rows, d), lambda j: (j, 0, 0)),
        out_shape=jax.ShapeDtypeStruct((n_chunks, n_rows, d), F32),
        scratch_shapes=[pltpu.VMEM((LANES, d), F32)],
        compiler_params=pltpu.CompilerParams(dimension_semantics=("parallel",),
                                             vmem_limit_bytes=VMEM_LIMIT_BYTES),
    )(c, w_ada, b_ada)


def _split3(x):
    hi = x.astype(BF16).astype(F32)
    r = x - hi
    mid = r.astype(BF16).astype(F32)
    lo = (r - mid).astype(BF16).astype(F32)
    return hi, mid, lo


def _in_kernel(x_ref, ada_ref, wt_ref, b_ref, wpm_ref, bpm_ref, ps_ref,
               q_ref, k_ref, v_ref, g_ref, ga_ref, yp_ref,
               fcarry_ref, pcarry_ref, wcat_ref, bcat_ref, wmix_ref):
    bi = pl.program_id(0)
    si = pl.program_id(1)
    ts = x_ref.shape[1]

    @pl.when((bi == 0) & (si == 0))
    def _():
        src = 0
        for name, width in IN_SOURCE:
            dst, wide = IN_OFFSET[name], IN_WIDTH[name]
            w_seg = wt_ref[src:src + wide, :].T
            b_seg = b_ref[:, src:src + wide]
            if width < wide:
                w_seg = jnp.where(lax.broadcasted_iota(jnp.int32, w_seg.shape, 1) < width,
                                  w_seg, 0.0)
                b_seg = jnp.where(lax.broadcasted_iota(jnp.int32, b_seg.shape, 1) < width,
                                  b_seg, 0.0)
            wcat_ref[:, dst:dst + wide] = w_seg.astype(BF16)
            bcat_ref[:, dst:dst + wide] = b_seg
            src += width
        gd = POOL_GROUP_DIM
        wmix_ref[...] = jnp.zeros_like(wmix_ref)
        for g in range(len(POOL_WINDOWS)):
            o = (g % 2) * gd
            wmix_ref[g // 2, o:o + gd, o:o + gd] = wpm_ref[g].astype(BF16)

    @pl.when(si == 0)
    def _():
        fcarry_ref[...] = jnp.zeros_like(fcarry_ref)
        pcarry_ref[...] = jnp.zeros_like(pcarry_ref)

    shift = ada_ref[0, pl.ds(bi, 1), :]
    scale = ada_ref[1, pl.ds(bi, 1), :]
    u = (x_ref[0] * (1.0 + scale) + shift).astype(BF16)
    proj = _dot(u, wcat_ref[...]) + bcat_ref[...]
    seg = lambda name: proj[:, IN_OFFSET[name]:IN_OFFSET[name] + IN_WIDTH[name]]

    q = seg("q") * (HEAD_DIM ** -0.5 * LOG2E)
    for blk in range(D_ATT // LANES):
        q_ref[0, blk] = q[:, blk * LANES:(blk + 1) * LANES].T.astype(BF16)

    k_ref[0] = seg("k").astype(BF16)

    v = seg("v")
    for blk in range(D_ATT // LANES):
        v_ref[0, blk] = v[:, blk * LANES:(blk + 1) * LANES].T.astype(BF16)

    fl = seg("f")
    logf = jnp.minimum(fl, 0.0) - jnp.log1p(jnp.exp(-jnp.abs(fl)))
    row = lax.broadcasted_iota(jnp.int32, (CUM_ROWS, CUM_ROWS), 0)
    col = lax.broadcasted_iota(jnp.int32, (CUM_ROWS, CUM_ROWS), 1)
    tri = jnp.where(row >= col, 1.0, 0.0).astype(BF16)
    terms = jnp.concatenate([t.astype(BF16) for t in _split3(logf)], axis=1)
    offset = fcarry_ref[0:1, :]
    cum_blocks = []
    for r in range(ts // CUM_ROWS):
        part = _dot(tri, terms[r * CUM_ROWS:(r + 1) * CUM_ROWS, :])
        local = part[:, :LANES] + part[:, LANES:2 * LANES] + part[:, 2 * LANES:]
        cum_blocks.append(local + offset)
        offset = cum_blocks[-1][CUM_ROWS - 1:CUM_ROWS, :]
    fcarry_ref[...] = jnp.broadcast_to(offset, fcarry_ref.shape)
    cum = jnp.concatenate(cum_blocks, axis=0)

    g_hi, g_mid, g_lo = _split3(cum * (-LOG2E))
    lane = lax.broadcasted_iota(jnp.int32, (ts, LANES), 1)
    g = jnp.where(lane < N_HEADS, g_hi,
                  jnp.where(lane < 2 * N_HEADS, pltpu.roll(g_mid, N_HEADS, axis=1),
                            jnp.where(lane < 3 * N_HEADS, pltpu.roll(g_lo, 2 * N_HEADS, axis=1),
                                      0.0)))
    g_ref[0] = g.astype(BF16)

    p = seg("p")
    pe = jnp.concatenate([pcarry_ref[...], p], axis=0)
    pcarry_ref[...] = p[ts - POOL_HISTORY:, :]
    gp = seg("gp")
    t_glob = si * ts + lax.broadcasted_iota(jnp.int32, (ts, POOL_GROUP_DIM), 0)
    pooled = []
    for g, w in enumerate(POOL_WINDOWS):
        sl = slice(g * POOL_GROUP_DIM, (g + 1) * POOL_GROUP_DIM)
        y = pe[:, sl]
        sh = 1
        while sh < w:
            y = y + pltpu.roll(y, sh, axis=0)
            sh *= 2
        cnt = jnp.minimum(t_glob + 1, w).astype(F32)
        pooled.append((y[POOL_HISTORY:, :] / cnt - p[:, sl]).astype(BF16))
    for j in range(len(POOL_WINDOWS) // 2):
        sl = slice(2 * j * POOL_GROUP_DIM, (2 * j + 2) * POOL_GROUP_DIM)
        bias = jnp.concatenate([bpm_ref[2 * j:2 * j + 1, :], bpm_ref[2 * j + 1:2 * j + 2, :]],
                               axis=1)
        mixed = _dot(jnp.concatenate(pooled[2 * j:2 * j + 2], axis=1), wmix_ref[j]) + bias
        yp_ref[0, :, sl] = (mixed * ps_ref[:, sl] * _silu(gp[:, sl])).astype(BF16)

    ga_ref[0] = _silu(seg("ga")).astype(BF16)


def _in_call(x, ada, w_in_t, b_in, w_pm, b_pm, pool_scale):
    B, S, D = x.shape
    ts = SEQ_TILE
    const2 = lambda b, s: (0, 0)
    const3 = lambda b, s: (0, 0, 0)
    tile = lambda width: pl.BlockSpec((1, ts, width), lambda b, s: (b, s, 0))
    n_pairs = D_ATT // LANES
    tile_t = pl.BlockSpec((1, n_pairs, LANES, ts), lambda b, s: (b, 0, 0, s))
    return pl.pallas_call(
        _in_kernel,
        name="input_stage",
        grid=(B, S // ts),
        in_specs=[tile(D), pl.BlockSpec(ada.shape, const3),
                  _resident(w_in_t.shape, const2), pl.BlockSpec(b_in.shape, const2),
                  pl.BlockSpec(w_pm.shape, const3), pl.BlockSpec(b_pm.shape, const2),
                  pl.BlockSpec(pool_scale.shape, const2)],
        out_specs=[tile_t, tile(D_ATT), tile_t, tile(LANES),
                   tile(D_ATT), tile(D_POOL)],
        out_shape=[jax.ShapeDtypeStruct((B, n_pairs, LANES, S), BF16),
                   jax.ShapeDtypeStruct((B, S, D_ATT), BF16),
                   jax.ShapeDtypeStruct((B, n_pairs, LANES, S), BF16),
                   jax.ShapeDtypeStruct((B, S, LANES), BF16),
                   jax.ShapeDtypeStruct((B, S, D_ATT), BF16),
                   jax.ShapeDtypeStruct((B, S, D_POOL), BF16)],
        scratch_shapes=[pltpu.VMEM((8, LANES), F32),
                        pltpu.VMEM((POOL_HISTORY, D_POOL), F32),
                        pltpu.VMEM((D, IN_COLS), BF16),
                        pltpu.VMEM((1, IN_COLS), F32),
                        pltpu.VMEM((len(POOL_WINDOWS) // 2, 2 * POOL_GROUP_DIM,
                                    2 * POOL_GROUP_DIM), BF16)],
        compiler_params=pltpu.CompilerParams(
            dimension_semantics=("arbitrary", "arbitrary"),
            vmem_limit_bytes=VMEM_LIMIT_BYTES),
    )(x, ada, w_in_t, b_in, w_pm, b_pm, pool_scale)


def _attn_kernel(qt_ref, k_ref, vt_ref, g_ref, ga_ref, yp_ref, x_ref, ada_ref,
                 wo_ref, bo_ref, lg_ref, lb_ref, mask_ref, o_ref,
                 m_ref, acc_ref, s0_ref, s1_ref, mx0_ref, mx1_ref, rhs_ref, wob_ref, h_ref):
    bi = pl.program_id(0)
    step = pl.program_id(1)
    n_tiles = pl.num_programs(1) - 1
    qi = jnp.minimum(step, n_tiles - 1)
    tq = qt_ref.shape[3]
    tk = KV_TILE
    n_main = (qi * tq) // tk
    diag_start = pl.multiple_of(n_main * tk, tk)
    diag_shift = qi * tq - diag_start
    ones_row = lax.broadcasted_iota(jnp.int32, (PV_ROWS - HEAD_DIM, tk), 0) == 0

    @pl.when((bi == 0) & (step == 0))
    def _():
        wob_ref[...] = wo_ref[...].astype(BF16)

    def layer_norm_previous():
        hres = h_ref[...]
        mu = jnp.mean(hres, axis=-1, keepdims=True)
        d = hres - mu
        var = jnp.mean(d * d, axis=-1, keepdims=True)
        o_ref[0] = d * lax.rsqrt(var + LN_EPS) * lg_ref[...] + lb_ref[...]

    def score_weights():
        row = lax.broadcasted_iota(jnp.int32, (LANES, tq), 0)
        for h in range(N_HEADS):
            pair, half = divmod(h, HEADS_PER_BLOCK)
            own = (row // HEAD_DIM) == half
            rhs_ref[h, :LANES, :] = jnp.where(own, qt_ref[0, pair], jnp.zeros((), BF16))
            pick = (row < G_TERMS * N_HEADS) & (row % N_HEADS == h)
            rhs_ref[h, LANES:, :] = jnp.where(pick, 1.0, 0.0).astype(BF16)

    def block_start(k):
        return pl.multiple_of(jnp.where(k == 0, diag_start, (k - 1) * tk), tk)

    def scores(k, s_ref, mx_ref, masked=False):
        start = block_start(k)
        gblk = g_ref[0, pl.ds(start, tk), :]
        for h in range(N_HEADS):
            pair = h // HEADS_PER_BLOCK
            kblk = k_ref[0, pl.ds(start, tk), pair * LANES:(pair + 1) * LANES]
            lhs = jnp.concatenate([kblk, gblk], axis=1)
            s = _dot(lhs, rhs_ref[h])
            if masked:
                s = s + mask_ref[diag_shift // tq]
            s_ref[h] = s
            mx_ref[h] = jnp.max(s.reshape(tk // 8, 8, tq), axis=0)

    def softmax_pv(k, s_ref, mx_ref):
        start = block_start(k)

        def probs(h):
            m_prev = m_ref[h]
            m_new = jnp.maximum(m_prev, jnp.max(mx_ref[h], axis=0, keepdims=True))
            m_ref[h] = m_new
            alpha = jnp.exp2(m_prev - m_new)
            p = jnp.exp2(s_ref[h] - m_new[0:1, :]).astype(BF16)
            return alpha, p

        def accumulate(h, alpha, p):
            pair, half = divmod(h, HEADS_PER_BLOCK)
            vt = vt_ref[0, pair, half * HEAD_DIM:(half + 1) * HEAD_DIM, pl.ds(start, tk)]
            ones = jnp.where(ones_row, 1.0, 0.0).astype(BF16)
            pv = _dot(jnp.concatenate([vt, ones], axis=0), p)
            acc_ref[h] = acc_ref[h] * alpha[0:1, :] + pv

        ap = {}
        for t in range(N_HEADS + 1):
            if t < N_HEADS:
                ap[t] = probs(t)
            if t >= 1:
                accumulate(t - 1, *ap.pop(t - 1))

    @pl.when(step == 0)
    def _():
        score_weights()
        scores(0, s0_ref, mx0_ref, masked=True)

    @pl.when((step > 0) & (step < n_tiles))
    def _():
        score_weights()
        scores(0, s0_ref, mx0_ref, masked=True)
        layer_norm_previous()

    @pl.when(step == n_tiles)
    def _():
        layer_norm_previous()

    @pl.when(step < n_tiles)
    def _():
        m_ref[...] = jnp.full_like(m_ref, NEG_BIG)
        acc_ref[...] = jnp.zeros_like(acc_ref)
        n_blocks = n_main + 1

        def pair_of_blocks(i, carry):
            k = 2 * i
            scores(k + 1, s1_ref, mx1_ref)
            softmax_pv(k, s0_ref, mx0_ref)
            scores(k + 2, s0_ref, mx0_ref)
            softmax_pv(k + 1, s1_ref, mx1_ref)
            return carry

        n_pairs = (n_blocks - 1) // 2
        lax.fori_loop(0, n_pairs, pair_of_blocks, 0)
        k_last = 2 * n_pairs

        @pl.when(n_blocks - k_last == 2)
        def _():
            scores(k_last + 1, s1_ref, mx1_ref)
            softmax_pv(k_last, s0_ref, mx0_ref)
            softmax_pv(k_last + 1, s1_ref, mx1_ref)

        @pl.when(n_blocks - k_last == 1)
        def _():
            softmax_pv(k_last, s0_ref, mx0_ref)

        gated = []
        for pair in range(N_HEADS // HEADS_PER_BLOCK):
            outs = []
            for half in range(HEADS_PER_BLOCK):
                acc = acc_ref[pair * HEADS_PER_BLOCK + half]
                outs.append(acc[:HEAD_DIM, :] / acc[HEAD_DIM:HEAD_DIM + 1, :])
            att = jnp.concatenate(outs, axis=0).T
            sl = slice(pair * LANES, (pair + 1) * LANES)
            gated.append((att * ga_ref[0, :, sl].astype(F32)).astype(BF16))
        ya = jnp.concatenate(gated, axis=1)
        y = _dot(jnp.concatenate([ya, yp_ref[0]], axis=1), wob_ref[...]) + bo_ref[...]
        gate = ada_ref[2, pl.ds(bi, 1), :]
        h_ref[...] = DEEPNORM_ALPHA * x_ref[0] + gate * y


def _attn_call(qt, k, vt, g, ga, yp, x, ada, w_out, b_out, ln_g, ln_b):
    B, n_pairs, _, S = qt.shape
    D = x.shape[2]
    assert KV_TILE % Q_TILE == 0 and S % KV_TILE == 0
    n_tiles = S // Q_TILE
    cur = lambda i: jnp.minimum(i, n_tiles - 1)
    prev = lambda i: jnp.maximum(i - 1, 0)
    tile = lambda width: pl.BlockSpec((1, Q_TILE, width), lambda b, i: (b, cur(i), 0))
    whole = lambda width: pl.BlockSpec((1, S, width), lambda b, i: (b, 0, 0))
    const2 = lambda a: pl.BlockSpec(a.shape, lambda b, i: (0, 0))
    score_buf = pltpu.VMEM((N_HEADS, KV_TILE, Q_TILE), F32)
    stat_buf = pltpu.VMEM((N_HEADS, 8, Q_TILE), F32)
    key = np.arange(KV_TILE)[None, :, None]
    qry = np.arange(Q_TILE)[None, None, :] + Q_TILE * np.arange(KV_TILE // Q_TILE)[:, None, None]
    diag_mask = jnp.asarray(np.where(key <= qry, 0.0, NEG_BIG), F32)
    return pl.pallas_call(
        _attn_kernel,
        name="attention",
        grid=(B, n_tiles + 1),
        in_specs=[pl.BlockSpec((1, n_pairs, LANES, Q_TILE), lambda b, i: (b, 0, 0, cur(i))),
                  whole(D_ATT),
                  pl.BlockSpec((1, n_pairs, LANES, S), lambda b, i: (b, 0, 0, 0)),
                  whole(LANES), tile(D_ATT), tile(D_POOL), tile(D),
                  pl.BlockSpec(ada.shape, lambda b, i: (0, 0, 0)),
                  _resident(w_out.shape, lambda b, i: (0, 0)),
                  const2(b_out), const2(ln_g), const2(ln_b),
                  pl.BlockSpec(diag_mask.shape, lambda b, i: (0, 0, 0))],
        out_specs=pl.BlockSpec((1, Q_TILE, D), lambda b, i: (b, prev(i), 0)),
        out_shape=jax.ShapeDtypeStruct((B, S, D), F32),
        scratch_shapes=[stat_buf, pltpu.VMEM((N_HEADS, PV_ROWS, Q_TILE), F32),
                        score_buf, score_buf, stat_buf, stat_buf,
                        pltpu.VMEM((N_HEADS, 2 * LANES, Q_TILE), BF16),
                        pltpu.VMEM(w_out.shape, BF16),
                        pltpu.VMEM((Q_TILE, D), F32)],
        compiler_params=pltpu.CompilerParams(
            dimension_semantics=("arbitrary", "arbitrary"),
            vmem_limit_bytes=VMEM_LIMIT_BYTES),
    )(qt, k, vt, g, ga, yp, x, ada, w_out, b_out, ln_g, ln_b, diag_mask)


def _layer(x, c, w_ada, b_ada, w_in, b_in, w_pool_mix, b_pool_mix, pool_scale,
           w_out, b_out, ln_g, ln_b):
    ada = _ada_call(c, w_ada, b_ada[None, :])
    qt, k, vt, g, ga, yp = _in_call(x, ada, w_in.T, b_in[None, :], w_pool_mix, b_pool_mix,
                                    pool_scale[None, :])
    return _attn_call(qt, k, vt, g, ga, yp, x, ada, w_out, b_out[None, :], ln_g[None, :],
                      ln_b[None, :])


def kernel(x, c, w_ada, b_ada, w_in, b_in, w_pool_mix, b_pool_mix, pool_scale, w_out, b_out,
           ln_g, ln_b):
    for layer in range(w_ada.shape[0]):
        x = _layer(x, c, w_ada[layer], b_ada[layer], w_in[layer], b_in[layer],
                   w_pool_mix[layer], b_pool_mix[layer], pool_scale[layer],
                   w_out[layer], b_out[layer], ln_g[layer], ln_b[layer])
    return x
```

```python
import math

import jax
import jax.numpy as jnp
from jax import lax
from jax.experimental import pallas as pl
from jax.experimental.pallas import tpu as pltpu

D_MODEL = 1024
D_ATT = 512
D_POOL = 512
N_HEADS = 8
HEAD_DIM = 64
POOL_WINDOWS = (2, 4, 8, 16)
POOL_GROUP_DIM = 128
POOL_HISTORY = 16
LN_EPS = 1e-5
DEEPNORM_ALPHA = 2.0 ** 0.25
LOG2E = math.log2(math.e)
NEG_BIG = -1e30

LANES = 128
HEADS_PER_BLOCK = LANES // HEAD_DIM
G_TERMS = 3
CUM_ROWS = 128
PV_ROWS = HEAD_DIM + 16

IN_SOURCE = (("q", D_ATT), ("k", D_ATT), ("v", D_ATT), ("f", N_HEADS), ("p", D_POOL),
             ("ga", D_ATT), ("gp", D_POOL))
IN_WIDTH = {"f": LANES, "p": D_POOL, "gp": D_POOL, "q": D_ATT, "v": D_ATT, "k": D_ATT,
            "ga": D_ATT}
IN_OFFSET = dict(zip(IN_WIDTH, (sum(list(IN_WIDTH.values())[:i]) for i in range(len(IN_WIDTH)))))
IN_COLS = sum(IN_WIDTH.values())

SEQ_TILE = 512
Q_TILE = 256
KV_TILE = 512
VMEM_LIMIT_BYTES = 56 * 1024 * 1024

F32 = jnp.float32
BF16 = jnp.bfloat16


def _silu(x):
    return x * jax.nn.sigmoid(x)


def _dot(a, b):
    return jnp.dot(a, b, preferred_element_type=F32)


def _resident(shape, index_map):
    return pl.BlockSpec(shape, index_map, pipeline_mode=pl.Buffered(1))


def _ada_kernel(c_ref, w_ref, b_ref, o_ref, cpad_ref):
    n_rows = c_ref.shape[0]
    cpad_ref[...] = jnp.zeros_like(cpad_ref)
    cpad_ref[0:n_rows, :] = c_ref[...]
    sc = _silu(cpad_ref[...].T)
    for b in range(n_rows):
        o_ref[0, b:b + 1, :] = (jnp.sum(w_ref[...] * sc[:, b:b + 1], axis=0, keepdims=True)
                                + b_ref[...])


def _ada_call(c, w_ada, b_ada):
    n_rows, d = c.shape
    n_chunks = w_ada.shape[1] // d
    return pl.pallas_call(
        _ada_kernel,
        name="ada_vector",
        grid=(n_chunks,),
        in_specs=[
            pl.BlockSpec((n_rows, d), lambda j: (0, 0)),
            pl.BlockSpec((d, d), lambda j: (0, j)),
            pl.BlockSpec((1, d), lambda j: (0, j)),
        ],
        out_specs=pl.BlockSpec((1, n_rows, d), lambda j: (j, 0, 0)),
        out_shape=jax.ShapeDtypeStruct((n_chunks, n_rows, d), F32),
        scratch_shapes=[pltpu.VMEM((LANES, d), F32)],
        compiler_params=pltpu.CompilerParams(dimension_semantics=("parallel",),
                                             vmem_limit_bytes=VMEM_LIMIT_BYTES),
    )(c, w_ada, b_ada)


def _split3(x):
    hi = x.astype(BF16).astype(F32)
    r = x - hi
    mid = r.astype(BF16).astype(F32)
    lo = (r - mid).astype(BF16).astype(F32)
    return hi, mid, lo


def _in_kernel(x_ref, ada_ref, wt_ref, b_ref, wpm_ref, bpm_ref, ps_ref,
               q_ref, k_ref, v_ref, g_ref, ga_ref, yp_ref,
               fcarry_ref, pcarry_ref, wcat_ref, bcat_ref, wmix_ref):
    bi = pl.program_id(0)
    si = pl.program_id(1)
    ts = x_ref.shape[1]

    @pl.when((bi == 0) & (si == 0))
    def _():
        src = 0
        for name, width in IN_SOURCE:
            dst, wide = IN_OFFSET[name], IN_WIDTH[name]
            w_seg = wt_ref[src:src + wide, :].T
            b_seg = b_ref[:, src:src + wide]
            if width < wide:
                w_seg = jnp.where(lax.broadcasted_iota(jnp.int32, w_seg.shape, 1) < width,
                                  w_seg, 0.0)
                b_seg = jnp.where(lax.broadcasted_iota(jnp.int32, b_seg.shape, 1) < width,
                                  b_seg, 0.0)
            wcat_ref[:, dst:dst + wide] = w_seg.astype(BF16)
            bcat_ref[:, dst:dst + wide] = b_seg
            src += width
        gd = POOL_GROUP_DIM
        wmix_ref[...] = jnp.zeros_like(wmix_ref)
        for g in range(len(POOL_WINDOWS)):
            o = (g % 2) * gd
            wmix_ref[g // 2, o:o + gd, o:o + gd] = wpm_ref[g].astype(BF16)

    @pl.when(si == 0)
    def _():
        fcarry_ref[...] = jnp.zeros_like(fcarry_ref)
        pcarry_ref[...] = jnp.zeros_like(pcarry_ref)

    shift = ada_ref[0, pl.ds(bi, 1), :]
    scale = ada_ref[1, pl.ds(bi, 1), :]
    u = (x_ref[0] * (1.0 + scale) + shift).astype(BF16)
    proj = _dot(u, wcat_ref[...]) + bcat_ref[...]
    seg = lambda name: proj[:, IN_OFFSET[name]:IN_OFFSET[name] + IN_WIDTH[name]]

    q = seg("q") * (HEAD_DIM ** -0.5 * LOG2E)
    for blk in range(D_ATT // LANES):
        q_ref[0, blk] = q[:, blk * LANES:(blk + 1) * LANES].T.astype(BF16)

    k_ref[0] = seg("k").astype(BF16)

    v = seg("v")
    for blk in range(D_ATT // LANES):
        v_ref[0, blk] = v[:, blk * LANES:(blk + 1) * LANES].T.astype(BF16)

    fl = seg("f")
    logf = jnp.minimum(fl, 0.0) - jnp.log1p(jnp.exp(-jnp.abs(fl)))
    row = lax.broadcasted_iota(jnp.int32, (CUM_ROWS, CUM_ROWS), 0)
    col = lax.broadcasted_iota(jnp.int32, (CUM_ROWS, CUM_ROWS), 1)
    tri = jnp.where(row >= col, 1.0, 0.0).astype(BF16)
    terms = jnp.concatenate([t.astype(BF16) for t in _split3(logf)], axis=1)
    offset = fcarry_ref[0:1, :]
    cum_blocks = []
    for r in range(ts // CUM_ROWS):
        part = _dot(tri, terms[r * CUM_ROWS:(r + 1) * CUM_ROWS, :])
        local = part[:, :LANES] + part[:, LANES:2 * LANES] + part[:, 2 * LANES:]
        cum_blocks.append(local + offset)
        offset = cum_blocks[-1][CUM_ROWS - 1:CUM_ROWS, :]
    fcarry_ref[...] = jnp.broadcast_to(offset, fcarry_ref.shape)
    cum = jnp.concatenate(cum_blocks, axis=0)

    g_hi, g_mid, g_lo = _split3(cum * (-LOG2E))
    lane = lax.broadcasted_iota(jnp.int32, (ts, LANES), 1)
    g = jnp.where(lane < N_HEADS, g_hi,
                  jnp.where(lane < 2 * N_HEADS, pltpu.roll(g_mid, N_HEADS, axis=1),
                            jnp.where(lane < 3 * N_HEADS, pltpu.roll(g_lo, 2 * N_HEADS, axis=1),
                                      0.0)))
    g_ref[0] = g.astype(BF16)

    p = seg("p")
    pe = jnp.concatenate([pcarry_ref[...], p], axis=0)
    pcarry_ref[...] = p[ts - POOL_HISTORY:, :]
    gp = seg("gp")
    t_glob = si * ts + lax.broadcasted_iota(jnp.int32, (ts, POOL_GROUP_DIM), 0)
    pooled = []
    for g, w in enumerate(POOL_WINDOWS):
        sl = slice(g * POOL_GROUP_DIM, (g + 1) * POOL_GROUP_DIM)
        y = pe[:, sl]
        sh = 1
        while sh < w:
            y = y + pltpu.roll(y, sh, axis=0)
            sh *= 2
        cnt = jnp.minimum(t_glob + 1, w).astype(F32)
        pooled.append((y[POOL_HISTORY:, :] / cnt - p[:, sl]).astype(BF16))
    for j in range(len(POOL_WINDOWS) // 2):
        sl = slice(2 * j * POOL_GROUP_DIM, (2 * j + 2) * POOL_GROUP_DIM)
        bias = jnp.concatenate([bpm_ref[2 * j:2 * j + 1, :], bpm_ref[2 * j + 1:2 * j + 2, :]],
                               axis=1)
        mixed = _dot(jnp.concatenate(pooled[2 * j:2 * j + 2], axis=1), wmix_ref[j]) + bias
        yp_ref[0, :, sl] = (mixed * ps_ref[:, sl] * _silu(gp[:, sl])).astype(BF16)

    ga_ref[0] = _silu(seg("ga")).astype(BF16)


def _in_call(x, ada, w_in_t, b_in, w_pm, b_pm, pool_scale):
    B, S, D = x.shape
    ts = SEQ_TILE
    const2 = lambda b, s: (0, 0)
    const3 = lambda b, s: (0, 0, 0)
    tile = lambda width: pl.BlockSpec((1, ts, width), lambda b, s: (b, s, 0))
    n_pairs = D_ATT // LANES
    tile_t = pl.BlockSpec((1, n_pairs, LANES, ts), lambda b, s: (b, 0, 0, s))
    return pl.pallas_call(
        _in_kernel,
        name="input_stage",
        grid=(B, S // ts),
        in_specs=[tile(D), pl.BlockSpec(ada.shape, const3),
                  _resident(w_in_t.shape, const2), pl.BlockSpec(b_in.shape, const2),
                  pl.BlockSpec(w_pm.shape, const3), pl.BlockSpec(b_pm.shape, const2),
                  pl.BlockSpec(pool_scale.shape, const2)],
        out_specs=[tile_t, tile(D_ATT), tile_t, tile(LANES),
                   tile(D_ATT), tile(D_POOL)],
        out_shape=[jax.ShapeDtypeStruct((B, n_pairs, LANES, S), BF16),
                   jax.ShapeDtypeStruct((B, S, D_ATT), BF16),
                   jax.ShapeDtypeStruct((B, n_pairs, LANES, S), BF16),
                   jax.ShapeDtypeStruct((B, S, LANES), BF16),
                   jax.ShapeDtypeStruct((B, S, D_ATT), BF16),
                   jax.ShapeDtypeStruct((B, S, D_POOL), BF16)],
        scratch_shapes=[pltpu.VMEM((8, LANES), F32),
                        pltpu.VMEM((POOL_HISTORY, D_POOL), F32),
                        pltpu.VMEM((D, IN_COLS), BF16),
                        pltpu.VMEM((1, IN_COLS), F32),
                        pltpu.VMEM((len(POOL_WINDOWS) // 2, 2 * POOL_GROUP_DIM,
                                    2 * POOL_GROUP_DIM), BF16)],
        compiler_params=pltpu.CompilerParams(
            dimension_semantics=("arbitrary", "arbitrary"),
            vmem_limit_bytes=VMEM_LIMIT_BYTES),
    )(x, ada, w_in_t, b_in, w_pm, b_pm, pool_scale)


def _attn_kernel(qt_ref, k_ref, vt_ref, g_ref, ga_ref, yp_ref, x_ref, ada_ref,
                 wo_ref, bo_ref, lg_ref, lb_ref, o_ref,
                 m_ref, acc_ref, s0_ref, s1_ref, mx0_ref, mx1_ref, rhs_ref, wob_ref):
    bi = pl.program_id(0)
    qi = pl.program_id(1)
    tq = qt_ref.shape[3]
    tk = KV_TILE
    n_main = (qi * tq) // tk
    diag_start = pl.multiple_of(n_main * tk, tk)
    diag_shift = qi * tq - diag_start

    @pl.when((bi == 0) & (qi == 0))
    def _():
        wob_ref[...] = wo_ref[...].astype(BF16)

    m_ref[...] = jnp.full_like(m_ref, NEG_BIG)
    acc_ref[...] = jnp.zeros_like(acc_ref)

    row = lax.broadcasted_iota(jnp.int32, (LANES, tq), 0)
    for h in range(N_HEADS):
        pair, half = divmod(h, HEADS_PER_BLOCK)
        own = (row // HEAD_DIM) == half
        rhs_ref[h, :LANES, :] = jnp.where(own, qt_ref[0, pair], jnp.zeros((), BF16))
        pick = (row < G_TERMS * N_HEADS) & (row % N_HEADS == h)
        rhs_ref[h, LANES:, :] = jnp.where(pick, 1.0, 0.0).astype(BF16)
    ones_row = lax.broadcasted_iota(jnp.int32, (PV_ROWS - HEAD_DIM, tk), 0) == 0

    def block_start(k):
        return pl.multiple_of(jnp.where(k == 0, diag_start, (k - 1) * tk), tk)

    def scores(k, s_ref, mx_ref, masked=False):
        start = block_start(k)
        gblk = g_ref[0, pl.ds(start, tk), :]
        for h in range(N_HEADS):
            pair = h // HEADS_PER_BLOCK
            kblk = k_ref[0, pl.ds(start, tk), pair * LANES:(pair + 1) * LANES]
            lhs = jnp.concatenate([kblk, gblk], axis=1)
            s = _dot(lhs, rhs_ref[h])
            if masked:
                key = lax.broadcasted_iota(jnp.int32, (tk, tq), 0)
                qry = lax.broadcasted_iota(jnp.int32, (tk, tq), 1)
                s = jnp.where(key <= qry + diag_shift, s, NEG_BIG)
            s_ref[h] = s
            mx_ref[h] = jnp.max(s.reshape(tk // 8, 8, tq), axis=0)

    def softmax_pv(k, s_ref, mx_ref):
        start = block_start(k)

        def probs(h):
            m_prev = m_ref[h]
            m_new = jnp.maximum(m_prev, jnp.max(mx_ref[h], axis=0, keepdims=True))
            m_ref[h] = m_new
            alpha = jnp.exp2(m_prev - m_new)
            p = jnp.exp2((s_ref[h] - m_new[0:1, :]).astype(BF16))
            return alpha, p

        def accumulate(h, alpha, p):
            pair, half = divmod(h, HEADS_PER_BLOCK)
            vt = vt_ref[0, pair, half * HEAD_DIM:(half + 1) * HEAD_DIM, pl.ds(start, tk)]
            ones = jnp.where(ones_row, 1.0, 0.0).astype(BF16)
            pv = _dot(jnp.concatenate([vt, ones], axis=0), p)
            acc_ref[h] = acc_ref[h] * alpha[0:1, :] + pv

        ap = {}
        for t in range(N_HEADS + 1):
            if t < N_HEADS:
                ap[t] = probs(t)
            if t >= 1:
                accumulate(t - 1, *ap.pop(t - 1))

    n_blocks = n_main + 1
    scores(0, s0_ref, mx0_ref, masked=True)

    def pair_of_blocks(i, carry):
        k = 2 * i
        scores(k + 1, s1_ref, mx1_ref)
        softmax_pv(k, s0_ref, mx0_ref)
        scores(k + 2, s0_ref, mx0_ref)
        softmax_pv(k + 1, s1_ref, mx1_ref)
        return carry

    n_pairs = (n_blocks - 1) // 2
    lax.fori_loop(0, n_pairs, pair_of_blocks, 0)
    k_last = 2 * n_pairs

    @pl.when(n_blocks - k_last == 2)
    def _():
        scores(k_last + 1, s1_ref, mx1_ref)
        softmax_pv(k_last, s0_ref, mx0_ref)
        softmax_pv(k_last + 1, s1_ref, mx1_ref)

    @pl.when(n_blocks - k_last == 1)
    def _():
        softmax_pv(k_last, s0_ref, mx0_ref)

    gated = []
    for pair in range(N_HEADS // HEADS_PER_BLOCK):
        outs = []
        for half in range(HEADS_PER_BLOCK):
            acc = acc_ref[pair * HEADS_PER_BLOCK + half]
            outs.append(acc[:HEAD_DIM, :] / acc[HEAD_DIM:HEAD_DIM + 1, :])
        att = jnp.concatenate(outs, axis=0).T
        sl = slice(pair * LANES, (pair + 1) * LANES)
        gated.append((att * ga_ref[0, :, sl].astype(F32)).astype(BF16))
    ya = jnp.concatenate(gated, axis=1)
    y = _dot(jnp.concatenate([ya, yp_ref[0]], axis=1), wob_ref[...]) + bo_ref[...]
    gate = ada_ref[2, pl.ds(bi, 1), :]
    hres = DEEPNORM_ALPHA * x_ref[0] + gate * y
    mu = jnp.mean(hres, axis=-1, keepdims=True)
    d = hres - mu
    var = jnp.mean(d * d, axis=-1, keepdims=True)
    o_ref[0] = d * lax.rsqrt(var + LN_EPS) * lg_ref[...] + lb_ref[...]


def _attn_call(qt, k, vt, g, ga, yp, x, ada, w_out, b_out, ln_g, ln_b):
    B, n_pairs, _, S = qt.shape
    D = x.shape[2]
    assert KV_TILE % Q_TILE == 0 and S % KV_TILE == 0
    tile = lambda width: pl.BlockSpec((1, Q_TILE, width), lambda b, i: (b, i, 0))
    whole = lambda width: pl.BlockSpec((1, S, width), lambda b, i: (b, 0, 0))
    const2 = lambda a: pl.BlockSpec(a.shape, lambda b, i: (0, 0))
    score_buf = pltpu.VMEM((N_HEADS, KV_TILE, Q_TILE), F32)
    stat_buf = pltpu.VMEM((N_HEADS, 8, Q_TILE), F32)
    return pl.pallas_call(
        _attn_kernel,
        name="attention",
        grid=(B, S // Q_TILE),
        in_specs=[pl.BlockSpec((1, n_pairs, LANES, Q_TILE), lambda b, i: (b, 0, 0, i)),
                  whole(D_ATT),
                  pl.BlockSpec((1, n_pairs, LANES, S), lambda b, i: (b, 0, 0, 0)),
                  whole(LANES), tile(D_ATT), tile(D_POOL), tile(D),
                  pl.BlockSpec(ada.shape, lambda b, i: (0, 0, 0)),
                  _resident(w_out.shape, lambda b, i: (0, 0)),
                  const2(b_out), const2(ln_g), const2(ln_b)],
        out_specs=tile(D),
        out_shape=jax.ShapeDtypeStruct((B, S, D), F32),
        scratch_shapes=[stat_buf, pltpu.VMEM((N_HEADS, PV_ROWS, Q_TILE), F32),
                        score_buf, score_buf, stat_buf, stat_buf,
                        pltpu.VMEM((N_HEADS, 2 * LANES, Q_TILE), BF16),
                        pltpu.VMEM(w_out.shape, BF16)],
        compiler_params=pltpu.CompilerParams(
            dimension_semantics=("arbitrary", "arbitrary"),
            vmem_limit_bytes=VMEM_LIMIT_BYTES),
    )(qt, k, vt, g, ga, yp, x, ada, w_out, b_out, ln_g, ln_b)


def _layer(x, c, w_ada, b_ada, w_in, b_in, w_pool_mix, b_pool_mix, pool_scale,
           w_out, b_out, ln_g, ln_b):
    ada = _ada_call(c, w_ada, b_ada[None, :])
    qt, k, vt, g, ga, yp = _in_call(x, ada, w_in.T, b_in[None, :], w_pool_mix, b_pool_mix,
                                    pool_scale[None, :])
    return _attn_call(qt, k, vt, g, ga, yp, x, ada, w_out, b_out[None, :], ln_g[None, :],
                      ln_b[None, :])


def kernel(x, c, w_ada, b_ada, w_in, b_in, w_pool_mix, b_pool_mix, pool_scale, w_out, b_out,
           ln_g, ln_b):
    for layer in range(w_ada.shape[0]):
        x = _layer(x, c, w_ada[layer], b_ada[layer], w_in[layer], b_in[layer],
                   w_pool_mix[layer], b_pool_mix[layer], pool_scale[layer],
                   w_out[layer], b_out[layer], ln_g[layer], ln_b[layer])
    return x
```

```python
import math

import jax
import jax.numpy as jnp
from jax import lax
from jax.experimental import pallas as pl
from jax.experimental.pallas import tpu as pltpu

D_MODEL = 1024
D_ATT = 512
D_POOL = 512
N_HEADS = 8
HEAD_DIM = 64
POOL_WINDOWS = (2, 4, 8, 16)
POOL_GROUP_DIM = 128
POOL_HISTORY = 16
LN_EPS = 1e-5
DEEPNORM_ALPHA = 2.0 ** 0.25
LOG2E = math.log2(math.e)
NEG_BIG = -1e30

LANES = 128
HEADS_PER_BLOCK = LANES // HEAD_DIM
G_TERMS = 3
CUM_ROWS = 128
PV_ROWS = HEAD_DIM + 16

IN_SOURCE = (("q", D_ATT), ("k", D_ATT), ("v", D_ATT), ("f", N_HEADS), ("p", D_POOL),
             ("ga", D_ATT), ("gp", D_POOL))
IN_WIDTH = {"f": LANES, "p": D_POOL, "gp": D_POOL, "q": D_ATT, "v": D_ATT, "k": D_ATT,
            "ga": D_ATT}
IN_OFFSET = dict(zip(IN_WIDTH, (sum(list(IN_WIDTH.values())[:i]) for i in range(len(IN_WIDTH)))))
IN_COLS = sum(IN_WIDTH.values())

SEQ_TILE = 512
Q_TILE = 256
KV_TILE = 512
VMEM_LIMIT_BYTES = 56 * 1024 * 1024

F32 = jnp.float32
BF16 = jnp.bfloat16


def _silu(x):
    return x * jax.nn.sigmoid(x)


def _dot(a, b):
    return jnp.dot(a, b, preferred_element_type=F32)


def _resident(shape, index_map):
    return pl.BlockSpec(shape, index_map, pipeline_mode=pl.Buffered(1))


def _ada_kernel(c_ref, w_ref, b_ref, o_ref, cpad_ref):
    n_rows = c_ref.shape[0]
    cpad_ref[...] = jnp.zeros_like(cpad_ref)
    cpad_ref[0:n_rows, :] = c_ref[...]
    sc = _silu(cpad_ref[...].T)
    for b in range(n_rows):
        o_ref[0, b:b + 1, :] = (jnp.sum(w_ref[...] * sc[:, b:b + 1], axis=0, keepdims=True)
                                + b_ref[...])


def _ada_call(c, w_ada, b_ada):
    n_rows, d = c.shape
    n_chunks = w_ada.shape[1] // d
    return pl.pallas_call(
        _ada_kernel,
        name="ada_vector",
        grid=(n_chunks,),
        in_specs=[
            pl.BlockSpec((n_rows, d), lambda j: (0, 0)),
            pl.BlockSpec((d, d), lambda j: (0, j)),
            pl.BlockSpec((1, d), lambda j: (0, j)),
        ],
        out_specs=pl.BlockSpec((1, n_rows, d), lambda j: (j, 0, 0)),
        out_shape=jax.ShapeDtypeStruct((n_chunks, n_rows, d), F32),
        scratch_shapes=[pltpu.VMEM((LANES, d), F32)],
        compiler_params=pltpu.CompilerParams(dimension_semantics=("parallel",),
                                             vmem_limit_bytes=VMEM_LIMIT_BYTES),
    )(c, w_ada, b_ada)


def _split3(x):
    hi = x.astype(BF16).astype(F32)
    r = x - hi
    mid = r.astype(BF16).astype(F32)
    lo = (r - mid).astype(BF16).astype(F32)
    return hi, mid, lo


def _in_kernel(x_ref, ada_ref, wt_ref, b_ref, wpm_ref, bpm_ref, ps_ref,
               q_ref, k_ref, v_ref, g_ref, ga_ref, yp_ref,
               fcarry_ref, pcarry_ref, wcat_ref, bcat_ref, wmix_ref):
    bi = pl.program_id(0)
    si = pl.program_id(1)
    ts = x_ref.shape[1]

    @pl.when((bi == 0) & (si == 0))
    def _():
        src = 0
        for name, width in IN_SOURCE:
            dst, wide = IN_OFFSET[name], IN_WIDTH[name]
            w_seg = wt_ref[src:src + wide, :].T
            b_seg = b_ref[:, src:src + wide]
            if width < wide:
                w_seg = jnp.where(lax.broadcasted_iota(jnp.int32, w_seg.shape, 1) < width,
                                  w_seg, 0.0)
                b_seg = jnp.where(lax.broadcasted_iota(jnp.int32, b_seg.shape, 1) < width,
                                  b_seg, 0.0)
            wcat_ref[:, dst:dst + wide] = w_seg.astype(BF16)
            bcat_ref[:, dst:dst + wide] = b_seg
            src += width
        gd = POOL_GROUP_DIM
        wmix_ref[...] = jnp.zeros_like(wmix_ref)
        for g in range(len(POOL_WINDOWS)):
            o = (g % 2) * gd
            wmix_ref[g // 2, o:o + gd, o:o + gd] = wpm_ref[g].astype(BF16)

    @pl.when(si == 0)
    def _():
        fcarry_ref[...] = jnp.zeros_like(fcarry_ref)
        pcarry_ref[...] = jnp.zeros_like(pcarry_ref)

    shift = ada_ref[0, pl.ds(bi, 1), :]
    scale = ada_ref[1, pl.ds(bi, 1), :]
    u = (x_ref[0] * (1.0 + scale) + shift).astype(BF16)
    proj = _dot(u, wcat_ref[...]) + bcat_ref[...]
    seg = lambda name: proj[:, IN_OFFSET[name]:IN_OFFSET[name] + IN_WIDTH[name]]

    q = seg("q") * (HEAD_DIM ** -0.5 * LOG2E)
    for blk in range(D_ATT // LANES):
        q_ref[0, blk] = q[:, blk * LANES:(blk + 1) * LANES].T.astype(BF16)

    k_ref[0] = seg("k").astype(BF16)

    v = seg("v")
    for blk in range(D_ATT // LANES):
        v_ref[0, blk] = v[:, blk * LANES:(blk + 1) * LANES].T.astype(BF16)

    fl = seg("f")
    logf = jnp.minimum(fl, 0.0) - jnp.log1p(jnp.exp(-jnp.abs(fl)))
    row = lax.broadcasted_iota(jnp.int32, (CUM_ROWS, CUM_ROWS), 0)
    col = lax.broadcasted_iota(jnp.int32, (CUM_ROWS, CUM_ROWS), 1)
    tri = jnp.where(row >= col, 1.0, 0.0).astype(BF16)
    terms = jnp.concatenate([t.astype(BF16) for t in _split3(logf)], axis=1)
    offset = fcarry_ref[0:1, :]
    cum_blocks = []
    for r in range(ts // CUM_ROWS):
        part = _dot(tri, terms[r * CUM_ROWS:(r + 1) * CUM_ROWS, :])
        local = part[:, :LANES] + part[:, LANES:2 * LANES] + part[:, 2 * LANES:]
        cum_blocks.append(local + offset)
        offset = cum_blocks[-1][CUM_ROWS - 1:CUM_ROWS, :]
    fcarry_ref[...] = jnp.broadcast_to(offset, fcarry_ref.shape)
    cum = jnp.concatenate(cum_blocks, axis=0)

    g_hi, g_mid, g_lo = _split3(cum * (-LOG2E))
    lane = lax.broadcasted_iota(jnp.int32, (ts, LANES), 1)
    g = jnp.where(lane < N_HEADS, g_hi,
                  jnp.where(lane < 2 * N_HEADS, pltpu.roll(g_mid, N_HEADS, axis=1),
                            jnp.where(lane < 3 * N_HEADS, pltpu.roll(g_lo, 2 * N_HEADS, axis=1),
                                      0.0)))
    g_ref[0] = g.astype(BF16)

    p = seg("p")
    pe = jnp.concatenate([pcarry_ref[...], p], axis=0)
    pcarry_ref[...] = p[ts - POOL_HISTORY:, :]
    gp = seg("gp")
    t_glob = si * ts + lax.broadcasted_iota(jnp.int32, (ts, POOL_GROUP_DIM), 0)
    pooled = []
    for g, w in enumerate(POOL_WINDOWS):
        sl = slice(g * POOL_GROUP_DIM, (g + 1) * POOL_GROUP_DIM)
        y = pe[:, sl]
        sh = 1
        while sh < w:
            y = y + pltpu.roll(y, sh, axis=0)
            sh *= 2
        cnt = jnp.minimum(t_glob + 1, w).astype(F32)
        pooled.append((y[POOL_HISTORY:, :] / cnt - p[:, sl]).astype(BF16))
    for j in range(len(POOL_WINDOWS) // 2):
        sl = slice(2 * j * POOL_GROUP_DIM, (2 * j + 2) * POOL_GROUP_DIM)
        bias = jnp.concatenate([bpm_ref[2 * j:2 * j + 1, :], bpm_ref[2 * j + 1:2 * j + 2, :]],
                               axis=1)
        mixed = _dot(jnp.concatenate(pooled[2 * j:2 * j + 2], axis=1), wmix_ref[j]) + bias
        yp_ref[0, :, sl] = (mixed * ps_ref[:, sl] * _silu(gp[:, sl])).astype(BF16)

    ga_ref[0] = _silu(seg("ga")).astype(BF16)


def _in_call(x, ada, w_in_t, b_in, w_pm, b_pm, pool_scale):
    B, S, D = x.shape
    ts = SEQ_TILE
    const2 = lambda b, s: (0, 0)
    const3 = lambda b, s: (0, 0, 0)
    tile = lambda width: pl.BlockSpec((1, ts, width), lambda b, s: (b, s, 0))
    n_pairs = D_ATT // LANES
    tile_t = pl.BlockSpec((1, n_pairs, LANES, ts), lambda b, s: (b, 0, 0, s))
    return pl.pallas_call(
        _in_kernel,
        name="input_stage",
        grid=(B, S // ts),
        in_specs=[tile(D), pl.BlockSpec(ada.shape, const3),
                  _resident(w_in_t.shape, const2), pl.BlockSpec(b_in.shape, const2),
                  pl.BlockSpec(w_pm.shape, const3), pl.BlockSpec(b_pm.shape, const2),
                  pl.BlockSpec(pool_scale.shape, const2)],
        out_specs=[tile_t, tile(D_ATT), tile_t, tile(LANES),
                   tile(D_ATT), tile(D_POOL)],
        out_shape=[jax.ShapeDtypeStruct((B, n_pairs, LANES, S), BF16),
                   jax.ShapeDtypeStruct((B, S, D_ATT), BF16),
                   jax.ShapeDtypeStruct((B, n_pairs, LANES, S), BF16),
                   jax.ShapeDtypeStruct((B, S, LANES), BF16),
                   jax.ShapeDtypeStruct((B, S, D_ATT), BF16),
                   jax.ShapeDtypeStruct((B, S, D_POOL), BF16)],
        scratch_shapes=[pltpu.VMEM((8, LANES), F32),
                        pltpu.VMEM((POOL_HISTORY, D_POOL), F32),
                        pltpu.VMEM((D, IN_COLS), BF16),
                        pltpu.VMEM((1, IN_COLS), F32),
                        pltpu.VMEM((len(POOL_WINDOWS) // 2, 2 * POOL_GROUP_DIM,
                                    2 * POOL_GROUP_DIM), BF16)],
        compiler_params=pltpu.CompilerParams(
            dimension_semantics=("arbitrary", "arbitrary"),
            vmem_limit_bytes=VMEM_LIMIT_BYTES),
    )(x, ada, w_in_t, b_in, w_pm, b_pm, pool_scale)


def _attn_kernel(qt_ref, k_ref, vt_ref, g_ref, ga_ref, yp_ref, x_ref, ada_ref,
                 wo_ref, bo_ref, lg_ref, lb_ref, o_ref,
                 m_ref, acc_ref, s0_ref, s1_ref, mx0_ref, mx1_ref, rhs_ref, wob_ref):
    bi = pl.program_id(0)
    qi = pl.program_id(1)
    tq = qt_ref.shape[3]
    tk = KV_TILE
    n_main = (qi * tq) // tk
    diag_start = pl.multiple_of(n_main * tk, tk)
    diag_shift = qi * tq - diag_start

    @pl.when((bi == 0) & (qi == 0))
    def _():
        wob_ref[...] = wo_ref[...].astype(BF16)

    m_ref[...] = jnp.full_like(m_ref, NEG_BIG)
    acc_ref[...] = jnp.zeros_like(acc_ref)

    row = lax.broadcasted_iota(jnp.int32, (LANES, tq), 0)
    for h in range(N_HEADS):
        pair, half = divmod(h, HEADS_PER_BLOCK)
        own = (row // HEAD_DIM) == half
        rhs_ref[h, :LANES, :] = jnp.where(own, qt_ref[0, pair], jnp.zeros((), BF16))
        pick = (row < G_TERMS * N_HEADS) & (row % N_HEADS == h)
        rhs_ref[h, LANES:, :] = jnp.where(pick, 1.0, 0.0).astype(BF16)
    ones_row = lax.broadcasted_iota(jnp.int32, (PV_ROWS - HEAD_DIM, tk), 0) == 0

    def block_start(k):
        return pl.multiple_of(jnp.where(k == 0, diag_start, (k - 1) * tk), tk)

    def scores(k, s_ref, mx_ref, masked=False):
        start = block_start(k)

        def head(h):
            pair = h // HEADS_PER_BLOCK
            kblk = k_ref[0, pl.ds(start, tk), pair * LANES:(pair + 1) * LANES]
            gblk = g_ref[0, pl.ds(start, tk), :]
            lhs = jnp.concatenate([kblk, gblk], axis=1)
            s = _dot(lhs, rhs_ref[h])
            if masked:
                key = lax.broadcasted_iota(jnp.int32, (tk, tq), 0)
                qry = lax.broadcasted_iota(jnp.int32, (tk, tq), 1)
                s = jnp.where(key <= qry + diag_shift, s, NEG_BIG)
            s_ref[h] = s
            mx_ref[h] = jnp.max(s.reshape(tk // 8, 8, tq), axis=0)

        return [lambda h=h: head(h) for h in range(N_HEADS)]

    def softmax_pv(k, s_ref, mx_ref):
        start = block_start(k)
        ap = {}

        def probs(h):
            m_prev = m_ref[h]
            m_new = jnp.maximum(m_prev, jnp.max(mx_ref[h], axis=0, keepdims=True))
            m_ref[h] = m_new
            alpha = jnp.exp2(m_prev - m_new)
            p = jnp.exp2(s_ref[h] - m_new[0:1, :]).astype(BF16)
            ap[h] = (alpha, p)

        def accumulate(h):
            alpha, p = ap.pop(h)
            pair, half = divmod(h, HEADS_PER_BLOCK)
            vt = vt_ref[0, pair, half * HEAD_DIM:(half + 1) * HEAD_DIM, pl.ds(start, tk)]
            ones = jnp.where(ones_row, 1.0, 0.0).astype(BF16)
            pv = _dot(jnp.concatenate([vt, ones], axis=0), p)
            acc_ref[h] = acc_ref[h] * alpha[0:1, :] + pv

        def step(t):
            if t < N_HEADS:
                probs(t)
            if t >= 1:
                accumulate(t - 1)

        return [lambda t=t: step(t) for t in range(N_HEADS + 1)]

    def run(*stages):
        for i in range(max(len(st) for st in stages)):
            for st in stages:
                if i < len(st):
                    st[i]()

    n_blocks = n_main + 1
    run(scores(0, s0_ref, mx0_ref, masked=True))

    def pair_of_blocks(i, carry):
        k = 2 * i
        run(scores(k + 1, s1_ref, mx1_ref), softmax_pv(k, s0_ref, mx0_ref))
        run(scores(k + 2, s0_ref, mx0_ref), softmax_pv(k + 1, s1_ref, mx1_ref))
        return carry

    n_pairs = (n_blocks - 1) // 2
    lax.fori_loop(0, n_pairs, pair_of_blocks, 0)
    k_last = 2 * n_pairs

    @pl.when(n_blocks - k_last == 2)
    def _():
        run(scores(k_last + 1, s1_ref, mx1_ref), softmax_pv(k_last, s0_ref, mx0_ref))
        run(softmax_pv(k_last + 1, s1_ref, mx1_ref))

    @pl.when(n_blocks - k_last == 1)
    def _():
        run(softmax_pv(k_last, s0_ref, mx0_ref))

    gated = []
    for pair in range(N_HEADS // HEADS_PER_BLOCK):
        outs = []
        for half in range(HEADS_PER_BLOCK):
            acc = acc_ref[pair * HEADS_PER_BLOCK + half]
            outs.append(acc[:HEAD_DIM, :] / acc[HEAD_DIM:HEAD_DIM + 1, :])
        att = jnp.concatenate(outs, axis=0).T
        sl = slice(pair * LANES, (pair + 1) * LANES)
        gated.append((att * ga_ref[0, :, sl].astype(F32)).astype(BF16))
    ya = jnp.concatenate(gated, axis=1)
    y = _dot(jnp.concatenate([ya, yp_ref[0]], axis=1), wob_ref[...]) + bo_ref[...]
    gate = ada_ref[2, pl.ds(bi, 1), :]
    hres = DEEPNORM_ALPHA * x_ref[0] + gate * y
    mu = jnp.mean(hres, axis=-1, keepdims=True)
    d = hres - mu
    var = jnp.mean(d * d, axis=-1, keepdims=True)
    o_ref[0] = d * lax.rsqrt(var + LN_EPS) * lg_ref[...] + lb_ref[...]


def _attn_call(qt, k, vt, g, ga, yp, x, ada, w_out, b_out, ln_g, ln_b):
    B, n_pairs, _, S = qt.shape
    D = x.shape[2]
    assert KV_TILE % Q_TILE == 0 and S % KV_TILE == 0
    tile = lambda width: pl.BlockSpec((1, Q_TILE, width), lambda b, i: (b, i, 0))
    whole = lambda width: pl.BlockSpec((1, S, width), lambda b, i: (b, 0, 0))
    const2 = lambda a: pl.BlockSpec(a.shape, lambda b, i: (0, 0))
    score_buf = pltpu.VMEM((N_HEADS, KV_TILE, Q_TILE), F32)
    stat_buf = pltpu.VMEM((N_HEADS, 8, Q_TILE), F32)
    return pl.pallas_call(
        _attn_kernel,
        name="attention",
        grid=(B, S // Q_TILE),
        in_specs=[pl.BlockSpec((1, n_pairs, LANES, Q_TILE), lambda b, i: (b, 0, 0, i)),
                  whole(D_ATT),
                  pl.BlockSpec((1, n_pairs, LANES, S), lambda b, i: (b, 0, 0, 0)),
                  whole(LANES), tile(D_ATT), tile(D_POOL), tile(D),
                  pl.BlockSpec(ada.shape, lambda b, i: (0, 0, 0)),
                  _resident(w_out.shape, lambda b, i: (0, 0)),
                  const2(b_out), const2(ln_g), const2(ln_b)],
        out_specs=tile(D),
        out_shape=jax.ShapeDtypeStruct((B, S, D), F32),
        scratch_shapes=[stat_buf, pltpu.VMEM((N_HEADS, PV_ROWS, Q_TILE), F32),
                        score_buf, score_buf, stat_buf, stat_buf,
                        pltpu.VMEM((N_HEADS, 2 * LANES, Q_TILE), BF16),
                        pltpu.VMEM(w_out.shape, BF16)],
        compiler_params=pltpu.CompilerParams(
            dimension_semantics=("arbitrary", "arbitrary"),
            vmem_limit_bytes=VMEM_LIMIT_BYTES),
    )(qt, k, vt, g, ga, yp, x, ada, w_out, b_out, ln_g, ln_b)


def _layer(x, c, w_ada, b_ada, w_in, b_in, w_pool_mix, b_pool_mix, pool_scale,
           w_out, b_out, ln_g, ln_b):
    ada = _ada_call(c, w_ada, b_ada[None, :])
    qt, k, vt, g, ga, yp = _in_call(x, ada, w_in.T, b_in[None, :], w_pool_mix, b_pool_mix,
                                    pool_scale[None, :])
    return _attn_call(qt, k, vt, g, ga, yp, x, ada, w_out, b_out[None, :], ln_g[None, :],
                      ln_b[None, :])


def kernel(x, c, w_ada, b_ada, w_in, b_in, w_pool_mix, b_pool_mix, pool_scale, w_out, b_out,
           ln_g, ln_b):
    for layer in range(w_ada.shape[0]):
        x = _layer(x, c, w_ada[layer], b_ada[layer], w_in[layer], b_in[layer],
                   w_pool_mix[layer], b_pool_mix[layer], pool_scale[layer],
                   w_out[layer], b_out[layer], ln_g[layer], ln_b[layer])
    return x
```

```python
import math

import jax
import jax.numpy as jnp
from jax import lax
from jax.experimental import pallas as pl
from jax.experimental.pallas import tpu as pltpu

D_MODEL = 1024
D_ATT = 512
D_POOL = 512
N_HEADS = 8
HEAD_DIM = 64
POOL_WINDOWS = (2, 4, 8, 16)
POOL_GROUP_DIM = 128
POOL_HISTORY = 16
LN_EPS = 1e-5
DEEPNORM_ALPHA = 2.0 ** 0.25
LOG2E = math.log2(math.e)
NEG_BIG = -1e30

LANES = 128
HEADS_PER_BLOCK = LANES // HEAD_DIM
G_TERMS = 3
CUM_ROWS = 128
PV_ROWS = HEAD_DIM + 16

IN_SOURCE = (("q", D_ATT), ("k", D_ATT), ("v", D_ATT), ("f", N_HEADS), ("p", D_POOL),
             ("ga", D_ATT), ("gp", D_POOL))
IN_WIDTH = {"f": LANES, "p": D_POOL, "gp": D_POOL, "q": D_ATT, "v": D_ATT, "k": D_ATT,
            "ga": D_ATT}
IN_OFFSET = dict(zip(IN_WIDTH, (sum(list(IN_WIDTH.values())[:i]) for i in range(len(IN_WIDTH)))))
IN_COLS = sum(IN_WIDTH.values())

SEQ_TILE = 512
Q_TILE = 256
KV_TILE = 512
VMEM_LIMIT_BYTES = 56 * 1024 * 1024

F32 = jnp.float32
BF16 = jnp.bfloat16


def _silu(x):
    return x * jax.nn.sigmoid(x)


def _dot(a, b):
    return jnp.dot(a, b, preferred_element_type=F32)


def _resident(shape, index_map):
    return pl.BlockSpec(shape, index_map, pipeline_mode=pl.Buffered(1))


def _ada_kernel(c_ref, w_ref, b_ref, o_ref, cpad_ref):
    n_rows = c_ref.shape[0]
    cpad_ref[...] = jnp.zeros_like(cpad_ref)
    cpad_ref[0:n_rows, :] = c_ref[...]
    sc = _silu(cpad_ref[...].T)
    for b in range(n_rows):
        o_ref[0, b:b + 1, :] = (jnp.sum(w_ref[...] * sc[:, b:b + 1], axis=0, keepdims=True)
                                + b_ref[...])


def _ada_call(c, w_ada, b_ada):
    n_rows, d = c.shape
    n_chunks = w_ada.shape[1] // d
    return pl.pallas_call(
        _ada_kernel,
        name="ada_vector",
        grid=(n_chunks,),
        in_specs=[
            pl.BlockSpec((n_rows, d), lambda j: (0, 0)),
            pl.BlockSpec((d, d), lambda j: (0, j)),
            pl.BlockSpec((1, d), lambda j: (0, j)),
        ],
        out_specs=pl.BlockSpec((1, n_rows, d), lambda j: (j, 0, 0)),
        out_shape=jax.ShapeDtypeStruct((n_chunks, n_rows, d), F32),
        scratch_shapes=[pltpu.VMEM((LANES, d), F32)],
        compiler_params=pltpu.CompilerParams(dimension_semantics=("parallel",),
                                             vmem_limit_bytes=VMEM_LIMIT_BYTES),
    )(c, w_ada, b_ada)


def _split3(x):
    hi = x.astype(BF16).astype(F32)
    r = x - hi
    mid = r.astype(BF16).astype(F32)
    lo = (r - mid).astype(BF16).astype(F32)
    return hi, mid, lo


def _in_kernel(x_ref, ada_ref, wt_ref, b_ref, wpm_ref, bpm_ref, ps_ref,
               q_ref, k_ref, v_ref, g_ref, ga_ref, yp_ref,
               fcarry_ref, pcarry_ref, wcat_ref, bcat_ref, wmix_ref):
    bi = pl.program_id(0)
    si = pl.program_id(1)
    ts = x_ref.shape[1]

    @pl.when((bi == 0) & (si == 0))
    def _():
        src = 0
        for name, width in IN_SOURCE:
            dst, wide = IN_OFFSET[name], IN_WIDTH[name]
            w_seg = wt_ref[src:src + wide, :].T
            b_seg = b_ref[:, src:src + wide]
            if width < wide:
                w_seg = jnp.where(lax.broadcasted_iota(jnp.int32, w_seg.shape, 1) < width,
                                  w_seg, 0.0)
                b_seg = jnp.where(lax.broadcasted_iota(jnp.int32, b_seg.shape, 1) < width,
                                  b_seg, 0.0)
            wcat_ref[:, dst:dst + wide] = w_seg.astype(BF16)
            bcat_ref[:, dst:dst + wide] = b_seg
            src += width
        gd = POOL_GROUP_DIM
        wmix_ref[...] = jnp.zeros_like(wmix_ref)
        for g in range(len(POOL_WINDOWS)):
            o = (g % 2) * gd
            wmix_ref[g // 2, o:o + gd, o:o + gd] = wpm_ref[g].astype(BF16)

    @pl.when(si == 0)
    def _():
        fcarry_ref[...] = jnp.zeros_like(fcarry_ref)
        pcarry_ref[...] = jnp.zeros_like(pcarry_ref)

    shift = ada_ref[0, pl.ds(bi, 1), :]
    scale = ada_ref[1, pl.ds(bi, 1), :]
    u = (x_ref[0] * (1.0 + scale) + shift).astype(BF16)
    proj = _dot(u, wcat_ref[...]) + bcat_ref[...]
    seg = lambda name: proj[:, IN_OFFSET[name]:IN_OFFSET[name] + IN_WIDTH[name]]

    q = seg("q") * (HEAD_DIM ** -0.5 * LOG2E)
    for blk in range(D_ATT // LANES):
        q_ref[0, blk] = q[:, blk * LANES:(blk + 1) * LANES].T.astype(BF16)

    k_ref[0] = seg("k").astype(BF16)

    v = seg("v")
    for blk in range(D_ATT // LANES):
        v_ref[0, blk] = v[:, blk * LANES:(blk + 1) * LANES].T.astype(BF16)

    fl = seg("f")
    logf = jnp.minimum(fl, 0.0) - jnp.log1p(jnp.exp(-jnp.abs(fl)))
    row = lax.broadcasted_iota(jnp.int32, (CUM_ROWS, CUM_ROWS), 0)
    col = lax.broadcasted_iota(jnp.int32, (CUM_ROWS, CUM_ROWS), 1)
    tri = jnp.where(row >= col, 1.0, 0.0).astype(BF16)
    terms = jnp.concatenate([t.astype(BF16) for t in _split3(logf)], axis=1)
    offset = fcarry_ref[0:1, :]
    cum_blocks = []
    for r in range(ts // CUM_ROWS):
        part = _dot(tri, terms[r * CUM_ROWS:(r + 1) * CUM_ROWS, :])
        local = part[:, :LANES] + part[:, LANES:2 * LANES] + part[:, 2 * LANES:]
        cum_blocks.append(local + offset)
        offset = cum_blocks[-1][CUM_ROWS - 1:CUM_ROWS, :]
    fcarry_ref[...] = jnp.broadcast_to(offset, fcarry_ref.shape)
    cum = jnp.concatenate(cum_blocks, axis=0)

    g_hi, g_mid, g_lo = _split3(cum * (-LOG2E))
    lane = lax.broadcasted_iota(jnp.int32, (ts, LANES), 1)
    g = jnp.where(lane < N_HEADS, g_hi,
                  jnp.where(lane < 2 * N_HEADS, pltpu.roll(g_mid, N_HEADS, axis=1),
                            jnp.where(lane < 3 * N_HEADS, pltpu.roll(g_lo, 2 * N_HEADS, axis=1),
                                      0.0)))
    g_ref[0] = g.astype(BF16)

    p = seg("p")
    pe = jnp.concatenate([pcarry_ref[...], p], axis=0)
    pcarry_ref[...] = p[ts - POOL_HISTORY:, :]
    gp = seg("gp")
    t_glob = si * ts + lax.broadcasted_iota(jnp.int32, (ts, POOL_GROUP_DIM), 0)
    pooled = []
    for g, w in enumerate(POOL_WINDOWS):
        sl = slice(g * POOL_GROUP_DIM, (g + 1) * POOL_GROUP_DIM)
        y = pe[:, sl]
        sh = 1
        while sh < w:
            y = y + pltpu.roll(y, sh, axis=0)
            sh *= 2
        cnt = jnp.minimum(t_glob + 1, w).astype(F32)
        pooled.append((y[POOL_HISTORY:, :] / cnt - p[:, sl]).astype(BF16))
    for j in range(len(POOL_WINDOWS) // 2):
        sl = slice(2 * j * POOL_GROUP_DIM, (2 * j + 2) * POOL_GROUP_DIM)
        bias = jnp.concatenate([bpm_ref[2 * j:2 * j + 1, :], bpm_ref[2 * j + 1:2 * j + 2, :]],
                               axis=1)
        mixed = _dot(jnp.concatenate(pooled[2 * j:2 * j + 2], axis=1), wmix_ref[j]) + bias
        yp_ref[0, :, sl] = (mixed * ps_ref[:, sl] * _silu(gp[:, sl])).astype(BF16)

    ga_ref[0] = _silu(seg("ga")).astype(BF16)


def _in_call(x, ada, w_in_t, b_in, w_pm, b_pm, pool_scale):
    B, S, D = x.shape
    ts = SEQ_TILE
    const2 = lambda b, s: (0, 0)
    const3 = lambda b, s: (0, 0, 0)
    tile = lambda width: pl.BlockSpec((1, ts, width), lambda b, s: (b, s, 0))
    n_pairs = D_ATT // LANES
    tile_t = pl.BlockSpec((1, n_pairs, LANES, ts), lambda b, s: (b, 0, 0, s))
    return pl.pallas_call(
        _in_kernel,
        name="input_stage",
        grid=(B, S // ts),
        in_specs=[tile(D), pl.BlockSpec(ada.shape, const3),
                  _resident(w_in_t.shape, const2), pl.BlockSpec(b_in.shape, const2),
                  pl.BlockSpec(w_pm.shape, const3), pl.BlockSpec(b_pm.shape, const2),
                  pl.BlockSpec(pool_scale.shape, const2)],
        out_specs=[tile_t, tile(D_ATT), tile_t, tile(LANES),
                   tile(D_ATT), tile(D_POOL)],
        out_shape=[jax.ShapeDtypeStruct((B, n_pairs, LANES, S), BF16),
                   jax.ShapeDtypeStruct((B, S, D_ATT), BF16),
                   jax.ShapeDtypeStruct((B, n_pairs, LANES, S), BF16),
                   jax.ShapeDtypeStruct((B, S, LANES), BF16),
                   jax.ShapeDtypeStruct((B, S, D_ATT), BF16),
                   jax.ShapeDtypeStruct((B, S, D_POOL), BF16)],
        scratch_shapes=[pltpu.VMEM((8, LANES), F32),
                        pltpu.VMEM((POOL_HISTORY, D_POOL), F32),
                        pltpu.VMEM((D, IN_COLS), BF16),
                        pltpu.VMEM((1, IN_COLS), F32),
                        pltpu.VMEM((len(POOL_WINDOWS) // 2, 2 * POOL_GROUP_DIM,
                                    2 * POOL_GROUP_DIM), BF16)],
        compiler_params=pltpu.CompilerParams(
            dimension_semantics=("arbitrary", "arbitrary"),
            vmem_limit_bytes=VMEM_LIMIT_BYTES),
    )(x, ada, w_in_t, b_in, w_pm, b_pm, pool_scale)


def _attn_kernel(qt_ref, k_ref, vt_ref, g_ref, ga_ref, yp_ref, x_ref, ada_ref,
                 wo_ref, bo_ref, lg_ref, lb_ref, o_ref,
                 m_ref, acc_ref, s0_ref, s1_ref, mx0_ref, mx1_ref, rhs_ref, wob_ref):
    bi = pl.program_id(0)
    qi = pl.program_id(1)
    tq = qt_ref.shape[3]
    tk = KV_TILE
    n_main = (qi * tq) // tk
    diag_start = pl.multiple_of(n_main * tk, tq)
    lead = qi * tq - n_main * tk

    @pl.when((bi == 0) & (qi == 0))
    def _():
        wob_ref[...] = wo_ref[...].astype(BF16)

    m_ref[...] = jnp.full_like(m_ref, NEG_BIG)
    acc_ref[...] = jnp.zeros_like(acc_ref)

    row = lax.broadcasted_iota(jnp.int32, (LANES, tq), 0)
    for h in range(N_HEADS):
        pair, half = divmod(h, HEADS_PER_BLOCK)
        own = (row // HEAD_DIM) == half
        rhs_ref[h, :LANES, :] = jnp.where(own, qt_ref[0, pair], jnp.zeros((), BF16))
        pick = (row < G_TERMS * N_HEADS) & (row % N_HEADS == h)
        rhs_ref[h, LANES:, :] = jnp.where(pick, 1.0, 0.0).astype(BF16)

    def scores(start, width, s_ref, mx_ref, diagonal=False):
        def head(h):
            pair = h // HEADS_PER_BLOCK
            kblk = k_ref[0, pl.ds(start, width), pair * LANES:(pair + 1) * LANES]
            gblk = g_ref[0, pl.ds(start, width), :]
            lhs = jnp.concatenate([kblk, gblk], axis=1)
            s = _dot(lhs, rhs_ref[h])
            if diagonal:
                key = lax.broadcasted_iota(jnp.int32, (tq, tq), 0)
                qry = lax.broadcasted_iota(jnp.int32, (tq, tq), 1)
                own = jnp.where(key <= qry, s[width - tq:, :], NEG_BIG)
                s = own if width == tq else jnp.concatenate([s[:width - tq, :], own], axis=0)
            s_ref[h, :width, :] = s
            mx_ref[h] = jnp.max(s.reshape(width // 8, 8, tq), axis=0)

        return [lambda h=h: head(h) for h in range(N_HEADS)]

    def softmax_pv(start, width, s_ref, mx_ref):
        ap = {}

        def probs(h):
            m_prev = m_ref[h]
            m_new = jnp.maximum(m_prev, jnp.max(mx_ref[h], axis=0, keepdims=True))
            m_ref[h] = m_new
            alpha = jnp.exp2(m_prev - m_new)
            p = jnp.exp2(s_ref[h, :width, :] - m_new[0:1, :]).astype(BF16)
            ap[h] = (alpha, p)

        def accumulate(h):
            alpha, p = ap.pop(h)
            pair, half = divmod(h, HEADS_PER_BLOCK)
            vt = vt_ref[0, pair, half * HEAD_DIM:(half + 1) * HEAD_DIM, pl.ds(start, width)]
            first = lax.broadcasted_iota(jnp.int32, (PV_ROWS - HEAD_DIM, width), 0) == 0
            ones = jnp.where(first, 1.0, 0.0).astype(BF16)
            pv = _dot(jnp.concatenate([vt, ones], axis=0), p)
            acc_ref[h] = acc_ref[h] * alpha[0:1, :] + pv

        def step(t):
            if t < N_HEADS:
                probs(t)
            if t >= 1:
                accumulate(t - 1)

        return [lambda t=t: step(t) for t in range(N_HEADS + 1)]

    def run(*stages):
        for i in range(max(len(st) for st in stages)):
            for st in stages:
                if i < len(st):
                    st[i]()

    main = lambda k: pl.multiple_of(k * tk, tk)
    leads = range(0, tk, tq)
    diag_scores = lambda ld, s_ref, mx_ref: scores(diag_start, ld + tq, s_ref, mx_ref, True)
    diag_softmax = lambda ld, s_ref, mx_ref: softmax_pv(diag_start, ld + tq, s_ref, mx_ref)

    @pl.when(n_main > 0)
    def _():
        run(scores(main(0), tk, s0_ref, mx0_ref))

    n_pairs = jnp.maximum(n_main - 1, 0) // 2

    def pair_of_blocks(i, carry):
        k = 2 * i
        run(scores(main(k + 1), tk, s1_ref, mx1_ref), softmax_pv(main(k), tk, s0_ref, mx0_ref))
        run(scores(main(k + 2), tk, s0_ref, mx0_ref), softmax_pv(main(k + 1), tk, s1_ref, mx1_ref))
        return carry

    lax.fori_loop(0, n_pairs, pair_of_blocks, 0)
    k_last = 2 * n_pairs
    left = n_main - k_last

    for ld in leads:
        @pl.when((left == 0) & (lead == ld))
        def _():
            run(diag_scores(ld, s0_ref, mx0_ref))
            run(diag_softmax(ld, s0_ref, mx0_ref))

        @pl.when((left == 1) & (lead == ld))
        def _():
            run(diag_scores(ld, s1_ref, mx1_ref), softmax_pv(main(k_last), tk, s0_ref, mx0_ref))
            run(diag_softmax(ld, s1_ref, mx1_ref))

        @pl.when((left == 2) & (lead == ld))
        def _():
            run(scores(main(k_last + 1), tk, s1_ref, mx1_ref),
                softmax_pv(main(k_last), tk, s0_ref, mx0_ref))
            run(diag_scores(ld, s0_ref, mx0_ref), softmax_pv(main(k_last + 1), tk, s1_ref, mx1_ref))
            run(diag_softmax(ld, s0_ref, mx0_ref))

    gated = []
    for pair in range(N_HEADS // HEADS_PER_BLOCK):
        outs = []
        for half in range(HEADS_PER_BLOCK):
            acc = acc_ref[pair * HEADS_PER_BLOCK + half]
            outs.append(acc[:HEAD_DIM, :] / acc[HEAD_DIM:HEAD_DIM + 1, :])
        att = jnp.concatenate(outs, axis=0).T
        sl = slice(pair * LANES, (pair + 1) * LANES)
        gated.append((att * ga_ref[0, :, sl].astype(F32)).astype(BF16))
    ya = jnp.concatenate(gated, axis=1)
    y = _dot(jnp.concatenate([ya, yp_ref[0]], axis=1), wob_ref[...]) + bo_ref[...]
    gate = ada_ref[2, pl.ds(bi, 1), :]
    hres = DEEPNORM_ALPHA * x_ref[0] + gate * y
    mu = jnp.mean(hres, axis=-1, keepdims=True)
    d = hres - mu
    var = jnp.mean(d * d, axis=-1, keepdims=True)
    o_ref[0] = d * lax.rsqrt(var + LN_EPS) * lg_ref[...] + lb_ref[...]


def _attn_call(qt, k, vt, g, ga, yp, x, ada, w_out, b_out, ln_g, ln_b):
    B, n_pairs, _, S = qt.shape
    D = x.shape[2]
    assert KV_TILE % Q_TILE == 0 and S % KV_TILE == 0
    tile = lambda width: pl.BlockSpec((1, Q_TILE, width), lambda b, i: (b, i, 0))
    whole = lambda width: pl.BlockSpec((1, S, width), lambda b, i: (b, 0, 0))
    const2 = lambda a: pl.BlockSpec(a.shape, lambda b, i: (0, 0))
    score_buf = pltpu.VMEM((N_HEADS, KV_TILE, Q_TILE), F32)
    stat_buf = pltpu.VMEM((N_HEADS, 8, Q_TILE), F32)
    return pl.pallas_call(
        _attn_kernel,
        name="attention",
        grid=(B, S // Q_TILE),
        in_specs=[pl.BlockSpec((1, n_pairs, LANES, Q_TILE), lambda b, i: (b, 0, 0, i)),
                  whole(D_ATT),
                  pl.BlockSpec((1, n_pairs, LANES, S), lambda b, i: (b, 0, 0, 0)),
                  whole(LANES), tile(D_ATT), tile(D_POOL), tile(D),
                  pl.BlockSpec(ada.shape, lambda b, i: (0, 0, 0)),
                  _resident(w_out.shape, lambda b, i: (0, 0)),
                  const2(b_out), const2(ln_g), const2(ln_b)],
        out_specs=tile(D),
        out_shape=jax.ShapeDtypeStruct((B, S, D), F32),
        scratch_shapes=[stat_buf, pltpu.VMEM((N_HEADS, PV_ROWS, Q_TILE), F32),
                        score_buf, score_buf, stat_buf, stat_buf,
                        pltpu.VMEM((N_HEADS, 2 * LANES, Q_TILE), BF16),
                        pltpu.VMEM(w_out.shape, BF16)],
        compiler_params=pltpu.CompilerParams(
            dimension_semantics=("arbitrary", "arbitrary"),
            vmem_limit_bytes=VMEM_LIMIT_BYTES),
    )(qt, k, vt, g, ga, yp, x, ada, w_out, b_out, ln_g, ln_b)


def _layer(x, c, w_ada, b_ada, w_in, b_in, w_pool_mix, b_pool_mix, pool_scale,
           w_out, b_out, ln_g, ln_b):
    ada = _ada_call(c, w_ada, b_ada[None, :])
    qt, k, vt, g, ga, yp = _in_call(x, ada, w_in.T, b_in[None, :], w_pool_mix, b_pool_mix,
                                    pool_scale[None, :])
    return _attn_call(qt, k, vt, g, ga, yp, x, ada, w_out, b_out[None, :], ln_g[None, :],
                      ln_b[None, :])


def kernel(x, c, w_ada, b_ada, w_in, b_in, w_pool_mix, b_pool_mix, pool_scale, w_out, b_out,
           ln_g, ln_b):
    for layer in range(w_ada.shape[0]):
        x = _layer(x, c, w_ada[layer], b_ada[layer], w_in[layer], b_in[layer],
                   w_pool_mix[layer], b_pool_mix[layer], pool_scale[layer],
                   w_out[layer], b_out[layer], ln_g[layer], ln_b[layer])
    return x
```

```python
import math

import jax
import jax.numpy as jnp
from jax import lax
from jax.experimental import pallas as pl
from jax.experimental.pallas import tpu as pltpu

D_MODEL = 1024
D_ATT = 512
D_POOL = 512
N_HEADS = 8
HEAD_DIM = 64
POOL_WINDOWS = (2, 4, 8, 16)
POOL_GROUP_DIM = 128
POOL_HISTORY = 16
LN_EPS = 1e-5
DEEPNORM_ALPHA = 2.0 ** 0.25
LOG2E = math.log2(math.e)
NEG_BIG = -1e30

LANES = 128
SUBLANES = 8
BF16_SUBLANES = 16
HEADS_PER_BLOCK = LANES // HEAD_DIM
G_TERMS = 3
CUM_ROWS = 128
PV_ROWS = HEAD_DIM + BF16_SUBLANES

IN_SOURCE = (("q", D_ATT), ("k", D_ATT), ("v", D_ATT), ("f", N_HEADS), ("p", D_POOL),
             ("ga", D_ATT), ("gp", D_POOL))
IN_WIDTH = {"f": LANES, "p": D_POOL, "gp": D_POOL, "q": D_ATT, "v": D_ATT, "k": D_ATT,
            "ga": D_ATT}
IN_OFFSET = dict(zip(IN_WIDTH, (sum(list(IN_WIDTH.values())[:i]) for i in range(len(IN_WIDTH)))))
IN_COLS = sum(IN_WIDTH.values())

SEQ_TILE = 512
Q_TILE = 256
KV_TILE = 512
VMEM_LIMIT_BYTES = 56 * 1024 * 1024

F32 = jnp.float32
BF16 = jnp.bfloat16


def _silu(x):
    return x * jax.nn.sigmoid(x)


def _dot(a, b):
    return jnp.dot(a, b, preferred_element_type=F32)


def _resident(shape, index_map):
    return pl.BlockSpec(shape, index_map, pipeline_mode=pl.Buffered(1))


def _ada_kernel(c_ref, w_ref, b_ref, o_ref, cpad_ref):
    n_rows = c_ref.shape[0]
    cpad_ref[...] = jnp.zeros_like(cpad_ref)
    cpad_ref[0:n_rows, :] = c_ref[...]
    sc = _silu(cpad_ref[...].T)
    for b in range(n_rows):
        o_ref[0, b:b + 1, :] = (jnp.sum(w_ref[...] * sc[:, b:b + 1], axis=0, keepdims=True)
                                + b_ref[...])


def _ada_call(c, w_ada, b_ada):
    n_rows, d = c.shape
    n_chunks = w_ada.shape[1] // d
    return pl.pallas_call(
        _ada_kernel,
        name="ada_vector",
        grid=(n_chunks,),
        in_specs=[
            pl.BlockSpec((n_rows, d), lambda j: (0, 0)),
            pl.BlockSpec((d, d), lambda j: (0, j)),
            pl.BlockSpec((1, d), lambda j: (0, j)),
        ],
        out_specs=pl.BlockSpec((1, n_rows, d), lambda j: (j, 0, 0)),
        out_shape=jax.ShapeDtypeStruct((n_chunks, n_rows, d), F32),
        scratch_shapes=[pltpu.VMEM((LANES, d), F32)],
        compiler_params=pltpu.CompilerParams(dimension_semantics=("parallel",),
                                             vmem_limit_bytes=VMEM_LIMIT_BYTES),
    )(c, w_ada, b_ada)


def _split3(x):
    hi = x.astype(BF16).astype(F32)
    r = x - hi
    mid = r.astype(BF16).astype(F32)
    lo = (r - mid).astype(BF16).astype(F32)
    return hi, mid, lo


def _in_kernel(x_ref, ada_ref, wt_ref, b_ref, wpm_ref, bpm_ref, ps_ref,
               q_ref, k_ref, v_ref, g_ref, ga_ref, yp_ref,
               fcarry_ref, pcarry_ref, wcat_ref, bcat_ref, wmix_ref):
    bi = pl.program_id(0)
    si = pl.program_id(1)
    ts = x_ref.shape[1]

    @pl.when((bi == 0) & (si == 0))
    def _():
        src = 0
        for name, width in IN_SOURCE:
            dst, wide = IN_OFFSET[name], IN_WIDTH[name]
            w_seg = wt_ref[src:src + wide, :].T
            b_seg = b_ref[:, src:src + wide]
            if width < wide:
                w_seg = jnp.where(lax.broadcasted_iota(jnp.int32, w_seg.shape, 1) < width,
                                  w_seg, 0.0)
                b_seg = jnp.where(lax.broadcasted_iota(jnp.int32, b_seg.shape, 1) < width,
                                  b_seg, 0.0)
            wcat_ref[:, dst:dst + wide] = w_seg.astype(BF16)
            bcat_ref[:, dst:dst + wide] = b_seg
            src += width
        gd = POOL_GROUP_DIM
        wmix_ref[...] = jnp.zeros_like(wmix_ref)
        for g in range(len(POOL_WINDOWS)):
            o = (g % 2) * gd
            wmix_ref[g // 2, o:o + gd, o:o + gd] = wpm_ref[g].astype(BF16)

    @pl.when(si == 0)
    def _():
        fcarry_ref[...] = jnp.zeros_like(fcarry_ref)
        pcarry_ref[...] = jnp.zeros_like(pcarry_ref)

    shift = ada_ref[0, pl.ds(bi, 1), :]
    scale = ada_ref[1, pl.ds(bi, 1), :]
    u = (x_ref[0] * (1.0 + scale) + shift).astype(BF16)
    proj = _dot(u, wcat_ref[...]) + bcat_ref[...]
    seg = lambda name: proj[:, IN_OFFSET[name]:IN_OFFSET[name] + IN_WIDTH[name]]

    q = seg("q") * (HEAD_DIM ** -0.5 * LOG2E)
    for blk in range(D_ATT // LANES):
        q_ref[0, blk] = q[:, blk * LANES:(blk + 1) * LANES].T.astype(BF16)

    k_ref[0] = seg("k").astype(BF16)

    v = seg("v")
    for blk in range(D_ATT // LANES):
        v_ref[0, blk] = v[:, blk * LANES:(blk + 1) * LANES].T.astype(BF16)

    fl = seg("f")
    logf = jnp.minimum(fl, 0.0) - jnp.log1p(jnp.exp(-jnp.abs(fl)))
    row = lax.broadcasted_iota(jnp.int32, (CUM_ROWS, CUM_ROWS), 0)
    col = lax.broadcasted_iota(jnp.int32, (CUM_ROWS, CUM_ROWS), 1)
    tri = jnp.where(row >= col, 1.0, 0.0).astype(BF16)
    terms = jnp.concatenate([t.astype(BF16) for t in _split3(logf)], axis=1)
    offset = fcarry_ref[0:1, :]
    cum_blocks = []
    for r in range(ts // CUM_ROWS):
        part = _dot(tri, terms[r * CUM_ROWS:(r + 1) * CUM_ROWS, :])
        local = part[:, :LANES] + part[:, LANES:2 * LANES] + part[:, 2 * LANES:]
        cum_blocks.append(local + offset)
        offset = cum_blocks[-1][CUM_ROWS - 1:CUM_ROWS, :]
    fcarry_ref[...] = jnp.broadcast_to(offset, fcarry_ref.shape)
    cum = jnp.concatenate(cum_blocks, axis=0)

    g_hi, g_mid, g_lo = _split3(cum * (-LOG2E))
    lane = lax.broadcasted_iota(jnp.int32, (ts, LANES), 1)
    g = jnp.where(lane < N_HEADS, g_hi,
                  jnp.where(lane < 2 * N_HEADS, pltpu.roll(g_mid, N_HEADS, axis=1),
                            jnp.where(lane < 3 * N_HEADS, pltpu.roll(g_lo, 2 * N_HEADS, axis=1),
                                      0.0)))
    g_ref[0] = g.astype(BF16)

    p = seg("p")
    pe = jnp.concatenate([pcarry_ref[...], p], axis=0)
    pcarry_ref[...] = p[ts - POOL_HISTORY:, :]
    gp = seg("gp")
    t_glob = si * ts + lax.broadcasted_iota(jnp.int32, (ts, POOL_GROUP_DIM), 0)
    pooled = []
    for g, w in enumerate(POOL_WINDOWS):
        sl = slice(g * POOL_GROUP_DIM, (g + 1) * POOL_GROUP_DIM)
        y = pe[:, sl]
        sh = 1
        while sh < w:
            y = y + pltpu.roll(y, sh, axis=0)
            sh *= 2
        cnt = jnp.minimum(t_glob + 1, w).astype(F32)
        pooled.append((y[POOL_HISTORY:, :] / cnt - p[:, sl]).astype(BF16))
    for j in range(len(POOL_WINDOWS) // 2):
        sl = slice(2 * j * POOL_GROUP_DIM, (2 * j + 2) * POOL_GROUP_DIM)
        bias = jnp.concatenate([bpm_ref[2 * j:2 * j + 1, :], bpm_ref[2 * j + 1:2 * j + 2, :]],
                               axis=1)
        mixed = _dot(jnp.concatenate(pooled[2 * j:2 * j + 2], axis=1), wmix_ref[j]) + bias
        yp_ref[0, :, sl] = (mixed * ps_ref[:, sl] * _silu(gp[:, sl])).astype(BF16)

    ga_ref[0] = _silu(seg("ga")).astype(BF16)


def _in_call(x, ada, w_in_t, b_in, w_pm, b_pm, pool_scale):
    B, S, D = x.shape
    ts = SEQ_TILE
    const2 = lambda b, s: (0, 0)
    const3 = lambda b, s: (0, 0, 0)
    tile = lambda width: pl.BlockSpec((1, ts, width), lambda b, s: (b, s, 0))
    n_pairs = D_ATT // LANES
    tile_t = pl.BlockSpec((1, n_pairs, LANES, ts), lambda b, s: (b, 0, 0, s))
    return pl.pallas_call(
        _in_kernel,
        name="input_stage",
        grid=(B, S // ts),
        in_specs=[tile(D), pl.BlockSpec(ada.shape, const3),
                  _resident(w_in_t.shape, const2), pl.BlockSpec(b_in.shape, const2),
                  pl.BlockSpec(w_pm.shape, const3), pl.BlockSpec(b_pm.shape, const2),
                  pl.BlockSpec(pool_scale.shape, const2)],
        out_specs=[tile_t, tile(D_ATT), tile_t, tile(LANES),
                   tile(D_ATT), tile(D_POOL)],
        out_shape=[jax.ShapeDtypeStruct((B, n_pairs, LANES, S), BF16),
                   jax.ShapeDtypeStruct((B, S, D_ATT), BF16),
                   jax.ShapeDtypeStruct((B, n_pairs, LANES, S), BF16),
                   jax.ShapeDtypeStruct((B, S, LANES), BF16),
                   jax.ShapeDtypeStruct((B, S, D_ATT), BF16),
                   jax.ShapeDtypeStruct((B, S, D_POOL), BF16)],
        scratch_shapes=[pltpu.VMEM((SUBLANES, LANES), F32),
                        pltpu.VMEM((POOL_HISTORY, D_POOL), F32),
                        pltpu.VMEM((D, IN_COLS), BF16),
                        pltpu.VMEM((1, IN_COLS), F32),
                        pltpu.VMEM((len(POOL_WINDOWS) // 2, 2 * POOL_GROUP_DIM,
                                    2 * POOL_GROUP_DIM), BF16)],
        compiler_params=pltpu.CompilerParams(
            dimension_semantics=("arbitrary", "arbitrary"),
            vmem_limit_bytes=VMEM_LIMIT_BYTES),
    )(x, ada, w_in_t, b_in, w_pm, b_pm, pool_scale)


def _attn_kernel(qt_ref, k_ref, vt_ref, g_ref, ga_ref, yp_ref, x_ref, ada_ref,
                 wo_ref, bo_ref, lg_ref, lb_ref, o_ref,
                 m_ref, acc_ref, s0_ref, s1_ref, mx0_ref, mx1_ref, rhs_ref, wob_ref):
    bi = pl.program_id(0)
    n_main = pl.program_id(1)
    tq = Q_TILE
    tk = KV_TILE
    diag_start = pl.multiple_of(n_main * tk, tk)

    @pl.when((bi == 0) & (n_main == 0))
    def _():
        wob_ref[...] = wo_ref[...].astype(BF16)

    def begin_tile(pos):
        m_ref[...] = jnp.full_like(m_ref, NEG_BIG)
        acc_ref[...] = jnp.zeros_like(acc_ref)
        row = lax.broadcasted_iota(jnp.int32, (LANES, tq), 0)
        for h in range(N_HEADS):
            pair, half = divmod(h, HEADS_PER_BLOCK)
            own = (row // HEAD_DIM) == half
            qt = qt_ref[0, pair, :, pos * tq:(pos + 1) * tq]
            rhs_ref[h, :LANES, :] = jnp.where(own, qt, jnp.zeros((), BF16))
            pick = (row < G_TERMS * N_HEADS) & (row % N_HEADS == h)
            rhs_ref[h, LANES:, :] = jnp.where(pick, 1.0, 0.0).astype(BF16)

    def scores(start, width, s_ref, mx_ref, diagonal=False):
        def head(h):
            pair = h // HEADS_PER_BLOCK
            kblk = k_ref[0, pl.ds(start, width), pair * LANES:(pair + 1) * LANES]
            gblk = g_ref[0, pl.ds(start, width), :]
            lhs = jnp.concatenate([kblk, gblk], axis=1)
            s = _dot(lhs, rhs_ref[h])
            if diagonal:
                key = lax.broadcasted_iota(jnp.int32, (tq, tq), 0)
                qry = lax.broadcasted_iota(jnp.int32, (tq, tq), 1)
                own = jnp.where(key <= qry, s[width - tq:, :], NEG_BIG)
                s = own if width == tq else jnp.concatenate([s[:width - tq, :], own], axis=0)
            s_ref[h, :width, :] = s
            mx_ref[h] = jnp.max(s.reshape(width // SUBLANES, SUBLANES, tq), axis=0)

        return [lambda h=h: head(h) for h in range(N_HEADS)]

    def softmax_pv(start, width, s_ref, mx_ref):
        ap = {}

        def probs(h):
            m_prev = m_ref[h]
            m_new = jnp.maximum(m_prev, jnp.max(mx_ref[h], axis=0, keepdims=True))
            m_ref[h] = m_new
            alpha = jnp.exp2(m_prev - m_new)
            p = jnp.exp2(s_ref[h, :width, :] - m_new[0:1, :]).astype(BF16)
            ap[h] = (alpha, p)

        def accumulate(h):
            alpha, p = ap.pop(h)
            pair, half = divmod(h, HEADS_PER_BLOCK)
            vt = vt_ref[0, pair, half * HEAD_DIM:(half + 1) * HEAD_DIM, pl.ds(start, width)]
            first = lax.broadcasted_iota(jnp.int32, (PV_ROWS - HEAD_DIM, width), 0) == 0
            ones = jnp.where(first, 1.0, 0.0).astype(BF16)
            pv = _dot(jnp.concatenate([vt, ones], axis=0), p)
            acc_ref[h] = acc_ref[h] * alpha[0:1, :] + pv

        def step(t):
            if t < N_HEADS:
                probs(t)
            if t >= 1:
                accumulate(t - 1)

        return [lambda t=t: step(t) for t in range(N_HEADS + 1)]

    def run(*stages):
        for i in range(max(len(st) for st in stages)):
            for st in stages:
                if i < len(st):
                    st[i]()

    main = lambda k: pl.multiple_of(k * tk, tk)

    def first_scores():
        run(scores(main(0), tk, s0_ref, mx0_ref))

    def attend(pos):
        width = (pos + 1) * tq
        diag_scores = lambda s_ref, mx_ref: scores(diag_start, width, s_ref, mx_ref, True)
        diag_softmax = lambda s_ref, mx_ref: softmax_pv(diag_start, width, s_ref, mx_ref)
        n_pairs = jnp.maximum(n_main - 1, 0) // 2

        def pair_of_blocks(i, carry):
            k = 2 * i
            run(scores(main(k + 1), tk, s1_ref, mx1_ref),
                softmax_pv(main(k), tk, s0_ref, mx0_ref))
            run(scores(main(k + 2), tk, s0_ref, mx0_ref),
                softmax_pv(main(k + 1), tk, s1_ref, mx1_ref))
            return carry

        lax.fori_loop(0, n_pairs, pair_of_blocks, 0)
        k_last = 2 * n_pairs
        left = n_main - k_last

        @pl.when(left == 0)
        def _():
            run(diag_scores(s0_ref, mx0_ref))
            run(diag_softmax(s0_ref, mx0_ref))

        @pl.when(left == 1)
        def _():
            run(diag_scores(s1_ref, mx1_ref), softmax_pv(main(k_last), tk, s0_ref, mx0_ref))
            run(diag_softmax(s1_ref, mx1_ref))

        @pl.when(left == 2)
        def _():
            run(scores(main(k_last + 1), tk, s1_ref, mx1_ref),
                softmax_pv(main(k_last), tk, s0_ref, mx0_ref))
            run(diag_scores(s0_ref, mx0_ref), softmax_pv(main(k_last + 1), tk, s1_ref, mx1_ref))
            run(diag_softmax(s0_ref, mx0_ref))

    def gated_heads(pos):
        rows = slice(pos * tq, (pos + 1) * tq)
        gated = []
        for pair in range(N_HEADS // HEADS_PER_BLOCK):
            outs = []
            for half in range(HEADS_PER_BLOCK):
                acc = acc_ref[pair * HEADS_PER_BLOCK + half]
                outs.append(acc[:HEAD_DIM, :] / acc[HEAD_DIM:HEAD_DIM + 1, :])
            att = jnp.concatenate(outs, axis=0).T
            sl = slice(pair * LANES, (pair + 1) * LANES)
            gated.append((att * ga_ref[0, rows, sl].astype(F32)).astype(BF16))
        return jnp.concatenate(gated, axis=1)

    def project(pos, ya):
        rows = slice(pos * tq, (pos + 1) * tq)
        return _dot(jnp.concatenate([ya, yp_ref[0, rows, :]], axis=1), wob_ref[...]) + bo_ref[...]

    def residual_norm(pos, y):
        rows = slice(pos * tq, (pos + 1) * tq)
        hres = DEEPNORM_ALPHA * x_ref[0, rows, :] + ada_ref[2, pl.ds(bi, 1), :] * y
        mu = jnp.mean(hres, axis=-1, keepdims=True)
        d = hres - mu
        var = jnp.mean(d * d, axis=-1, keepdims=True)
        o_ref[0, rows, :] = d * lax.rsqrt(var + LN_EPS) * lg_ref[...] + lb_ref[...]

    n_pos = tk // tq
    begin_tile(0)

    @pl.when(n_main > 0)
    def _():
        first_scores()

    for pos in range(n_pos):
        attend(pos)
        ya = gated_heads(pos)
        if pos + 1 < n_pos:
            @pl.when(n_main > 0)
            def _():
                y = project(pos, ya)
                begin_tile(pos + 1)
                first_scores()
                residual_norm(pos, y)

            @pl.when(n_main == 0)
            def _():
                residual_norm(pos, project(pos, ya))
                begin_tile(pos + 1)
        else:
            residual_norm(pos, project(pos, ya))


def _attn_call(qt, k, vt, g, ga, yp, x, ada, w_out, b_out, ln_g, ln_b):
    B, n_pairs, _, S = qt.shape
    D = x.shape[2]
    assert KV_TILE % Q_TILE == 0 and S % KV_TILE == 0
    rows = KV_TILE
    tile = lambda width: pl.BlockSpec((1, rows, width), lambda b, i: (b, i, 0))
    whole = lambda width: pl.BlockSpec((1, S, width), lambda b, i: (b, 0, 0))
    const2 = lambda a: pl.BlockSpec(a.shape, lambda b, i: (0, 0))
    score_buf = pltpu.VMEM((N_HEADS, KV_TILE, Q_TILE), F32)
    stat_buf = pltpu.VMEM((N_HEADS, SUBLANES, Q_TILE), F32)
    return pl.pallas_call(
        _attn_kernel,
        name="attention",
        grid=(B, S // rows),
        in_specs=[pl.BlockSpec((1, n_pairs, LANES, rows), lambda b, i: (b, 0, 0, i)),
                  whole(D_ATT),
                  pl.BlockSpec((1, n_pairs, LANES, S), lambda b, i: (b, 0, 0, 0)),
                  whole(LANES), tile(D_ATT), tile(D_POOL), tile(D),
                  pl.BlockSpec(ada.shape, lambda b, i: (0, 0, 0)),
                  _resident(w_out.shape, lambda b, i: (0, 0)),
                  const2(b_out), const2(ln_g), const2(ln_b)],
        out_specs=tile(D),
        out_shape=jax.ShapeDtypeStruct((B, S, D), F32),
        scratch_shapes=[stat_buf, pltpu.VMEM((N_HEADS, PV_ROWS, Q_TILE), F32),
                        score_buf, score_buf, stat_buf, stat_buf,
                        pltpu.VMEM((N_HEADS, 2 * LANES, Q_TILE), BF16),
                        pltpu.VMEM(w_out.shape, BF16)],
        compiler_params=pltpu.CompilerParams(
            dimension_semantics=("arbitrary", "arbitrary"),
            vmem_limit_bytes=VMEM_LIMIT_BYTES),
    )(qt, k, vt, g, ga, yp, x, ada, w_out, b_out, ln_g, ln_b)


def _layer(x, c, w_ada, b_ada, w_in, b_in, w_pool_mix, b_pool_mix, pool_scale,
           w_out, b_out, ln_g, ln_b):
    ada = _ada_call(c, w_ada, b_ada[None, :])
    qt, k, vt, g, ga, yp = _in_call(x, ada, w_in.T, b_in[None, :], w_pool_mix, b_pool_mix,
                                    pool_scale[None, :])
    return _attn_call(qt, k, vt, g, ga, yp, x, ada, w_out, b_out[None, :], ln_g[None, :],
                      ln_b[None, :])


def kernel(x, c, w_ada, b_ada, w_in, b_in, w_pool_mix, b_pool_mix, pool_scale, w_out, b_out,
           ln_g, ln_b):
    for layer in range(w_ada.shape[0]):
        x = _layer(x, c, w_ada[layer], b_ada[layer], w_in[layer], b_in[layer],
                   w_pool_mix[layer], b_pool_mix[layer], pool_scale[layer],
                   w_out[layer], b_out[layer], ln_g[layer], ln_b[layer])
    return x
```

```python
import math

import jax
import jax.numpy as jnp
from jax import lax
from jax.experimental import pallas as pl
from jax.experimental.pallas import tpu as pltpu

D_MODEL = 1024
D_ATT = 512
D_POOL = 512
N_HEADS = 8
HEAD_DIM = 64
POOL_WINDOWS = (2, 4, 8, 16)
POOL_GROUP_DIM = 128
POOL_HISTORY = 16
LN_EPS = 1e-5
DEEPNORM_ALPHA = 2.0 ** 0.25
LOG2E = math.log2(math.e)
NEG_BIG = -1e30

LANES = 128
SUBLANES = 8
BF16_SUBLANES = 16
HEADS_PER_BLOCK = LANES // HEAD_DIM
G_TERMS = 3
CUM_ROWS = 128
PV_ROWS = HEAD_DIM + BF16_SUBLANES

IN_SOURCE = (("q", D_ATT), ("k", D_ATT), ("v", D_ATT), ("f", N_HEADS), ("p", D_POOL),
             ("ga", D_ATT), ("gp", D_POOL))
IN_WIDTH = {"f": LANES, "p": D_POOL, "gp": D_POOL, "q": D_ATT, "v": D_ATT, "k": D_ATT,
            "ga": D_ATT}
IN_OFFSET = dict(zip(IN_WIDTH, (sum(list(IN_WIDTH.values())[:i]) for i in range(len(IN_WIDTH)))))
IN_COLS = sum(IN_WIDTH.values())

SEQ_TILE = 512
Q_TILE = 256
KV_TILE = 512
VMEM_LIMIT_BYTES = 56 * 1024 * 1024

F32 = jnp.float32
BF16 = jnp.bfloat16


def _silu(x):
    return x * jax.nn.sigmoid(x)


def _dot(a, b):
    return jnp.dot(a, b, preferred_element_type=F32)


def _resident(shape, index_map):
    return pl.BlockSpec(shape, index_map, pipeline_mode=pl.Buffered(1))


def _ada_kernel(c_ref, w_ref, b_ref, o_ref, cpad_ref):
    n_rows = c_ref.shape[0]
    cpad_ref[...] = jnp.zeros_like(cpad_ref)
    cpad_ref[0:n_rows, :] = c_ref[...]
    sc = _silu(cpad_ref[...].T)
    for b in range(n_rows):
        o_ref[0, b:b + 1, :] = (jnp.sum(w_ref[...] * sc[:, b:b + 1], axis=0, keepdims=True)
                                + b_ref[...])


def _ada_call(c, w_ada, b_ada):
    n_rows, d = c.shape
    n_chunks = w_ada.shape[1] // d
    return pl.pallas_call(
        _ada_kernel,
        name="ada_vector",
        grid=(n_chunks,),
        in_specs=[
            pl.BlockSpec((n_rows, d), lambda j: (0, 0)),
            pl.BlockSpec((d, d), lambda j: (0, j)),
            pl.BlockSpec((1, d), lambda j: (0, j)),
        ],
        out_specs=pl.BlockSpec((1, n_rows, d), lambda j: (j, 0, 0)),
        out_shape=jax.ShapeDtypeStruct((n_chunks, n_rows, d), F32),
        scratch_shapes=[pltpu.VMEM((LANES, d), F32)],
        compiler_params=pltpu.CompilerParams(dimension_semantics=("parallel",),
                                             vmem_limit_bytes=VMEM_LIMIT_BYTES),
    )(c, w_ada, b_ada)


def _split3(x):
    hi = x.astype(BF16).astype(F32)
    r = x - hi
    mid = r.astype(BF16).astype(F32)
    lo = (r - mid).astype(BF16).astype(F32)
    return hi, mid, lo


def _in_kernel(x_ref, ada_ref, wt_ref, b_ref, wpm_ref, bpm_ref, ps_ref,
               q_ref, k_ref, v_ref, g_ref, ga_ref, yp_ref,
               fcarry_ref, pcarry_ref, wcat_ref, bcat_ref, wmix_ref):
    bi = pl.program_id(0)
    si = pl.program_id(1)
    ts = x_ref.shape[1]

    @pl.when((bi == 0) & (si == 0))
    def _():
        src = 0
        for name, width in IN_SOURCE:
            dst, wide = IN_OFFSET[name], IN_WIDTH[name]
            w_seg = wt_ref[src:src + wide, :].T
            b_seg = b_ref[:, src:src + wide]
            if width < wide:
                w_seg = jnp.where(lax.broadcasted_iota(jnp.int32, w_seg.shape, 1) < width,
                                  w_seg, 0.0)
                b_seg = jnp.where(lax.broadcasted_iota(jnp.int32, b_seg.shape, 1) < width,
                                  b_seg, 0.0)
            wcat_ref[:, dst:dst + wide] = w_seg.astype(BF16)
            bcat_ref[:, dst:dst + wide] = b_seg
            src += width
        gd = POOL_GROUP_DIM
        wmix_ref[...] = jnp.zeros_like(wmix_ref)
        for g in range(len(POOL_WINDOWS)):
            o = (g % 2) * gd
            wmix_ref[g // 2, o:o + gd, o:o + gd] = wpm_ref[g].astype(BF16)

    @pl.when(si == 0)
    def _():
        fcarry_ref[...] = jnp.zeros_like(fcarry_ref)
        pcarry_ref[...] = jnp.zeros_like(pcarry_ref)

    shift = ada_ref[0, pl.ds(bi, 1), :]
    scale = ada_ref[1, pl.ds(bi, 1), :]
    u = (x_ref[0] * (1.0 + scale) + shift).astype(BF16)
    proj = _dot(u, wcat_ref[...]) + bcat_ref[...]
    seg = lambda name: proj[:, IN_OFFSET[name]:IN_OFFSET[name] + IN_WIDTH[name]]

    q = seg("q") * (HEAD_DIM ** -0.5 * LOG2E)
    for blk in range(D_ATT // LANES):
        q_ref[0, blk] = q[:, blk * LANES:(blk + 1) * LANES].T.astype(BF16)

    k_ref[0] = seg("k").astype(BF16)

    v = seg("v")
    for blk in range(D_ATT // LANES):
        v_ref[0, blk] = v[:, blk * LANES:(blk + 1) * LANES].T.astype(BF16)

    fl = seg("f")
    logf = jnp.minimum(fl, 0.0) - jnp.log1p(jnp.exp(-jnp.abs(fl)))
    row = lax.broadcasted_iota(jnp.int32, (CUM_ROWS, CUM_ROWS), 0)
    col = lax.broadcasted_iota(jnp.int32, (CUM_ROWS, CUM_ROWS), 1)
    tri = jnp.where(row >= col, 1.0, 0.0).astype(BF16)
    terms = jnp.concatenate([t.astype(BF16) for t in _split3(logf)], axis=1)
    offset = fcarry_ref[0:1, :]
    cum_blocks = []
    for r in range(ts // CUM_ROWS):
        part = _dot(tri, terms[r * CUM_ROWS:(r + 1) * CUM_ROWS, :])
        local = part[:, :LANES] + part[:, LANES:2 * LANES] + part[:, 2 * LANES:]
        cum_blocks.append(local + offset)
        offset = cum_blocks[-1][CUM_ROWS - 1:CUM_ROWS, :]
    fcarry_ref[...] = jnp.broadcast_to(offset, fcarry_ref.shape)
    cum = jnp.concatenate(cum_blocks, axis=0)

    g_hi, g_mid, g_lo = _split3(cum * (-LOG2E))
    lane = lax.broadcasted_iota(jnp.int32, (ts, LANES), 1)
    g = jnp.where(lane < N_HEADS, g_hi,
                  jnp.where(lane < 2 * N_HEADS, pltpu.roll(g_mid, N_HEADS, axis=1),
                            jnp.where(lane < 3 * N_HEADS, pltpu.roll(g_lo, 2 * N_HEADS, axis=1),
                                      0.0)))
    g_ref[0] = g.astype(BF16)

    p = seg("p")
    pe = jnp.concatenate([pcarry_ref[...], p], axis=0)
    pcarry_ref[...] = p[ts - POOL_HISTORY:, :]
    gp = seg("gp")
    t_glob = si * ts + lax.broadcasted_iota(jnp.int32, (ts, POOL_GROUP_DIM), 0)
    pooled = []
    for g, w in enumerate(POOL_WINDOWS):
        sl = slice(g * POOL_GROUP_DIM, (g + 1) * POOL_GROUP_DIM)
        y = pe[:, sl]
        sh = 1
        while sh < w:
            y = y + pltpu.roll(y, sh, axis=0)
            sh *= 2
        cnt = jnp.minimum(t_glob + 1, w).astype(F32)
        pooled.append((y[POOL_HISTORY:, :] / cnt - p[:, sl]).astype(BF16))
    for j in range(len(POOL_WINDOWS) // 2):
        sl = slice(2 * j * POOL_GROUP_DIM, (2 * j + 2) * POOL_GROUP_DIM)
        bias = jnp.concatenate([bpm_ref[2 * j:2 * j + 1, :], bpm_ref[2 * j + 1:2 * j + 2, :]],
                               axis=1)
        mixed = _dot(jnp.concatenate(pooled[2 * j:2 * j + 2], axis=1), wmix_ref[j]) + bias
        yp_ref[0, :, sl] = (mixed * ps_ref[:, sl] * _silu(gp[:, sl])).astype(BF16)

    ga_ref[0] = _silu(seg("ga")).astype(BF16)


def _in_call(x, ada, w_in_t, b_in, w_pm, b_pm, pool_scale):
    B, S, D = x.shape
    ts = SEQ_TILE
    const2 = lambda b, s: (0, 0)
    const3 = lambda b, s: (0, 0, 0)
    tile = lambda width: pl.BlockSpec((1, ts, width), lambda b, s: (b, s, 0))
    n_pairs = D_ATT // LANES
    tile_t = pl.BlockSpec((1, n_pairs, LANES, ts), lambda b, s: (b, 0, 0, s))
    return pl.pallas_call(
        _in_kernel,
        name="input_stage",
        grid=(B, S // ts),
        in_specs=[tile(D), pl.BlockSpec(ada.shape, const3),
                  _resident(w_in_t.shape, const2), pl.BlockSpec(b_in.shape, const2),
                  pl.BlockSpec(w_pm.shape, const3), pl.BlockSpec(b_pm.shape, const2),
                  pl.BlockSpec(pool_scale.shape, const2)],
        out_specs=[tile_t, tile(D_ATT), tile_t, tile(LANES),
                   tile(D_ATT), tile(D_POOL)],
        out_shape=[jax.ShapeDtypeStruct((B, n_pairs, LANES, S), BF16),
                   jax.ShapeDtypeStruct((B, S, D_ATT), BF16),
                   jax.ShapeDtypeStruct((B, n_pairs, LANES, S), BF16),
                   jax.ShapeDtypeStruct((B, S, LANES), BF16),
                   jax.ShapeDtypeStruct((B, S, D_ATT), BF16),
                   jax.ShapeDtypeStruct((B, S, D_POOL), BF16)],
        scratch_shapes=[pltpu.VMEM((SUBLANES, LANES), F32),
                        pltpu.VMEM((POOL_HISTORY, D_POOL), F32),
                        pltpu.VMEM((D, IN_COLS), BF16),
                        pltpu.VMEM((1, IN_COLS), F32),
                        pltpu.VMEM((len(POOL_WINDOWS) // 2, 2 * POOL_GROUP_DIM,
                                    2 * POOL_GROUP_DIM), BF16)],
        compiler_params=pltpu.CompilerParams(
            dimension_semantics=("arbitrary", "arbitrary"),
            vmem_limit_bytes=VMEM_LIMIT_BYTES),
    )(x, ada, w_in_t, b_in, w_pm, b_pm, pool_scale)


def _attn_kernel(qt_ref, k_ref, vt_ref, g_ref, ga_ref, yp_ref, x_ref, ada_ref,
                 wo_ref, bo_ref, lg_ref, lb_ref, o_ref,
                 m_ref, acc_ref, s0_ref, s1_ref, mx0_ref, mx1_ref, rhs_ref, wob_ref):
    bi = pl.program_id(0)
    n_main = pl.program_id(1)
    tq = Q_TILE
    tk = KV_TILE
    diag_start = pl.multiple_of(n_main * tk, tk)

    @pl.when((bi == 0) & (n_main == 0))
    def _():
        wob_ref[...] = wo_ref[...].astype(BF16)

    def begin_tile(pos):
        m_ref[...] = jnp.full_like(m_ref, NEG_BIG)
        acc_ref[...] = jnp.zeros_like(acc_ref)
        row = lax.broadcasted_iota(jnp.int32, (LANES, tq), 0)
        for h in range(N_HEADS):
            pair, half = divmod(h, HEADS_PER_BLOCK)
            own = (row // HEAD_DIM) == half
            qt = qt_ref[0, pair, :, pos * tq:(pos + 1) * tq]
            rhs_ref[h, :LANES, :] = jnp.where(own, qt, jnp.zeros((), BF16))
            pick = (row < G_TERMS * N_HEADS) & (row % N_HEADS == h)
            rhs_ref[h, LANES:, :] = jnp.where(pick, 1.0, 0.0).astype(BF16)

    def scores(start, width, s_ref, mx_ref, diagonal=False):
        def head(h):
            pair = h // HEADS_PER_BLOCK
            kblk = k_ref[0, pl.ds(start, width), pair * LANES:(pair + 1) * LANES]
            gblk = g_ref[0, pl.ds(start, width), :]
            lhs = jnp.concatenate([kblk, gblk], axis=1)
            s = _dot(lhs, rhs_ref[h])
            if diagonal:
                key = lax.broadcasted_iota(jnp.int32, (tq, tq), 0)
                qry = lax.broadcasted_iota(jnp.int32, (tq, tq), 1)
                own = jnp.where(key <= qry, s[width - tq:, :], NEG_BIG)
                s = own if width == tq else jnp.concatenate([s[:width - tq, :], own], axis=0)
            s_ref[h, :width, :] = s
            mx_ref[h] = jnp.max(s.reshape(width // SUBLANES, SUBLANES, tq), axis=0)

        return [lambda h=h: head(h) for h in range(N_HEADS)]

    def softmax_pv(start, width, s_ref, mx_ref):
        ap = {}

        def probs(h):
            m_prev = m_ref[h]
            m_new = jnp.maximum(m_prev, jnp.max(mx_ref[h], axis=0, keepdims=True))
            m_ref[h] = m_new
            alpha = jnp.exp2(m_prev - m_new)
            p = jnp.exp2(s_ref[h, :width, :] - m_new[0:1, :]).astype(BF16)
            ap[h] = (alpha, p)

        def accumulate(h):
            alpha, p = ap.pop(h)
            pair, half = divmod(h, HEADS_PER_BLOCK)
            vt = vt_ref[0, pair, half * HEAD_DIM:(half + 1) * HEAD_DIM, pl.ds(start, width)]
            first = lax.broadcasted_iota(jnp.int32, (PV_ROWS - HEAD_DIM, width), 0) == 0
            ones = jnp.where(first, 1.0, 0.0).astype(BF16)
            pv = _dot(jnp.concatenate([vt, ones], axis=0), p)
            acc_ref[h] = acc_ref[h] * alpha[0:1, :] + pv

        def step(t):
            probs(t)
            accumulate(t)

        return [lambda t=t: step(t) for t in range(N_HEADS)]

    def run(*stages):
        for i in range(max(len(st) for st in stages)):
            for st in stages:
                if i < len(st):
                    st[i]()

    main = lambda k: pl.multiple_of(k * tk, tk)

    def first_scores():
        run(scores(main(0), tk, s0_ref, mx0_ref))

    def attend(pos):
        width = (pos + 1) * tq
        diag_scores = lambda s_ref, mx_ref: scores(diag_start, width, s_ref, mx_ref, True)
        diag_softmax = lambda s_ref, mx_ref: softmax_pv(diag_start, width, s_ref, mx_ref)
        n_pairs = jnp.maximum(n_main - 1, 0) // 2

        def pair_of_blocks(i, carry):
            k = 2 * i
            run(scores(main(k + 1), tk, s1_ref, mx1_ref),
                softmax_pv(main(k), tk, s0_ref, mx0_ref))
            run(scores(main(k + 2), tk, s0_ref, mx0_ref),
                softmax_pv(main(k + 1), tk, s1_ref, mx1_ref))
            return carry

        lax.fori_loop(0, n_pairs, pair_of_blocks, 0)
        k_last = 2 * n_pairs
        left = n_main - k_last

        @pl.when(left == 0)
        def _():
            run(diag_scores(s0_ref, mx0_ref))
            run(diag_softmax(s0_ref, mx0_ref))

        @pl.when(left == 1)
        def _():
            run(diag_scores(s1_ref, mx1_ref), softmax_pv(main(k_last), tk, s0_ref, mx0_ref))
            run(diag_softmax(s1_ref, mx1_ref))

        @pl.when(left == 2)
        def _():
            run(scores(main(k_last + 1), tk, s1_ref, mx1_ref),
                softmax_pv(main(k_last), tk, s0_ref, mx0_ref))
            run(diag_scores(s0_ref, mx0_ref), softmax_pv(main(k_last + 1), tk, s1_ref, mx1_ref))
            run(diag_softmax(s0_ref, mx0_ref))

    def gated_heads(pos):
        rows = slice(pos * tq, (pos + 1) * tq)
        gated = []
        for pair in range(N_HEADS // HEADS_PER_BLOCK):
            outs = []
            for half in range(HEADS_PER_BLOCK):
                acc = acc_ref[pair * HEADS_PER_BLOCK + half]
                outs.append(acc[:HEAD_DIM, :] / acc[HEAD_DIM:HEAD_DIM + 1, :])
            att = jnp.concatenate(outs, axis=0).T
            sl = slice(pair * LANES, (pair + 1) * LANES)
            gated.append((att * ga_ref[0, rows, sl].astype(F32)).astype(BF16))
        return jnp.concatenate(gated, axis=1)

    def project(pos, ya):
        rows = slice(pos * tq, (pos + 1) * tq)
        return _dot(jnp.concatenate([ya, yp_ref[0, rows, :]], axis=1), wob_ref[...]) + bo_ref[...]

    def residual_norm(pos, y):
        rows = slice(pos * tq, (pos + 1) * tq)
        hres = DEEPNORM_ALPHA * x_ref[0, rows, :] + ada_ref[2, pl.ds(bi, 1), :] * y
        mu = jnp.mean(hres, axis=-1, keepdims=True)
        d = hres - mu
        var = jnp.mean(d * d, axis=-1, keepdims=True)
        o_ref[0, rows, :] = d * lax.rsqrt(var + LN_EPS) * lg_ref[...] + lb_ref[...]

    n_pos = tk // tq
    begin_tile(0)

    @pl.when(n_main > 0)
    def _():
        first_scores()

    for pos in range(n_pos):
        attend(pos)
        ya = gated_heads(pos)
        if pos + 1 < n_pos:
            @pl.when(n_main > 0)
            def _():
                y = project(pos, ya)
                begin_tile(pos + 1)
                first_scores()
                residual_norm(pos, y)

            @pl.when(n_main == 0)
            def _():
                residual_norm(pos, project(pos, ya))
                begin_tile(pos + 1)
        else:
            residual_norm(pos, project(pos, ya))


def _attn_call(qt, k, vt, g, ga, yp, x, ada, w_out, b_out, ln_g, ln_b):
    B, n_pairs, _, S = qt.shape
    D = x.shape[2]
    assert KV_TILE % Q_TILE == 0 and S % KV_TILE == 0
    rows = KV_TILE
    tile = lambda width: pl.BlockSpec((1, rows, width), lambda b, i: (b, i, 0))
    whole = lambda width: pl.BlockSpec((1, S, width), lambda b, i: (b, 0, 0))
    const2 = lambda a: pl.BlockSpec(a.shape, lambda b, i: (0, 0))
    score_buf = pltpu.VMEM((N_HEADS, KV_TILE, Q_TILE), F32)
    stat_buf = pltpu.VMEM((N_HEADS, SUBLANES, Q_TILE), F32)
    return pl.pallas_call(
        _attn_kernel,
        name="attention",
        grid=(B, S // rows),
        in_specs=[pl.BlockSpec((1, n_pairs, LANES, rows), lambda b, i: (b, 0, 0, i)),
                  whole(D_ATT),
                  pl.BlockSpec((1, n_pairs, LANES, S), lambda b, i: (b, 0, 0, 0)),
                  whole(LANES), tile(D_ATT), tile(D_POOL), tile(D),
                  pl.BlockSpec(ada.shape, lambda b, i: (0, 0, 0)),
                  _resident(w_out.shape, lambda b, i: (0, 0)),
                  const2(b_out), const2(ln_g), const2(ln_b)],
        out_specs=tile(D),
        out_shape=jax.ShapeDtypeStruct((B, S, D), F32),
        scratch_shapes=[stat_buf, pltpu.VMEM((N_HEADS, PV_ROWS, Q_TILE), F32),
                        score_buf, score_buf, stat_buf, stat_buf,
                        pltpu.VMEM((N_HEADS, 2 * LANES, Q_TILE), BF16),
                        pltpu.VMEM(w_out.shape, BF16)],
        compiler_params=pltpu.CompilerParams(
            dimension_semantics=("arbitrary", "arbitrary"),
            vmem_limit_bytes=VMEM_LIMIT_BYTES),
    )(qt, k, vt, g, ga, yp, x, ada, w_out, b_out, ln_g, ln_b)


def _layer(x, c, w_ada, b_ada, w_in, b_in, w_pool_mix, b_pool_mix, pool_scale,
           w_out, b_out, ln_g, ln_b):
    ada = _ada_call(c, w_ada, b_ada[None, :])
    qt, k, vt, g, ga, yp = _in_call(x, ada, w_in.T, b_in[None, :], w_pool_mix, b_pool_mix,
                                    pool_scale[None, :])
    return _attn_call(qt, k, vt, g, ga, yp, x, ada, w_out, b_out[None, :], ln_g[None, :],
                      ln_b[None, :])


def kernel(x, c, w_ada, b_ada, w_in, b_in, w_pool_mix, b_pool_mix, pool_scale, w_out, b_out,
           ln_g, ln_b):
    for layer in range(w_ada.shape[0]):
        x = _layer(x, c, w_ada[layer], b_ada[layer], w_in[layer], b_in[layer],
                   w_pool_mix[layer], b_pool_mix[layer], pool_scale[layer],
                   w_out[layer], b_out[layer], ln_g[layer], ln_b[layer])
    return x
```

```python
import math

import jax
import jax.numpy as jnp
from jax import lax
from jax.experimental import pallas as pl
from jax.experimental.pallas import tpu as pltpu

D_MODEL = 1024
D_ATT = 512
D_POOL = 512
N_HEADS = 8
HEAD_DIM = 64
POOL_WINDOWS = (2, 4, 8, 16)
POOL_GROUP_DIM = 128
POOL_HISTORY = 16
LN_EPS = 1e-5
DEEPNORM_ALPHA = 2.0 ** 0.25
LOG2E = math.log2(math.e)
NEG_BIG = -1e30

LANES = 128
SUBLANES = 8
BF16_SUBLANES = 16
HEADS_PER_BLOCK = LANES // HEAD_DIM
G_TERMS = 3
CUM_ROWS = 128
PV_ROWS = HEAD_DIM + BF16_SUBLANES

IN_SOURCE = (("q", D_ATT), ("k", D_ATT), ("v", D_ATT), ("f", N_HEADS), ("p", D_POOL),
             ("ga", D_ATT), ("gp", D_POOL))
IN_WIDTH = {"f": LANES, "p": D_POOL, "gp": D_POOL, "q": D_ATT, "v": D_ATT, "k": D_ATT,
            "ga": D_ATT}
IN_OFFSET = dict(zip(IN_WIDTH, (sum(list(IN_WIDTH.values())[:i]) for i in range(len(IN_WIDTH)))))
IN_COLS = sum(IN_WIDTH.values())

SEQ_TILE = 512
Q_TILE = 256
KV_TILE = 512
VMEM_LIMIT_BYTES = 56 * 1024 * 1024

F32 = jnp.float32
BF16 = jnp.bfloat16


def _silu(x):
    return x * jax.nn.sigmoid(x)


def _dot(a, b):
    return jnp.dot(a, b, preferred_element_type=F32)


def _resident(shape, index_map):
    return pl.BlockSpec(shape, index_map, pipeline_mode=pl.Buffered(1))


def _ada_kernel(c_ref, w_ref, b_ref, o_ref, cpad_ref):
    n_rows = c_ref.shape[0]
    cpad_ref[...] = jnp.zeros_like(cpad_ref)
    cpad_ref[0:n_rows, :] = c_ref[...]
    sc = _silu(cpad_ref[...].T)
    for b in range(n_rows):
        o_ref[0, b:b + 1, :] = (jnp.sum(w_ref[...] * sc[:, b:b + 1], axis=0, keepdims=True)
                                + b_ref[...])


def _ada_call(c, w_ada, b_ada):
    n_rows, d = c.shape
    n_chunks = w_ada.shape[1] // d
    return pl.pallas_call(
        _ada_kernel,
        name="ada_vector",
        grid=(n_chunks,),
        in_specs=[
            pl.BlockSpec((n_rows, d), lambda j: (0, 0)),
            pl.BlockSpec((d, d), lambda j: (0, j)),
            pl.BlockSpec((1, d), lambda j: (0, j)),
        ],
        out_specs=pl.BlockSpec((1, n_rows, d), lambda j: (j, 0, 0)),
        out_shape=jax.ShapeDtypeStruct((n_chunks, n_rows, d), F32),
        scratch_shapes=[pltpu.VMEM((LANES, d), F32)],
        compiler_params=pltpu.CompilerParams(dimension_semantics=("parallel",),
                                             vmem_limit_bytes=VMEM_LIMIT_BYTES),
    )(c, w_ada, b_ada)


def _split3(x):
    hi = x.astype(BF16).astype(F32)
    r = x - hi
    mid = r.astype(BF16).astype(F32)
    lo = (r - mid).astype(BF16).astype(F32)
    return hi, mid, lo


def _in_kernel(x_ref, ada_ref, wt_ref, b_ref, wpm_ref, bpm_ref, ps_ref,
               q_ref, k_ref, v_ref, g_ref, ga_ref, yp_ref,
               fcarry_ref, pcarry_ref, wcat_ref, bcat_ref, wmix_ref):
    bi = pl.program_id(0)
    si = pl.program_id(1)
    ts = x_ref.shape[1]

    @pl.when((bi == 0) & (si == 0))
    def _():
        src = 0
        for name, width in IN_SOURCE:
            dst, wide = IN_OFFSET[name], IN_WIDTH[name]
            w_seg = wt_ref[src:src + wide, :].T
            b_seg = b_ref[:, src:src + wide]
            if width < wide:
                w_seg = jnp.where(lax.broadcasted_iota(jnp.int32, w_seg.shape, 1) < width,
                                  w_seg, 0.0)
                b_seg = jnp.where(lax.broadcasted_iota(jnp.int32, b_seg.shape, 1) < width,
                                  b_seg, 0.0)
            wcat_ref[:, dst:dst + wide] = w_seg.astype(BF16)
            bcat_ref[:, dst:dst + wide] = b_seg
            src += width
        gd = POOL_GROUP_DIM
        wmix_ref[...] = jnp.zeros_like(wmix_ref)
        for g in range(len(POOL_WINDOWS)):
            o = (g % 2) * gd
            wmix_ref[g // 2, o:o + gd, o:o + gd] = wpm_ref[g].astype(BF16)

    @pl.when(si == 0)
    def _():
        fcarry_ref[...] = jnp.zeros_like(fcarry_ref)
        pcarry_ref[...] = jnp.zeros_like(pcarry_ref)

    shift = ada_ref[0, pl.ds(bi, 1), :]
    scale = ada_ref[1, pl.ds(bi, 1), :]
    u = (x_ref[0] * (1.0 + scale) + shift).astype(BF16)
    proj = _dot(u, wcat_ref[...]) + bcat_ref[...]
    seg = lambda name: proj[:, IN_OFFSET[name]:IN_OFFSET[name] + IN_WIDTH[name]]

    q = seg("q") * (HEAD_DIM ** -0.5 * LOG2E)
    for blk in range(D_ATT // LANES):
        q_ref[0, blk] = q[:, blk * LANES:(blk + 1) * LANES].T.astype(BF16)

    k_ref[0] = seg("k").astype(BF16)

    v = seg("v")
    for blk in range(D_ATT // LANES):
        v_ref[0, blk] = v[:, blk * LANES:(blk + 1) * LANES].T.astype(BF16)

    fl = seg("f")
    logf = jnp.minimum(fl, 0.0) - jnp.log1p(jnp.exp(-jnp.abs(fl)))
    row = lax.broadcasted_iota(jnp.int32, (CUM_ROWS, CUM_ROWS), 0)
    col = lax.broadcasted_iota(jnp.int32, (CUM_ROWS, CUM_ROWS), 1)
    tri = jnp.where(row >= col, 1.0, 0.0).astype(BF16)
    terms = jnp.concatenate([t.astype(BF16) for t in _split3(logf)], axis=1)
    offset = fcarry_ref[0:1, :]
    cum_blocks = []
    for r in range(ts // CUM_ROWS):
        part = _dot(tri, terms[r * CUM_ROWS:(r + 1) * CUM_ROWS, :])
        local = part[:, :LANES] + part[:, LANES:2 * LANES] + part[:, 2 * LANES:]
        cum_blocks.append(local + offset)
        offset = cum_blocks[-1][CUM_ROWS - 1:CUM_ROWS, :]
    fcarry_ref[...] = jnp.broadcast_to(offset, fcarry_ref.shape)
    cum = jnp.concatenate(cum_blocks, axis=0)

    g_hi, g_mid, g_lo = _split3(cum * (-LOG2E))
    lane = lax.broadcasted_iota(jnp.int32, (ts, LANES), 1)
    g = jnp.where(lane < N_HEADS, g_hi,
                  jnp.where(lane < 2 * N_HEADS, pltpu.roll(g_mid, N_HEADS, axis=1),
                            jnp.where(lane < 3 * N_HEADS, pltpu.roll(g_lo, 2 * N_HEADS, axis=1),
                                      0.0)))
    g_ref[0] = g.astype(BF16)

    p = seg("p")
    pe = jnp.concatenate([pcarry_ref[...], p], axis=0)
    pcarry_ref[...] = p[ts - POOL_HISTORY:, :]
    gp = seg("gp")
    t_glob = si * ts + lax.broadcasted_iota(jnp.int32, (ts, POOL_GROUP_DIM), 0)
    pooled = []
    for g, w in enumerate(POOL_WINDOWS):
        sl = slice(g * POOL_GROUP_DIM, (g + 1) * POOL_GROUP_DIM)
        y = pe[:, sl]
        sh = 1
        while sh < w:
            y = y + pltpu.roll(y, sh, axis=0)
            sh *= 2
        cnt = jnp.minimum(t_glob + 1, w).astype(F32)
        pooled.append((y[POOL_HISTORY:, :] / cnt - p[:, sl]).astype(BF16))
    for j in range(len(POOL_WINDOWS) // 2):
        sl = slice(2 * j * POOL_GROUP_DIM, (2 * j + 2) * POOL_GROUP_DIM)
        bias = jnp.concatenate([bpm_ref[2 * j:2 * j + 1, :], bpm_ref[2 * j + 1:2 * j + 2, :]],
                               axis=1)
        mixed = _dot(jnp.concatenate(pooled[2 * j:2 * j + 2], axis=1), wmix_ref[j]) + bias
        yp_ref[0, :, sl] = (mixed * ps_ref[:, sl] * _silu(gp[:, sl])).astype(BF16)

    ga_ref[0] = _silu(seg("ga")).astype(BF16)


def _in_call(x, ada, w_in_t, b_in, w_pm, b_pm, pool_scale):
    B, S, D = x.shape
    ts = SEQ_TILE
    const2 = lambda b, s: (0, 0)
    const3 = lambda b, s: (0, 0, 0)
    tile = lambda width: pl.BlockSpec((1, ts, width), lambda b, s: (b, s, 0))
    n_pairs = D_ATT // LANES
    tile_t = pl.BlockSpec((1, n_pairs, LANES, ts), lambda b, s: (b, 0, 0, s))
    return pl.pallas_call(
        _in_kernel,
        name="input_stage",
        grid=(B, S // ts),
        in_specs=[tile(D), pl.BlockSpec(ada.shape, const3),
                  _resident(w_in_t.shape, const2), pl.BlockSpec(b_in.shape, const2),
                  pl.BlockSpec(w_pm.shape, const3), pl.BlockSpec(b_pm.shape, const2),
                  pl.BlockSpec(pool_scale.shape, const2)],
        out_specs=[tile_t, tile(D_ATT), tile_t, tile(LANES),
                   tile(D_ATT), tile(D_POOL)],
        out_shape=[jax.ShapeDtypeStruct((B, n_pairs, LANES, S), BF16),
                   jax.ShapeDtypeStruct((B, S, D_ATT), BF16),
                   jax.ShapeDtypeStruct((B, n_pairs, LANES, S), BF16),
                   jax.ShapeDtypeStruct((B, S, LANES), BF16),
                   jax.ShapeDtypeStruct((B, S, D_ATT), BF16),
                   jax.ShapeDtypeStruct((B, S, D_POOL), BF16)],
        scratch_shapes=[pltpu.VMEM((SUBLANES, LANES), F32),
                        pltpu.VMEM((POOL_HISTORY, D_POOL), F32),
                        pltpu.VMEM((D, IN_COLS), BF16),
                        pltpu.VMEM((1, IN_COLS), F32),
                        pltpu.VMEM((len(POOL_WINDOWS) // 2, 2 * POOL_GROUP_DIM,
                                    2 * POOL_GROUP_DIM), BF16)],
        compiler_params=pltpu.CompilerParams(
            dimension_semantics=("arbitrary", "arbitrary"),
            vmem_limit_bytes=VMEM_LIMIT_BYTES),
    )(x, ada, w_in_t, b_in, w_pm, b_pm, pool_scale)


def _attn_kernel(qt_ref, k_ref, vt_ref, g_ref, ga_ref, yp_ref, x_ref, ada_ref,
                 wo_ref, bo_ref, lg_ref, lb_ref, o_ref,
                 m_ref, acc_ref, s0_ref, s1_ref, mx0_ref, mx1_ref, rhs_ref, wob_ref):
    bi = pl.program_id(0)
    n_main = pl.program_id(1)
    tq = Q_TILE
    tk = KV_TILE
    diag_start = pl.multiple_of(n_main * tk, tk)

    @pl.when((bi == 0) & (n_main == 0))
    def _():
        wob_ref[...] = wo_ref[...].astype(BF16)

    def begin_tile(pos):
        m_ref[...] = jnp.full_like(m_ref, NEG_BIG)
        acc_ref[...] = jnp.zeros_like(acc_ref)
        row = lax.broadcasted_iota(jnp.int32, (LANES, tq), 0)
        for h in range(N_HEADS):
            pair, half = divmod(h, HEADS_PER_BLOCK)
            own = (row // HEAD_DIM) == half
            qt = qt_ref[0, pair, :, pos * tq:(pos + 1) * tq]
            rhs_ref[h, :LANES, :] = jnp.where(own, qt, jnp.zeros((), BF16))
            pick = (row < G_TERMS * N_HEADS) & (row % N_HEADS == h)
            rhs_ref[h, LANES:, :] = jnp.where(pick, 1.0, 0.0).astype(BF16)

    def scores(start, width, s_ref, mx_ref, diagonal=False):
        def head(h):
            pair = h // HEADS_PER_BLOCK
            kblk = k_ref[0, pl.ds(start, width), pair * LANES:(pair + 1) * LANES]
            gblk = g_ref[0, pl.ds(start, width), :]
            lhs = jnp.concatenate([kblk, gblk], axis=1)
            s = _dot(lhs, rhs_ref[h])
            if diagonal:
                key = lax.broadcasted_iota(jnp.int32, (tq, tq), 0)
                qry = lax.broadcasted_iota(jnp.int32, (tq, tq), 1)
                own = jnp.where(key <= qry, s[width - tq:, :], NEG_BIG)
                s = own if width == tq else jnp.concatenate([s[:width - tq, :], own], axis=0)
            s_ref[h, :width, :] = s
            mx_ref[h] = jnp.max(s.reshape(width // SUBLANES, SUBLANES, tq), axis=0)

        return [lambda h=h: head(h) for h in range(N_HEADS)]

    def softmax_pv(start, width, s_ref, mx_ref):
        ap = {}

        def probs(h):
            m_prev = m_ref[h]
            m_new = jnp.maximum(m_prev, jnp.max(mx_ref[h], axis=0, keepdims=True))
            m_ref[h] = m_new
            alpha = jnp.exp2(m_prev - m_new)
            p = jnp.exp2(s_ref[h, :width, :] - m_new[0:1, :]).astype(BF16)
            ap[h] = (alpha, p)

        def accumulate(h):
            alpha, p = ap.pop(h)
            pair, half = divmod(h, HEADS_PER_BLOCK)
            vt = vt_ref[0, pair, half * HEAD_DIM:(half + 1) * HEAD_DIM, pl.ds(start, width)]
            first = lax.broadcasted_iota(jnp.int32, (PV_ROWS - HEAD_DIM, width), 0) == 0
            ones = jnp.where(first, 1.0, 0.0).astype(BF16)
            pv = _dot(jnp.concatenate([vt, ones], axis=0), p)
            acc_ref[h] = acc_ref[h] * alpha[0:1, :] + pv

        def step(t):
            probs(t)
            accumulate(t)

        return [lambda t=t: step(t) for t in range(N_HEADS)]

    def run(*stages):
        for i in range(max(len(st) for st in stages)):
            for st in stages:
                if i < len(st):
                    st[i]()

    main = lambda k: pl.multiple_of(k * tk, tk)

    def first_scores():
        run(scores(main(0), tk, s0_ref, mx0_ref))

    def attend(pos):
        width = (pos + 1) * tq
        diag_scores = lambda s_ref, mx_ref: scores(diag_start, width, s_ref, mx_ref, True)
        diag_softmax = lambda s_ref, mx_ref: softmax_pv(diag_start, width, s_ref, mx_ref)
        n_pairs = jnp.maximum(n_main - 1, 0) // 2

        def pair_of_blocks(k):
            run(scores(main(k + 1), tk, s1_ref, mx1_ref),
                softmax_pv(main(k), tk, s0_ref, mx0_ref))
            run(scores(main(k + 2), tk, s0_ref, mx0_ref),
                softmax_pv(main(k + 1), tk, s1_ref, mx1_ref))

        def two_pairs(i, carry):
            pair_of_blocks(4 * i)
            pair_of_blocks(4 * i + 2)
            return carry

        lax.fori_loop(0, n_pairs // 2, two_pairs, 0)

        @pl.when(n_pairs % 2 == 1)
        def _():
            pair_of_blocks(2 * (n_pairs - 1))

        k_last = 2 * n_pairs
        left = n_main - k_last

        @pl.when(left == 0)
        def _():
            run(diag_scores(s0_ref, mx0_ref))
            run(diag_softmax(s0_ref, mx0_ref))

        @pl.when(left == 1)
        def _():
            run(diag_scores(s1_ref, mx1_ref), softmax_pv(main(k_last), tk, s0_ref, mx0_ref))
            run(diag_softmax(s1_ref, mx1_ref))

        @pl.when(left == 2)
        def _():
            run(scores(main(k_last + 1), tk, s1_ref, mx1_ref),
                softmax_pv(main(k_last), tk, s0_ref, mx0_ref))
            run(diag_scores(s0_ref, mx0_ref), softmax_pv(main(k_last + 1), tk, s1_ref, mx1_ref))
            run(diag_softmax(s0_ref, mx0_ref))

    def gated_heads(pos):
        rows = slice(pos * tq, (pos + 1) * tq)
        gated = []
        for pair in range(N_HEADS // HEADS_PER_BLOCK):
            outs = []
            for half in range(HEADS_PER_BLOCK):
                acc = acc_ref[pair * HEADS_PER_BLOCK + half]
                outs.append(acc[:HEAD_DIM, :] / acc[HEAD_DIM:HEAD_DIM + 1, :])
            att = jnp.concatenate(outs, axis=0).T
            sl = slice(pair * LANES, (pair + 1) * LANES)
            gated.append((att * ga_ref[0, rows, sl].astype(F32)).astype(BF16))
        return jnp.concatenate(gated, axis=1)

    def project(pos, ya):
        rows = slice(pos * tq, (pos + 1) * tq)
        return _dot(jnp.concatenate([ya, yp_ref[0, rows, :]], axis=1), wob_ref[...]) + bo_ref[...]

    def residual_norm(pos, y):
        rows = slice(pos * tq, (pos + 1) * tq)
        hres = DEEPNORM_ALPHA * x_ref[0, rows, :] + ada_ref[2, pl.ds(bi, 1), :] * y
        mu = jnp.mean(hres, axis=-1, keepdims=True)
        d = hres - mu
        var = jnp.mean(d * d, axis=-1, keepdims=True)
        o_ref[0, rows, :] = d * lax.rsqrt(var + LN_EPS) * lg_ref[...] + lb_ref[...]

    n_pos = tk // tq
    begin_tile(0)

    @pl.when(n_main > 0)
    def _():
        first_scores()

    for pos in range(n_pos):
        attend(pos)
        ya = gated_heads(pos)
        if pos + 1 < n_pos:
            @pl.when(n_main > 0)
            def _():
                y = project(pos, ya)
                begin_tile(pos + 1)
                first_scores()
                residual_norm(pos, y)

            @pl.when(n_main == 0)
            def _():
                residual_norm(pos, project(pos, ya))
                begin_tile(pos + 1)
        else:
            residual_norm(pos, project(pos, ya))


def _attn_call(qt, k, vt, g, ga, yp, x, ada, w_out, b_out, ln_g, ln_b):
    B, n_pairs, _, S = qt.shape
    D = x.shape[2]
    assert KV_TILE % Q_TILE == 0 and S % KV_TILE == 0
    rows = KV_TILE
    tile = lambda width: pl.BlockSpec((1, rows, width), lambda b, i: (b, i, 0))
    whole = lambda width: pl.BlockSpec((1, S, width), lambda b, i: (b, 0, 0))
    const2 = lambda a: pl.BlockSpec(a.shape, lambda b, i: (0, 0))
    score_buf = pltpu.VMEM((N_HEADS, KV_TILE, Q_TILE), F32)
    stat_buf = pltpu.VMEM((N_HEADS, SUBLANES, Q_TILE), F32)
    return pl.pallas_call(
        _attn_kernel,
        name="attention",
        grid=(B, S // rows),
        in_specs=[pl.BlockSpec((1, n_pairs, LANES, rows), lambda b, i: (b, 0, 0, i)),
                  whole(D_ATT),
                  pl.BlockSpec((1, n_pairs, LANES, S), lambda b, i: (b, 0, 0, 0)),
                  whole(LANES), tile(D_ATT), tile(D_POOL), tile(D),
                  pl.BlockSpec(ada.shape, lambda b, i: (0, 0, 0)),
                  _resident(w_out.shape, lambda b, i: (0, 0)),
                  const2(b_out), const2(ln_g), const2(ln_b)],
        out_specs=tile(D),
        out_shape=jax.ShapeDtypeStruct((B, S, D), F32),
        scratch_shapes=[stat_buf, pltpu.VMEM((N_HEADS, PV_ROWS, Q_TILE), F32),
                        score_buf, score_buf, stat_buf, stat_buf,
                        pltpu.VMEM((N_HEADS, 2 * LANES, Q_TILE), BF16),
                        pltpu.VMEM(w_out.shape, BF16)],
        compiler_params=pltpu.CompilerParams(
            dimension_semantics=("arbitrary", "arbitrary"),
            vmem_limit_bytes=VMEM_LIMIT_BYTES),
    )(qt, k, vt, g, ga, yp, x, ada, w_out, b_out, ln_g, ln_b)


def _layer(x, c, w_ada, b_ada, w_in, b_in, w_pool_mix, b_pool_mix, pool_scale,
           w_out, b_out, ln_g, ln_b):
    ada = _ada_call(c, w_ada, b_ada[None, :])
    qt, k, vt, g, ga, yp = _in_call(x, ada, w_in.T, b_in[None, :], w_pool_mix, b_pool_mix,
                                    pool_scale[None, :])
    return _attn_call(qt, k, vt, g, ga, yp, x, ada, w_out, b_out[None, :], ln_g[None, :],
                      ln_b[None, :])


def kernel(x, c, w_ada, b_ada, w_in, b_in, w_pool_mix, b_pool_mix, pool_scale, w_out, b_out,
           ln_g, ln_b):
    for layer in range(w_ada.shape[0]):
        x = _layer(x, c, w_ada[layer], b_ada[layer], w_in[layer], b_in[layer],
                   w_pool_mix[layer], b_pool_mix[layer], pool_scale[layer],
                   w_out[layer], b_out[layer], ln_g[layer], ln_b[layer])
    return x
```

```python
import math

import jax
import jax.numpy as jnp
from jax import lax
from jax.experimental import pallas as pl
from jax.experimental.pallas import tpu as pltpu

D_MODEL = 1024
D_ATT = 512
D_POOL = 512
N_HEADS = 8
HEAD_DIM = 64
POOL_WINDOWS = (2, 4, 8, 16)
POOL_GROUP_DIM = 128
POOL_HISTORY = 16
LN_EPS = 1e-5
DEEPNORM_ALPHA = 2.0 ** 0.25
LOG2E = math.log2(math.e)
NEG_BIG = -1e30

LANES = 128
SUBLANES = 8
BF16_SUBLANES = 16
HEADS_PER_BLOCK = LANES // HEAD_DIM
G_TERMS = 3
CUM_ROWS = 128
PV_ROWS = HEAD_DIM + BF16_SUBLANES

IN_SOURCE = (("q", D_ATT), ("k", D_ATT), ("v", D_ATT), ("f", N_HEADS), ("p", D_POOL),
             ("ga", D_ATT), ("gp", D_POOL))
IN_WIDTH = {"f": LANES, "p": D_POOL, "gp": D_POOL, "q": D_ATT, "v": D_ATT, "k": D_ATT,
            "ga": D_ATT}
IN_OFFSET = dict(zip(IN_WIDTH, (sum(list(IN_WIDTH.values())[:i]) for i in range(len(IN_WIDTH)))))
IN_COLS = sum(IN_WIDTH.values())

SEQ_TILE = 512
Q_TILE = 256
KV_TILE = 512
VMEM_LIMIT_BYTES = 56 * 1024 * 1024

F32 = jnp.float32
BF16 = jnp.bfloat16


def _silu(x):
    return x * jax.nn.sigmoid(x)


def _dot(a, b):
    return jnp.dot(a, b, preferred_element_type=F32)


def _resident(shape, index_map):
    return pl.BlockSpec(shape, index_map, pipeline_mode=pl.Buffered(1))


def _ada_kernel(c_ref, w_ref, b_ref, o_ref, cpad_ref):
    n_rows = c_ref.shape[0]
    cpad_ref[...] = jnp.zeros_like(cpad_ref)
    cpad_ref[0:n_rows, :] = c_ref[...]
    sc = _silu(cpad_ref[...].T)
    for b in range(n_rows):
        o_ref[0, b:b + 1, :] = (jnp.sum(w_ref[...] * sc[:, b:b + 1], axis=0, keepdims=True)
                                + b_ref[...])


def _ada_call(c, w_ada, b_ada):
    n_rows, d = c.shape
    n_chunks = w_ada.shape[1] // d
    return pl.pallas_call(
        _ada_kernel,
        name="ada_vector",
        grid=(n_chunks,),
        in_specs=[
            pl.BlockSpec((n_rows, d), lambda j: (0, 0)),
            pl.BlockSpec((d, d), lambda j: (0, j)),
            pl.BlockSpec((1, d), lambda j: (0, j)),
        ],
        out_specs=pl.BlockSpec((1, n_rows, d), lambda j: (j, 0, 0)),
        out_shape=jax.ShapeDtypeStruct((n_chunks, n_rows, d), F32),
        scratch_shapes=[pltpu.VMEM((LANES, d), F32)],
        compiler_params=pltpu.CompilerParams(dimension_semantics=("parallel",),
                                             vmem_limit_bytes=VMEM_LIMIT_BYTES),
    )(c, w_ada, b_ada)


def _split3(x):
    hi = x.astype(BF16).astype(F32)
    r = x - hi
    mid = r.astype(BF16).astype(F32)
    lo = (r - mid).astype(BF16).astype(F32)
    return hi, mid, lo


def _in_kernel(x_ref, ada_ref, wt_ref, b_ref, wpm_ref, bpm_ref, ps_ref,
               q_ref, k_ref, v_ref, g_ref, ga_ref, yp_ref,
               fcarry_ref, pcarry_ref, wcat_ref, bcat_ref, wmix_ref):
    bi = pl.program_id(0)
    si = pl.program_id(1)
    ts = x_ref.shape[1]

    @pl.when((bi == 0) & (si == 0))
    def _():
        src = 0
        for name, width in IN_SOURCE:
            dst, wide = IN_OFFSET[name], IN_WIDTH[name]
            w_seg = wt_ref[src:src + wide, :].T
            b_seg = b_ref[:, src:src + wide]
            if width < wide:
                w_seg = jnp.where(lax.broadcasted_iota(jnp.int32, w_seg.shape, 1) < width,
                                  w_seg, 0.0)
                b_seg = jnp.where(lax.broadcasted_iota(jnp.int32, b_seg.shape, 1) < width,
                                  b_seg, 0.0)
            wcat_ref[:, dst:dst + wide] = w_seg.astype(BF16)
            bcat_ref[:, dst:dst + wide] = b_seg
            src += width
        gd = POOL_GROUP_DIM
        wmix_ref[...] = jnp.zeros_like(wmix_ref)
        for g in range(len(POOL_WINDOWS)):
            o = (g % 2) * gd
            wmix_ref[g // 2, o:o + gd, o:o + gd] = wpm_ref[g].astype(BF16)

    @pl.when(si == 0)
    def _():
        fcarry_ref[...] = jnp.zeros_like(fcarry_ref)
        pcarry_ref[...] = jnp.zeros_like(pcarry_ref)

    shift = ada_ref[0, pl.ds(bi, 1), :]
    scale = ada_ref[1, pl.ds(bi, 1), :]
    u = (x_ref[0] * (1.0 + scale) + shift).astype(BF16)
    proj = _dot(u, wcat_ref[...]) + bcat_ref[...]
    seg = lambda name: proj[:, IN_OFFSET[name]:IN_OFFSET[name] + IN_WIDTH[name]]

    q = seg("q") * (HEAD_DIM ** -0.5 * LOG2E)
    for blk in range(D_ATT // LANES):
        q_ref[0, blk] = q[:, blk * LANES:(blk + 1) * LANES].T.astype(BF16)

    k_ref[0] = seg("k").astype(BF16)

    v = seg("v")
    for blk in range(D_ATT // LANES):
        v_ref[0, blk] = v[:, blk * LANES:(blk + 1) * LANES].T.astype(BF16)

    fl = seg("f")
    logf = jnp.minimum(fl, 0.0) - jnp.log1p(jnp.exp(-jnp.abs(fl)))
    row = lax.broadcasted_iota(jnp.int32, (CUM_ROWS, CUM_ROWS), 0)
    col = lax.broadcasted_iota(jnp.int32, (CUM_ROWS, CUM_ROWS), 1)
    tri = jnp.where(row >= col, 1.0, 0.0).astype(BF16)
    terms = jnp.concatenate([t.astype(BF16) for t in _split3(logf)], axis=1)
    offset = fcarry_ref[0:1, :]
    cum_blocks = []
    for r in range(ts // CUM_ROWS):
        part = _dot(tri, terms[r * CUM_ROWS:(r + 1) * CUM_ROWS, :])
        local = part[:, :LANES] + part[:, LANES:2 * LANES] + part[:, 2 * LANES:]
        cum_blocks.append(local + offset)
        offset = cum_blocks[-1][CUM_ROWS - 1:CUM_ROWS, :]
    fcarry_ref[...] = jnp.broadcast_to(offset, fcarry_ref.shape)
    cum = jnp.concatenate(cum_blocks, axis=0)

    g_hi, g_mid, g_lo = _split3(cum * (-LOG2E))
    lane = lax.broadcasted_iota(jnp.int32, (ts, LANES), 1)
    g = jnp.where(lane < N_HEADS, g_hi,
                  jnp.where(lane < 2 * N_HEADS, pltpu.roll(g_mid, N_HEADS, axis=1),
                            jnp.where(lane < 3 * N_HEADS, pltpu.roll(g_lo, 2 * N_HEADS, axis=1),
                                      0.0)))
    g_ref[0] = g.astype(BF16)

    p = seg("p")
    pe = jnp.concatenate([pcarry_ref[...], p], axis=0)
    pcarry_ref[...] = p[ts - POOL_HISTORY:, :]
    gp = seg("gp")
    t_glob = si * ts + lax.broadcasted_iota(jnp.int32, (ts, POOL_GROUP_DIM), 0)
    pooled = []
    for g, w in enumerate(POOL_WINDOWS):
        sl = slice(g * POOL_GROUP_DIM, (g + 1) * POOL_GROUP_DIM)
        y = pe[:, sl]
        sh = 1
        while sh < w:
            y = y + pltpu.roll(y, sh, axis=0)
            sh *= 2
        cnt = jnp.minimum(t_glob + 1, w).astype(F32)
        pooled.append((y[POOL_HISTORY:, :] / cnt - p[:, sl]).astype(BF16))
    for j in range(len(POOL_WINDOWS) // 2):
        sl = slice(2 * j * POOL_GROUP_DIM, (2 * j + 2) * POOL_GROUP_DIM)
        bias = jnp.concatenate([bpm_ref[2 * j:2 * j + 1, :], bpm_ref[2 * j + 1:2 * j + 2, :]],
                               axis=1)
        mixed = _dot(jnp.concatenate(pooled[2 * j:2 * j + 2], axis=1), wmix_ref[j]) + bias
        yp_ref[0, :, sl] = (mixed * ps_ref[:, sl] * _silu(gp[:, sl])).astype(BF16)

    ga_ref[0] = _silu(seg("ga")).astype(BF16)


def _in_call(x, ada, w_in_t, b_in, w_pm, b_pm, pool_scale):
    B, S, D = x.shape
    ts = SEQ_TILE
    const2 = lambda b, s: (0, 0)
    const3 = lambda b, s: (0, 0, 0)
    tile = lambda width: pl.BlockSpec((1, ts, width), lambda b, s: (b, s, 0))
    n_pairs = D_ATT // LANES
    tile_t = pl.BlockSpec((1, n_pairs, LANES, ts), lambda b, s: (b, 0, 0, s))
    return pl.pallas_call(
        _in_kernel,
        name="input_stage",
        grid=(B, S // ts),
        in_specs=[tile(D), pl.BlockSpec(ada.shape, const3),
                  _resident(w_in_t.shape, const2), pl.BlockSpec(b_in.shape, const2),
                  pl.BlockSpec(w_pm.shape, const3), pl.BlockSpec(b_pm.shape, const2),
                  pl.BlockSpec(pool_scale.shape, const2)],
        out_specs=[tile_t, tile(D_ATT), tile_t, tile(LANES),
                   tile(D_ATT), tile(D_POOL)],
        out_shape=[jax.ShapeDtypeStruct((B, n_pairs, LANES, S), BF16),
                   jax.ShapeDtypeStruct((B, S, D_ATT), BF16),
                   jax.ShapeDtypeStruct((B, n_pairs, LANES, S), BF16),
                   jax.ShapeDtypeStruct((B, S, LANES), BF16),
                   jax.ShapeDtypeStruct((B, S, D_ATT), BF16),
                   jax.ShapeDtypeStruct((B, S, D_POOL), BF16)],
        scratch_shapes=[pltpu.VMEM((SUBLANES, LANES), F32),
                        pltpu.VMEM((POOL_HISTORY, D_POOL), F32),
                        pltpu.VMEM((D, IN_COLS), BF16),
                        pltpu.VMEM((1, IN_COLS), F32),
                        pltpu.VMEM((len(POOL_WINDOWS) // 2, 2 * POOL_GROUP_DIM,
                                    2 * POOL_GROUP_DIM), BF16)],
        compiler_params=pltpu.CompilerParams(
            dimension_semantics=("arbitrary", "arbitrary"),
            vmem_limit_bytes=VMEM_LIMIT_BYTES),
    )(x, ada, w_in_t, b_in, w_pm, b_pm, pool_scale)


def _attn_kernel(qt_ref, k_ref, vt_ref, g_ref, ga_ref, yp_ref, x_ref, ada_ref,
                 wo_ref, bo_ref, lg_ref, lb_ref, o_ref,
                 m_ref, acc_ref, s0_ref, s1_ref, mx0_ref, mx1_ref, rhs_ref, wob_ref):
    bi = pl.program_id(0)
    n_main = pl.program_id(1)
    tq = Q_TILE
    tk = KV_TILE
    n_steps = k_ref.shape[1] // tk

    @pl.when((bi == 0) & (n_main == 0))
    def _():
        wob_ref[...] = wo_ref[...].astype(BF16)

    def begin_tile(pos):
        m_ref[...] = jnp.full_like(m_ref, NEG_BIG)
        acc_ref[...] = jnp.zeros_like(acc_ref)
        row = lax.broadcasted_iota(jnp.int32, (LANES, tq), 0)
        for h in range(N_HEADS):
            pair, half = divmod(h, HEADS_PER_BLOCK)
            own = (row // HEAD_DIM) == half
            qt = qt_ref[0, pair, :, pos * tq:(pos + 1) * tq]
            rhs_ref[h, :LANES, :] = jnp.where(own, qt, jnp.zeros((), BF16))
            pick = (row < G_TERMS * N_HEADS) & (row % N_HEADS == h)
            rhs_ref[h, LANES:, :] = jnp.where(pick, 1.0, 0.0).astype(BF16)

    def scores(start, width, s_ref, mx_ref, diagonal=False):
        def head(h):
            pair = h // HEADS_PER_BLOCK
            kblk = k_ref[0, pl.ds(start, width), pair * LANES:(pair + 1) * LANES]
            gblk = g_ref[0, pl.ds(start, width), :]
            lhs = jnp.concatenate([kblk, gblk], axis=1)
            s = _dot(lhs, rhs_ref[h])
            if diagonal:
                key = lax.broadcasted_iota(jnp.int32, (tq, tq), 0)
                qry = lax.broadcasted_iota(jnp.int32, (tq, tq), 1)
                own = jnp.where(key <= qry, s[width - tq:, :], NEG_BIG)
                s = own if width == tq else jnp.concatenate([s[:width - tq, :], own], axis=0)
            s_ref[h, :width, :] = s
            mx_ref[h] = jnp.max(s.reshape(width // SUBLANES, SUBLANES, tq), axis=0)

        return [lambda h=h: head(h) for h in range(N_HEADS)]

    def softmax_pv(start, width, s_ref, mx_ref):
        ap = {}

        def probs(h):
            m_prev = m_ref[h]
            m_new = jnp.maximum(m_prev, jnp.max(mx_ref[h], axis=0, keepdims=True))
            m_ref[h] = m_new
            alpha = jnp.exp2(m_prev - m_new)
            p = jnp.exp2(s_ref[h, :width, :] - m_new[0:1, :]).astype(BF16)
            ap[h] = (alpha, p)

        def accumulate(h):
            alpha, p = ap.pop(h)
            pair, half = divmod(h, HEADS_PER_BLOCK)
            vt = vt_ref[0, pair, half * HEAD_DIM:(half + 1) * HEAD_DIM, pl.ds(start, width)]
            first = lax.broadcasted_iota(jnp.int32, (PV_ROWS - HEAD_DIM, width), 0) == 0
            ones = jnp.where(first, 1.0, 0.0).astype(BF16)
            pv = _dot(jnp.concatenate([vt, ones], axis=0), p)
            acc_ref[h] = acc_ref[h] * alpha[0:1, :] + pv

        def step(t):
            probs(t)
            accumulate(t)

        return [lambda t=t: step(t) for t in range(N_HEADS)]

    def run(*stages):
        for i in range(max(len(st) for st in stages)):
            for st in stages:
                if i < len(st):
                    st[i]()

    def attend(pos):
        buffers = ((s0_ref, mx0_ref), (s1_ref, mx1_ref))
        for n_visible in range(n_steps):
            @pl.when(n_main == n_visible)
            def _():
                blocks = [(k * tk, tk, False) for k in range(n_visible)]
                blocks.append((n_visible * tk, (pos + 1) * tq, True))
                stage = lambda i: buffers[i % 2]
                run(scores(*blocks[0][:2], *stage(0), blocks[0][2]))
                for i in range(1, len(blocks)):
                    run(scores(*blocks[i][:2], *stage(i), blocks[i][2]),
                        softmax_pv(*blocks[i - 1][:2], *stage(i - 1)))
                run(softmax_pv(*blocks[-1][:2], *stage(len(blocks) - 1)))

    def gated_heads(pos):
        rows = slice(pos * tq, (pos + 1) * tq)
        gated = []
        for pair in range(N_HEADS // HEADS_PER_BLOCK):
            outs = []
            for half in range(HEADS_PER_BLOCK):
                acc = acc_ref[pair * HEADS_PER_BLOCK + half]
                outs.append(acc[:HEAD_DIM, :] / acc[HEAD_DIM:HEAD_DIM + 1, :])
            att = jnp.concatenate(outs, axis=0).T
            sl = slice(pair * LANES, (pair + 1) * LANES)
            gated.append((att * ga_ref[0, rows, sl].astype(F32)).astype(BF16))
        return jnp.concatenate(gated, axis=1)

    def project(pos, ya):
        rows = slice(pos * tq, (pos + 1) * tq)
        return _dot(jnp.concatenate([ya, yp_ref[0, rows, :]], axis=1), wob_ref[...]) + bo_ref[...]

    def residual_norm(pos, y):
        rows = slice(pos * tq, (pos + 1) * tq)
        hres = DEEPNORM_ALPHA * x_ref[0, rows, :] + ada_ref[2, pl.ds(bi, 1), :] * y
        mu = jnp.mean(hres, axis=-1, keepdims=True)
        d = hres - mu
        var = jnp.mean(d * d, axis=-1, keepdims=True)
        o_ref[0, rows, :] = d * lax.rsqrt(var + LN_EPS) * lg_ref[...] + lb_ref[...]

    for pos in range(tk // tq):
        begin_tile(pos)
        attend(pos)
        residual_norm(pos, project(pos, gated_heads(pos)))


def _attn_call(qt, k, vt, g, ga, yp, x, ada, w_out, b_out, ln_g, ln_b):
    B, n_pairs, _, S = qt.shape
    D = x.shape[2]
    assert KV_TILE % Q_TILE == 0 and S % KV_TILE == 0
    rows = KV_TILE
    tile = lambda width: pl.BlockSpec((1, rows, width), lambda b, i: (b, i, 0))
    whole = lambda width: pl.BlockSpec((1, S, width), lambda b, i: (b, 0, 0))
    const2 = lambda a: pl.BlockSpec(a.shape, lambda b, i: (0, 0))
    score_buf = pltpu.VMEM((N_HEADS, KV_TILE, Q_TILE), F32)
    stat_buf = pltpu.VMEM((N_HEADS, SUBLANES, Q_TILE), F32)
    return pl.pallas_call(
        _attn_kernel,
        name="attention",
        grid=(B, S // rows),
        in_specs=[pl.BlockSpec((1, n_pairs, LANES, rows), lambda b, i: (b, 0, 0, i)),
                  whole(D_ATT),
                  pl.BlockSpec((1, n_pairs, LANES, S), lambda b, i: (b, 0, 0, 0)),
                  whole(LANES), tile(D_ATT), tile(D_POOL), tile(D),
                  pl.BlockSpec(ada.shape, lambda b, i: (0, 0, 0)),
                  _resident(w_out.shape, lambda b, i: (0, 0)),
                  const2(b_out), const2(ln_g), const2(ln_b)],
        out_specs=tile(D),
        out_shape=jax.ShapeDtypeStruct((B, S, D), F32),
        scratch_shapes=[stat_buf, pltpu.VMEM((N_HEADS, PV_ROWS, Q_TILE), F32),
                        score_buf, score_buf, stat_buf, stat_buf,
                        pltpu.VMEM((N_HEADS, 2 * LANES, Q_TILE), BF16),
                        pltpu.VMEM(w_out.shape, BF16)],
        compiler_params=pltpu.CompilerParams(
            dimension_semantics=("arbitrary", "arbitrary"),
            vmem_limit_bytes=VMEM_LIMIT_BYTES),
    )(qt, k, vt, g, ga, yp, x, ada, w_out, b_out, ln_g, ln_b)


def _layer(x, c, w_ada, b_ada, w_in, b_in, w_pool_mix, b_pool_mix, pool_scale,
           w_out, b_out, ln_g, ln_b):
    ada = _ada_call(c, w_ada, b_ada[None, :])
    qt, k, vt, g, ga, yp = _in_call(x, ada, w_in.T, b_in[None, :], w_pool_mix, b_pool_mix,
                                    pool_scale[None, :])
    return _attn_call(qt, k, vt, g, ga, yp, x, ada, w_out, b_out[None, :], ln_g[None, :],
                      ln_b[None, :])


def kernel(x, c, w_ada, b_ada, w_in, b_in, w_pool_mix, b_pool_mix, pool_scale, w_out, b_out,
           ln_g, ln_b):
    for layer in range(w_ada.shape[0]):
        x = _layer(x, c, w_ada[layer], b_ada[layer], w_in[layer], b_in[layer],
                   w_pool_mix[layer], b_pool_mix[layer], pool_scale[layer],
                   w_out[layer], b_out[layer], ln_g[layer], ln_b[layer])
    return x
```

```python
import math

import jax
import jax.numpy as jnp
from jax import lax
from jax.experimental import pallas as pl
from jax.experimental.pallas import tpu as pltpu

D_MODEL = 1024
D_ATT = 512
D_POOL = 512
N_HEADS = 8
HEAD_DIM = 64
POOL_WINDOWS = (2, 4, 8, 16)
POOL_GROUP_DIM = 128
POOL_HISTORY = 16
LN_EPS = 1e-5
DEEPNORM_ALPHA = 2.0 ** 0.25
LOG2E = math.log2(math.e)
NEG_BIG = -1e30

LANES = 128
SUBLANES = 8
BF16_SUBLANES = 16
HEADS_PER_BLOCK = LANES // HEAD_DIM
G_TERMS = 3
CUM_ROWS = 128
PV_ROWS = HEAD_DIM + BF16_SUBLANES

IN_SOURCE = (("q", D_ATT), ("k", D_ATT), ("v", D_ATT), ("f", N_HEADS), ("p", D_POOL),
             ("ga", D_ATT), ("gp", D_POOL))
IN_WIDTH = {"f": LANES, "p": D_POOL, "gp": D_POOL, "q": D_ATT, "v": D_ATT, "k": D_ATT,
            "ga": D_ATT}
IN_OFFSET = dict(zip(IN_WIDTH, (sum(list(IN_WIDTH.values())[:i]) for i in range(len(IN_WIDTH)))))
IN_COLS = sum(IN_WIDTH.values())

SEQ_TILE = 512
Q_TILE = 256
KV_TILE = 512
VMEM_LIMIT_BYTES = 56 * 1024 * 1024

F32 = jnp.float32
BF16 = jnp.bfloat16


def _silu(x):
    return x * jax.nn.sigmoid(x)


def _dot(a, b):
    return jnp.dot(a, b, preferred_element_type=F32)


def _resident(shape, index_map):
    return pl.BlockSpec(shape, index_map, pipeline_mode=pl.Buffered(1))


def _ada_kernel(c_ref, w_ref, b_ref, o_ref, cpad_ref):
    n_rows = c_ref.shape[0]
    cpad_ref[...] = jnp.zeros_like(cpad_ref)
    cpad_ref[0:n_rows, :] = c_ref[...]
    sc = _silu(cpad_ref[...].T)
    for b in range(n_rows):
        o_ref[0, b:b + 1, :] = (jnp.sum(w_ref[...] * sc[:, b:b + 1], axis=0, keepdims=True)
                                + b_ref[...])


def _ada_call(c, w_ada, b_ada):
    n_rows, d = c.shape
    n_chunks = w_ada.shape[1] // d
    return pl.pallas_call(
        _ada_kernel,
        name="ada_vector",
        grid=(n_chunks,),
        in_specs=[
            pl.BlockSpec((n_rows, d), lambda j: (0, 0)),
            pl.BlockSpec((d, d), lambda j: (0, j)),
            pl.BlockSpec((1, d), lambda j: (0, j)),
        ],
        out_specs=pl.BlockSpec((1, n_rows, d), lambda j: (j, 0, 0)),
        out_shape=jax.ShapeDtypeStruct((n_chunks, n_rows, d), F32),
        scratch_shapes=[pltpu.VMEM((LANES, d), F32)],
        compiler_params=pltpu.CompilerParams(dimension_semantics=("parallel",),
                                             vmem_limit_bytes=VMEM_LIMIT_BYTES),
    )(c, w_ada, b_ada)


def _split3(x):
    hi = x.astype(BF16).astype(F32)
    r = x - hi
    mid = r.astype(BF16).astype(F32)
    lo = (r - mid).astype(BF16).astype(F32)
    return hi, mid, lo


def _in_kernel(x_ref, ada_ref, wt_ref, b_ref, wpm_ref, bpm_ref, ps_ref,
               q_ref, k_ref, v_ref, g_ref, ga_ref, yp_ref,
               fcarry_ref, pcarry_ref, wcat_ref, bcat_ref, wmix_ref):
    bi = pl.program_id(0)
    si = pl.program_id(1)
    ts = x_ref.shape[1]

    @pl.when((bi == 0) & (si == 0))
    def _():
        src = 0
        for name, width in IN_SOURCE:
            dst, wide = IN_OFFSET[name], IN_WIDTH[name]
            w_seg = wt_ref[src:src + wide, :].T
            b_seg = b_ref[:, src:src + wide]
            if width < wide:
                w_seg = jnp.where(lax.broadcasted_iota(jnp.int32, w_seg.shape, 1) < width,
                                  w_seg, 0.0)
                b_seg = jnp.where(lax.broadcasted_iota(jnp.int32, b_seg.shape, 1) < width,
                                  b_seg, 0.0)
            wcat_ref[:, dst:dst + wide] = w_seg.astype(BF16)
            bcat_ref[:, dst:dst + wide] = b_seg
            src += width
        gd = POOL_GROUP_DIM
        wmix_ref[...] = jnp.zeros_like(wmix_ref)
        for g in range(len(POOL_WINDOWS)):
            o = (g % 2) * gd
            wmix_ref[g // 2, o:o + gd, o:o + gd] = wpm_ref[g].astype(BF16)

    @pl.when(si == 0)
    def _():
        fcarry_ref[...] = jnp.zeros_like(fcarry_ref)
        pcarry_ref[...] = jnp.zeros_like(pcarry_ref)

    shift = ada_ref[0, pl.ds(bi, 1), :]
    scale = ada_ref[1, pl.ds(bi, 1), :]
    u = (x_ref[0] * (1.0 + scale) + shift).astype(BF16)
    proj = _dot(u, wcat_ref[...]) + bcat_ref[...]
    seg = lambda name: proj[:, IN_OFFSET[name]:IN_OFFSET[name] + IN_WIDTH[name]]

    q = seg("q") * (HEAD_DIM ** -0.5 * LOG2E)
    for blk in range(D_ATT // LANES):
        q_ref[0, blk] = q[:, blk * LANES:(blk + 1) * LANES].T.astype(BF16)

    k_ref[0] = seg("k").astype(BF16)

    v = seg("v")
    for blk in range(D_ATT // LANES):
        v_ref[0, blk] = v[:, blk * LANES:(blk + 1) * LANES].T.astype(BF16)

    fl = seg("f")
    logf = jnp.minimum(fl, 0.0) - jnp.log1p(jnp.exp(-jnp.abs(fl)))
    row = lax.broadcasted_iota(jnp.int32, (CUM_ROWS, CUM_ROWS), 0)
    col = lax.broadcasted_iota(jnp.int32, (CUM_ROWS, CUM_ROWS), 1)
    tri = jnp.where(row >= col, 1.0, 0.0).astype(BF16)
    terms = jnp.concatenate([t.astype(BF16) for t in _split3(logf)], axis=1)
    offset = fcarry_ref[0:1, :]
    cum_blocks = []
    for r in range(ts // CUM_ROWS):
        part = _dot(tri, terms[r * CUM_ROWS:(r + 1) * CUM_ROWS, :])
        local = part[:, :LANES] + part[:, LANES:2 * LANES] + part[:, 2 * LANES:]
        cum_blocks.append(local + offset)
        offset = cum_blocks[-1][CUM_ROWS - 1:CUM_ROWS, :]
    fcarry_ref[...] = jnp.broadcast_to(offset, fcarry_ref.shape)
    cum = jnp.concatenate(cum_blocks, axis=0)

    g_hi, g_mid, g_lo = _split3(cum * (-LOG2E))
    lane = lax.broadcasted_iota(jnp.int32, (ts, LANES), 1)
    g = jnp.where(lane < N_HEADS, g_hi,
                  jnp.where(lane < 2 * N_HEADS, pltpu.roll(g_mid, N_HEADS, axis=1),
                            jnp.where(lane < 3 * N_HEADS, pltpu.roll(g_lo, 2 * N_HEADS, axis=1),
                                      0.0)))
    g_ref[0] = g.astype(BF16)

    p = seg("p")
    pe = jnp.concatenate([pcarry_ref[...], p], axis=0)
    pcarry_ref[...] = p[ts - POOL_HISTORY:, :]
    gp = seg("gp")
    t_glob = si * ts + lax.broadcasted_iota(jnp.int32, (ts, POOL_GROUP_DIM), 0)
    pooled = []
    for g, w in enumerate(POOL_WINDOWS):
        sl = slice(g * POOL_GROUP_DIM, (g + 1) * POOL_GROUP_DIM)
        y = pe[:, sl]
        sh = 1
        while sh < w:
            y = y + pltpu.roll(y, sh, axis=0)
            sh *= 2
        cnt = jnp.minimum(t_glob + 1, w).astype(F32)
        pooled.append((y[POOL_HISTORY:, :] / cnt - p[:, sl]).astype(BF16))
    for j in range(len(POOL_WINDOWS) // 2):
        sl = slice(2 * j * POOL_GROUP_DIM, (2 * j + 2) * POOL_GROUP_DIM)
        bias = jnp.concatenate([bpm_ref[2 * j:2 * j + 1, :], bpm_ref[2 * j + 1:2 * j + 2, :]],
                               axis=1)
        mixed = _dot(jnp.concatenate(pooled[2 * j:2 * j + 2], axis=1), wmix_ref[j]) + bias
        yp_ref[0, :, sl] = (mixed * ps_ref[:, sl] * _silu(gp[:, sl])).astype(BF16)

    ga_ref[0] = _silu(seg("ga")).astype(BF16)


def _in_call(x, ada, w_in_t, b_in, w_pm, b_pm, pool_scale):
    B, S, D = x.shape
    ts = SEQ_TILE
    const2 = lambda b, s: (0, 0)
    const3 = lambda b, s: (0, 0, 0)
    tile = lambda width: pl.BlockSpec((1, ts, width), lambda b, s: (b, s, 0))
    n_pairs = D_ATT // LANES
    tile_t = pl.BlockSpec((1, n_pairs, LANES, ts), lambda b, s: (b, 0, 0, s))
    return pl.pallas_call(
        _in_kernel,
        name="input_stage",
        grid=(B, S // ts),
        in_specs=[tile(D), pl.BlockSpec(ada.shape, const3),
                  _resident(w_in_t.shape, const2), pl.BlockSpec(b_in.shape, const2),
                  pl.BlockSpec(w_pm.shape, const3), pl.BlockSpec(b_pm.shape, const2),
                  pl.BlockSpec(pool_scale.shape, const2)],
        out_specs=[tile_t, tile(D_ATT), tile_t, tile(LANES),
                   tile(D_ATT), tile(D_POOL)],
        out_shape=[jax.ShapeDtypeStruct((B, n_pairs, LANES, S), BF16),
                   jax.ShapeDtypeStruct((B, S, D_ATT), BF16),
                   jax.ShapeDtypeStruct((B, n_pairs, LANES, S), BF16),
                   jax.ShapeDtypeStruct((B, S, LANES), BF16),
                   jax.ShapeDtypeStruct((B, S, D_ATT), BF16),
                   jax.ShapeDtypeStruct((B, S, D_POOL), BF16)],
        scratch_shapes=[pltpu.VMEM((SUBLANES, LANES), F32),
                        pltpu.VMEM((POOL_HISTORY, D_POOL), F32),
                        pltpu.VMEM((D, IN_COLS), BF16),
                        pltpu.VMEM((1, IN_COLS), F32),
                        pltpu.VMEM((len(POOL_WINDOWS) // 2, 2 * POOL_GROUP_DIM,
                                    2 * POOL_GROUP_DIM), BF16)],
        compiler_params=pltpu.CompilerParams(
            dimension_semantics=("arbitrary", "arbitrary"),
            vmem_limit_bytes=VMEM_LIMIT_BYTES),
    )(x, ada, w_in_t, b_in, w_pm, b_pm, pool_scale)


def _attn_kernel(qt_ref, k_ref, vt_ref, g_ref, ga_ref, yp_ref, x_ref, ada_ref,
                 wo_ref, bo_ref, lg_ref, lb_ref, o_ref,
                 m_ref, acc_ref, s0_ref, s1_ref, mx0_ref, mx1_ref, rhs_ref, wob_ref):
    bi = pl.program_id(0)
    n_main = pl.program_id(1)
    tq = Q_TILE
    tk = KV_TILE
    diag_start = pl.multiple_of(n_main * tk, tk)

    row = lax.broadcasted_iota(jnp.int32, (LANES, tq), 0)

    @pl.when((bi == 0) & (n_main == 0))
    def _():
        wob_ref[...] = wo_ref[...].astype(BF16)
        for h in range(N_HEADS):
            pick = (row < G_TERMS * N_HEADS) & (row % N_HEADS == h)
            rhs_ref[h, LANES:, :] = jnp.where(pick, 1.0, 0.0).astype(BF16)

    def begin_tile(pos):
        m_ref[...] = jnp.full_like(m_ref, NEG_BIG)
        acc_ref[...] = jnp.zeros_like(acc_ref)
        for h in range(N_HEADS):
            pair, half = divmod(h, HEADS_PER_BLOCK)
            own = (row // HEAD_DIM) == half
            qt = qt_ref[0, pair, :, pos * tq:(pos + 1) * tq]
            rhs_ref[h, :LANES, :] = jnp.where(own, qt, jnp.zeros((), BF16))

    def scores(start, width, s_ref, mx_ref, diagonal=False):
        def head(h):
            pair = h // HEADS_PER_BLOCK
            kblk = k_ref[0, pl.ds(start, width), pair * LANES:(pair + 1) * LANES]
            gblk = g_ref[0, pl.ds(start, width), :]
            lhs = jnp.concatenate([kblk, gblk], axis=1)
            s = _dot(lhs, rhs_ref[h])
            if diagonal:
                key = lax.broadcasted_iota(jnp.int32, (tq, tq), 0)
                qry = lax.broadcasted_iota(jnp.int32, (tq, tq), 1)
                own = jnp.where(key <= qry, s[width - tq:, :], NEG_BIG)
                s = own if width == tq else jnp.concatenate([s[:width - tq, :], own], axis=0)
            s_ref[h, :width, :] = s
            mx_ref[h] = jnp.max(s.reshape(width // SUBLANES, SUBLANES, tq), axis=0)

        return [lambda h=h: head(h) for h in range(N_HEADS)]

    def softmax_pv(start, width, s_ref, mx_ref):
        ap = {}

        def probs(h):
            m_prev = m_ref[h]
            m_new = jnp.maximum(m_prev, jnp.max(mx_ref[h], axis=0, keepdims=True))
            m_ref[h] = m_new
            alpha = jnp.exp2(m_prev - m_new)
            p = jnp.exp2(s_ref[h, :width, :] - m_new[0:1, :]).astype(BF16)
            ap[h] = (alpha, p)

        def accumulate(h):
            alpha, p = ap.pop(h)
            pair, half = divmod(h, HEADS_PER_BLOCK)
            vt = vt_ref[0, pair, half * HEAD_DIM:(half + 1) * HEAD_DIM, pl.ds(start, width)]
            first = lax.broadcasted_iota(jnp.int32, (PV_ROWS - HEAD_DIM, width), 0) == 0
            ones = jnp.where(first, 1.0, 0.0).astype(BF16)
            pv = _dot(jnp.concatenate([vt, ones], axis=0), p)
            acc_ref[h] = acc_ref[h] * alpha[0:1, :] + pv

        def step(t):
            probs(t)
            accumulate(t)

        return [lambda t=t: step(t) for t in range(N_HEADS)]

    def run(*stages):
        for i in range(max(len(st) for st in stages)):
            for st in stages:
                if i < len(st):
                    st[i]()

    main = lambda k: pl.multiple_of(k * tk, tk)

    def first_scores():
        run(scores(main(0), tk, s0_ref, mx0_ref))

    def attend(pos):
        width = (pos + 1) * tq
        diag_scores = lambda s_ref, mx_ref: scores(diag_start, width, s_ref, mx_ref, True)
        diag_softmax = lambda s_ref, mx_ref: softmax_pv(diag_start, width, s_ref, mx_ref)
        n_pairs = jnp.maximum(n_main - 1, 0) // 2

        def pair_of_blocks(k):
            run(scores(main(k + 1), tk, s1_ref, mx1_ref),
                softmax_pv(main(k), tk, s0_ref, mx0_ref))
            run(scores(main(k + 2), tk, s0_ref, mx0_ref),
                softmax_pv(main(k + 1), tk, s1_ref, mx1_ref))

        def two_pairs(i, carry):
            pair_of_blocks(4 * i)
            pair_of_blocks(4 * i + 2)
            return carry

        lax.fori_loop(0, n_pairs // 2, two_pairs, 0)

        @pl.when(n_pairs % 2 == 1)
        def _():
            pair_of_blocks(2 * (n_pairs - 1))

        k_last = 2 * n_pairs
        left = n_main - k_last

        @pl.when(left == 0)
        def _():
            run(diag_scores(s0_ref, mx0_ref))
            run(diag_softmax(s0_ref, mx0_ref))

        @pl.when(left == 1)
        def _():
            run(diag_scores(s1_ref, mx1_ref), softmax_pv(main(k_last), tk, s0_ref, mx0_ref))
            run(diag_softmax(s1_ref, mx1_ref))

        @pl.when(left == 2)
        def _():
            run(scores(main(k_last + 1), tk, s1_ref, mx1_ref),
                softmax_pv(main(k_last), tk, s0_ref, mx0_ref))
            run(diag_scores(s0_ref, mx0_ref), softmax_pv(main(k_last + 1), tk, s1_ref, mx1_ref))
            run(diag_softmax(s0_ref, mx0_ref))

    def gated_heads(pos):
        rows = slice(pos * tq, (pos + 1) * tq)
        gated = []
        for pair in range(N_HEADS // HEADS_PER_BLOCK):
            outs = []
            for half in range(HEADS_PER_BLOCK):
                acc = acc_ref[pair * HEADS_PER_BLOCK + half]
                outs.append(acc[:HEAD_DIM, :] / acc[HEAD_DIM:HEAD_DIM + 1, :])
            att = jnp.concatenate(outs, axis=0).T
            sl = slice(pair * LANES, (pair + 1) * LANES)
            gated.append((att * ga_ref[0, rows, sl].astype(F32)).astype(BF16))
        return jnp.concatenate(gated, axis=1)

    def project(pos, ya):
        rows = slice(pos * tq, (pos + 1) * tq)
        return _dot(jnp.concatenate([ya, yp_ref[0, rows, :]], axis=1), wob_ref[...]) + bo_ref[...]

    def residual_norm(pos, y):
        rows = slice(pos * tq, (pos + 1) * tq)
        hres = DEEPNORM_ALPHA * x_ref[0, rows, :] + ada_ref[2, pl.ds(bi, 1), :] * y
        mu = jnp.mean(hres, axis=-1, keepdims=True)
        d = hres - mu
        var = jnp.mean(d * d, axis=-1, keepdims=True)
        o_ref[0, rows, :] = d * lax.rsqrt(var + LN_EPS) * lg_ref[...] + lb_ref[...]

    n_pos = tk // tq
    begin_tile(0)

    @pl.when(n_main > 0)
    def _():
        first_scores()

    for pos in range(n_pos):
        attend(pos)
        ya = gated_heads(pos)
        if pos + 1 < n_pos:
            @pl.when(n_main > 0)
            def _():
                y = project(pos, ya)
                begin_tile(pos + 1)
                first_scores()
                residual_norm(pos, y)

            @pl.when(n_main == 0)
            def _():
                residual_norm(pos, project(pos, ya))
                begin_tile(pos + 1)
        else:
            residual_norm(pos, project(pos, ya))


def _attn_call(qt, k, vt, g, ga, yp, x, ada, w_out, b_out, ln_g, ln_b):
    B, n_pairs, _, S = qt.shape
    D = x.shape[2]
    assert KV_TILE % Q_TILE == 0 and S % KV_TILE == 0
    rows = KV_TILE
    tile = lambda width: pl.BlockSpec((1, rows, width), lambda b, i: (b, i, 0))
    whole = lambda width: pl.BlockSpec((1, S, width), lambda b, i: (b, 0, 0))
    const2 = lambda a: pl.BlockSpec(a.shape, lambda b, i: (0, 0))
    score_buf = pltpu.VMEM((N_HEADS, KV_TILE, Q_TILE), F32)
    stat_buf = pltpu.VMEM((N_HEADS, SUBLANES, Q_TILE), F32)
    return pl.pallas_call(
        _attn_kernel,
        name="attention",
        grid=(B, S // rows),
        in_specs=[pl.BlockSpec((1, n_pairs, LANES, rows), lambda b, i: (b, 0, 0, i)),
                  whole(D_ATT),
                  pl.BlockSpec((1, n_pairs, LANES, S), lambda b, i: (b, 0, 0, 0)),
                  whole(LANES), tile(D_ATT), tile(D_POOL), tile(D),
                  pl.BlockSpec(ada.shape, lambda b, i: (0, 0, 0)),
                  _resident(w_out.shape, lambda b, i: (0, 0)),
                  const2(b_out), const2(ln_g), const2(ln_b)],
        out_specs=tile(D),
        out_shape=jax.ShapeDtypeStruct((B, S, D), F32),
        scratch_shapes=[stat_buf, pltpu.VMEM((N_HEADS, PV_ROWS, Q_TILE), F32),
                        score_buf, score_buf, stat_buf, stat_buf,
                        pltpu.VMEM((N_HEADS, 2 * LANES, Q_TILE), BF16),
                        pltpu.VMEM(w_out.shape, BF16)],
        compiler_params=pltpu.CompilerParams(
            dimension_semantics=("arbitrary", "arbitrary"),
            vmem_limit_bytes=VMEM_LIMIT_BYTES),
    )(qt, k, vt, g, ga, yp, x, ada, w_out, b_out, ln_g, ln_b)


def _layer(x, c, w_ada, b_ada, w_in, b_in, w_pool_mix, b_pool_mix, pool_scale,
           w_out, b_out, ln_g, ln_b):
    ada = _ada_call(c, w_ada, b_ada[None, :])
    qt, k, vt, g, ga, yp = _in_call(x, ada, w_in.T, b_in[None, :], w_pool_mix, b_pool_mix,
                                    pool_scale[None, :])
    return _attn_call(qt, k, vt, g, ga, yp, x, ada, w_out, b_out[None, :], ln_g[None, :],
                      ln_b[None, :])


def kernel(x, c, w_ada, b_ada, w_in, b_in, w_pool_mix, b_pool_mix, pool_scale, w_out, b_out,
           ln_g, ln_b):
    for layer in range(w_ada.shape[0]):
        x = _layer(x, c, w_ada[layer], b_ada[layer], w_in[layer], b_in[layer],
                   w_pool_mix[layer], b_pool_mix[layer], pool_scale[layer],
                   w_out[layer], b_out[layer], ln_g[layer], ln_b[layer])
    return x
```

```python
import math

import jax
import jax.numpy as jnp
from jax import lax
from jax.experimental import pallas as pl
from jax.experimental.pallas import tpu as pltpu

D_MODEL = 1024
D_ATT = 512
D_POOL = 512
N_HEADS = 8
HEAD_DIM = 64
POOL_WINDOWS = (2, 4, 8, 16)
POOL_GROUP_DIM = 128
POOL_HISTORY = 16
LN_EPS = 1e-5
DEEPNORM_ALPHA = 2.0 ** 0.25
LOG2E = math.log2(math.e)
NEG_BIG = -1e30

LANES = 128
SUBLANES = 8
BF16_SUBLANES = 16
HEADS_PER_BLOCK = LANES // HEAD_DIM
G_TERMS = 3
CUM_ROWS = 128
PV_ROWS = HEAD_DIM + BF16_SUBLANES

IN_SOURCE = (("q", D_ATT), ("k", D_ATT), ("v", D_ATT), ("f", N_HEADS), ("p", D_POOL),
             ("ga", D_ATT), ("gp", D_POOL))
IN_WIDTH = {"f": LANES, "p": D_POOL, "gp": D_POOL, "q": D_ATT, "v": D_ATT, "k": D_ATT,
            "ga": D_ATT}
IN_OFFSET = dict(zip(IN_WIDTH, (sum(list(IN_WIDTH.values())[:i]) for i in range(len(IN_WIDTH)))))
IN_COLS = sum(IN_WIDTH.values())

SEQ_TILE = 512
Q_TILE = 256
KV_TILE = 512
VMEM_LIMIT_BYTES = 56 * 1024 * 1024

F32 = jnp.float32
BF16 = jnp.bfloat16


def _silu(x):
    return x * jax.nn.sigmoid(x)


def _dot(a, b):
    return jnp.dot(a, b, preferred_element_type=F32)


def _resident(shape, index_map):
    return pl.BlockSpec(shape, index_map, pipeline_mode=pl.Buffered(1))


def _ada_kernel(c_ref, w_ref, b_ref, o_ref, cpad_ref):
    n_rows = c_ref.shape[0]
    cpad_ref[...] = jnp.zeros_like(cpad_ref)
    cpad_ref[0:n_rows, :] = c_ref[...]
    sc = _silu(cpad_ref[...].T)
    for b in range(n_rows):
        o_ref[0, b:b + 1, :] = (jnp.sum(w_ref[...] * sc[:, b:b + 1], axis=0, keepdims=True)
                                + b_ref[...])


def _ada_call(c, w_ada, b_ada):
    n_rows, d = c.shape
    n_chunks = w_ada.shape[1] // d
    return pl.pallas_call(
        _ada_kernel,
        name="ada_vector",
        grid=(n_chunks,),
        in_specs=[
            pl.BlockSpec((n_rows, d), lambda j: (0, 0)),
            pl.BlockSpec((d, d), lambda j: (0, j)),
            pl.BlockSpec((1, d), lambda j: (0, j)),
        ],
        out_specs=pl.BlockSpec((1, n_rows, d), lambda j: (j, 0, 0)),
        out_shape=jax.ShapeDtypeStruct((n_chunks, n_rows, d), F32),
        scratch_shapes=[pltpu.VMEM((LANES, d), F32)],
        compiler_params=pltpu.CompilerParams(dimension_semantics=("parallel",),
                                             vmem_limit_bytes=VMEM_LIMIT_BYTES),
    )(c, w_ada, b_ada)


def _split3(x):
    hi = x.astype(BF16).astype(F32)
    r = x - hi
    mid = r.astype(BF16).astype(F32)
    lo = (r - mid).astype(BF16).astype(F32)
    return hi, mid, lo


def _in_kernel(x_ref, ada_ref, wt_ref, b_ref, wpm_ref, bpm_ref, ps_ref,
               q_ref, k_ref, v_ref, g_ref, ga_ref, yp_ref,
               fcarry_ref, pcarry_ref, wcat_ref, bcat_ref, wmix_ref):
    bi = pl.program_id(0)
    si = pl.program_id(1)
    ts = x_ref.shape[1]

    @pl.when((bi == 0) & (si == 0))
    def _():
        src = 0
        for name, width in IN_SOURCE:
            dst, wide = IN_OFFSET[name], IN_WIDTH[name]
            w_seg = wt_ref[src:src + wide, :].T
            b_seg = b_ref[:, src:src + wide]
            if width < wide:
                w_seg = jnp.where(lax.broadcasted_iota(jnp.int32, w_seg.shape, 1) < width,
                                  w_seg, 0.0)
                b_seg = jnp.where(lax.broadcasted_iota(jnp.int32, b_seg.shape, 1) < width,
                                  b_seg, 0.0)
            wcat_ref[:, dst:dst + wide] = w_seg.astype(BF16)
            bcat_ref[:, dst:dst + wide] = b_seg
            src += width
        gd = POOL_GROUP_DIM
        wmix_ref[...] = jnp.zeros_like(wmix_ref)
        for g in range(len(POOL_WINDOWS)):
            o = (g % 2) * gd
            wmix_ref[g // 2, o:o + gd, o:o + gd] = wpm_ref[g].astype(BF16)

    @pl.when(si == 0)
    def _():
        fcarry_ref[...] = jnp.zeros_like(fcarry_ref)
        pcarry_ref[...] = jnp.zeros_like(pcarry_ref)

    shift = ada_ref[0, pl.ds(bi, 1), :]
    scale = ada_ref[1, pl.ds(bi, 1), :]
    u = (x_ref[0] * (1.0 + scale) + shift).astype(BF16)
    proj = _dot(u, wcat_ref[...]) + bcat_ref[...]
    seg = lambda name: proj[:, IN_OFFSET[name]:IN_OFFSET[name] + IN_WIDTH[name]]

    q = seg("q") * (HEAD_DIM ** -0.5 * LOG2E)
    for blk in range(D_ATT // LANES):
        q_ref[0, blk] = q[:, blk * LANES:(blk + 1) * LANES].T.astype(BF16)

    k_ref[0] = seg("k").astype(BF16)

    v = seg("v")
    for blk in range(D_ATT // LANES):
        v_ref[0, blk] = v[:, blk * LANES:(blk + 1) * LANES].T.astype(BF16)

    fl = seg("f")
    logf = jnp.minimum(fl, 0.0) - jnp.log1p(jnp.exp(-jnp.abs(fl)))
    row = lax.broadcasted_iota(jnp.int32, (CUM_ROWS, CUM_ROWS), 0)
    col = lax.broadcasted_iota(jnp.int32, (CUM_ROWS, CUM_ROWS), 1)
    tri = jnp.where(row >= col, 1.0, 0.0).astype(BF16)
    terms = jnp.concatenate([t.astype(BF16) for t in _split3(logf)], axis=1)
    offset = fcarry_ref[0:1, :]
    cum_blocks = []
    for r in range(ts // CUM_ROWS):
        part = _dot(tri, terms[r * CUM_ROWS:(r + 1) * CUM_ROWS, :])
        local = part[:, :LANES] + part[:, LANES:2 * LANES] + part[:, 2 * LANES:]
        cum_blocks.append(local + offset)
        offset = cum_blocks[-1][CUM_ROWS - 1:CUM_ROWS, :]
    fcarry_ref[...] = jnp.broadcast_to(offset, fcarry_ref.shape)
    cum = jnp.concatenate(cum_blocks, axis=0)

    g_hi, g_mid, g_lo = _split3(cum * (-LOG2E))
    lane = lax.broadcasted_iota(jnp.int32, (ts, LANES), 1)
    g = jnp.where(lane < N_HEADS, g_hi,
                  jnp.where(lane < 2 * N_HEADS, pltpu.roll(g_mid, N_HEADS, axis=1),
                            jnp.where(lane < 3 * N_HEADS, pltpu.roll(g_lo, 2 * N_HEADS, axis=1),
                                      0.0)))
    g_ref[0] = g.astype(BF16)

    p = seg("p")
    pe = jnp.concatenate([pcarry_ref[...], p], axis=0)
    pcarry_ref[...] = p[ts - POOL_HISTORY:, :]
    gp = seg("gp")
    t_glob = si * ts + lax.broadcasted_iota(jnp.int32, (ts, POOL_GROUP_DIM), 0)
    pooled = []
    for g, w in enumerate(POOL_WINDOWS):
        sl = slice(g * POOL_GROUP_DIM, (g + 1) * POOL_GROUP_DIM)
        y = pe[:, sl]
        sh = 1
        while sh < w:
            y = y + pltpu.roll(y, sh, axis=0)
            sh *= 2
        cnt = jnp.minimum(t_glob + 1, w).astype(F32)
        pooled.append((y[POOL_HISTORY:, :] / cnt - p[:, sl]).astype(BF16))
    for j in range(len(POOL_WINDOWS) // 2):
        sl = slice(2 * j * POOL_GROUP_DIM, (2 * j + 2) * POOL_GROUP_DIM)
        bias = jnp.concatenate([bpm_ref[2 * j:2 * j + 1, :], bpm_ref[2 * j + 1:2 * j + 2, :]],
                               axis=1)
        mixed = _dot(jnp.concatenate(pooled[2 * j:2 * j + 2], axis=1), wmix_ref[j]) + bias
        yp_ref[0, :, sl] = (mixed * ps_ref[:, sl] * _silu(gp[:, sl])).astype(BF16)

    ga_ref[0] = _silu(seg("ga")).astype(BF16)


def _in_call(x, ada, w_in_t, b_in, w_pm, b_pm, pool_scale):
    B, S, D = x.shape
    ts = SEQ_TILE
    const2 = lambda b, s: (0, 0)
    const3 = lambda b, s: (0, 0, 0)
    tile = lambda width: pl.BlockSpec((1, ts, width), lambda b, s: (b, s, 0))
    n_pairs = D_ATT // LANES
    tile_t = pl.BlockSpec((1, n_pairs, LANES, ts), lambda b, s: (b, 0, 0, s))
    return pl.pallas_call(
        _in_kernel,
        name="input_stage",
        grid=(B, S // ts),
        in_specs=[tile(D), pl.BlockSpec(ada.shape, const3),
                  _resident(w_in_t.shape, const2), pl.BlockSpec(b_in.shape, const2),
                  pl.BlockSpec(w_pm.shape, const3), pl.BlockSpec(b_pm.shape, const2),
                  pl.BlockSpec(pool_scale.shape, const2)],
        out_specs=[tile_t, tile(D_ATT), tile_t, tile(LANES),
                   tile(D_ATT), tile(D_POOL)],
        out_shape=[jax.ShapeDtypeStruct((B, n_pairs, LANES, S), BF16),
                   jax.ShapeDtypeStruct((B, S, D_ATT), BF16),
                   jax.ShapeDtypeStruct((B, n_pairs, LANES, S), BF16),
                   jax.ShapeDtypeStruct((B, S, LANES), BF16),
                   jax.ShapeDtypeStruct((B, S, D_ATT), BF16),
                   jax.ShapeDtypeStruct((B, S, D_POOL), BF16)],
        scratch_shapes=[pltpu.VMEM((SUBLANES, LANES), F32),
                        pltpu.VMEM((POOL_HISTORY, D_POOL), F32),
                        pltpu.VMEM((D, IN_COLS), BF16),
                        pltpu.VMEM((1, IN_COLS), F32),
                        pltpu.VMEM((len(POOL_WINDOWS) // 2, 2 * POOL_GROUP_DIM,
                                    2 * POOL_GROUP_DIM), BF16)],
        compiler_params=pltpu.CompilerParams(
            dimension_semantics=("arbitrary", "arbitrary"),
            vmem_limit_bytes=VMEM_LIMIT_BYTES),
    )(x, ada, w_in_t, b_in, w_pm, b_pm, pool_scale)


def _attn_kernel(qt_ref, k_ref, vt_ref, g_ref, ga_ref, yp_ref, x_ref, ada_ref,
                 wo_ref, bo_ref, lg_ref, lb_ref, o_ref,
                 m_ref, acc_ref, s0_ref, s1_ref, mx0_ref, mx1_ref, rhs_ref, wob_ref):
    bi = pl.program_id(0)
    n_main = pl.program_id(1)
    tq = Q_TILE
    tk = KV_TILE
    diag_start = pl.multiple_of(n_main * tk, tk)

    row = lax.broadcasted_iota(jnp.int32, (LANES, tq), 0)

    @pl.when((bi == 0) & (n_main == 0))
    def _():
        wob_ref[...] = wo_ref[...].astype(BF16)
        for h in range(N_HEADS):
            pick = (row < G_TERMS * N_HEADS) & (row % N_HEADS == h)
            rhs_ref[h, LANES:, :] = jnp.where(pick, 1.0, 0.0).astype(BF16)

    def begin_tile(pos):
        m_ref[...] = jnp.full_like(m_ref, NEG_BIG)
        acc_ref[...] = jnp.zeros_like(acc_ref)
        for h in range(N_HEADS):
            pair, half = divmod(h, HEADS_PER_BLOCK)
            own = (row // HEAD_DIM) == half
            qt = qt_ref[0, pair, :, pos * tq:(pos + 1) * tq]
            rhs_ref[h, :LANES, :] = jnp.where(own, qt, jnp.zeros((), BF16))

    def scores(start, width, s_ref, mx_ref, diagonal=False):
        def head(h):
            pair = h // HEADS_PER_BLOCK
            kblk = k_ref[0, pl.ds(start, width), pair * LANES:(pair + 1) * LANES]
            gblk = g_ref[0, pl.ds(start, width), :]
            lhs = jnp.concatenate([kblk, gblk], axis=1)
            s = _dot(lhs, rhs_ref[h])
            if diagonal:
                key = lax.broadcasted_iota(jnp.int32, (tq, tq), 0)
                qry = lax.broadcasted_iota(jnp.int32, (tq, tq), 1)
                own = jnp.where(key <= qry, s[width - tq:, :], NEG_BIG)
                s = own if width == tq else jnp.concatenate([s[:width - tq, :], own], axis=0)
            s_ref[h, :width, :] = s
            mx_ref[h] = jnp.max(s.reshape(width // SUBLANES, SUBLANES, tq), axis=0)

        return [lambda h=h: head(h) for h in range(N_HEADS)]

    def softmax_pv(start, width, s_ref, mx_ref):
        def head(h):
            pair, half = divmod(h, HEADS_PER_BLOCK)
            m_prev = m_ref[h]
            m_new = jnp.maximum(m_prev, jnp.max(mx_ref[h], axis=0, keepdims=True))
            m_ref[h] = m_new
            alpha = jnp.exp2(m_prev - m_new)
            p = jnp.exp2(s_ref[h, :width, :] - m_new[0:1, :]).astype(BF16)
            vt = vt_ref[0, pair, half * HEAD_DIM:(half + 1) * HEAD_DIM, pl.ds(start, width)]
            first = lax.broadcasted_iota(jnp.int32, (PV_ROWS - HEAD_DIM, width), 0) == 0
            ones = jnp.where(first, 1.0, 0.0).astype(BF16)
            pv = _dot(jnp.concatenate([vt, ones], axis=0), p)
            acc_ref[h] = acc_ref[h] * alpha[0:1, :] + pv

        return [lambda h=h: head(h) for h in range(N_HEADS)]

    def run(*stages):
        for i in range(max(len(st) for st in stages)):
            for st in stages:
                if i < len(st):
                    st[i]()

    main = lambda k: pl.multiple_of(k * tk, tk)

    def first_scores():
        run(scores(main(0), tk, s0_ref, mx0_ref))

    def attend(pos):
        width = (pos + 1) * tq
        diag_scores = lambda s_ref, mx_ref: scores(diag_start, width, s_ref, mx_ref, True)
        diag_softmax = lambda s_ref, mx_ref: softmax_pv(diag_start, width, s_ref, mx_ref)
        n_pairs = jnp.maximum(n_main - 1, 0) // 2

        def pair_of_blocks(k):
            run(scores(main(k + 1), tk, s1_ref, mx1_ref),
                softmax_pv(main(k), tk, s0_ref, mx0_ref))
            run(scores(main(k + 2), tk, s0_ref, mx0_ref),
                softmax_pv(main(k + 1), tk, s1_ref, mx1_ref))

        def two_pairs(i, carry):
            pair_of_blocks(4 * i)
            pair_of_blocks(4 * i + 2)
            return carry

        lax.fori_loop(0, n_pairs // 2, two_pairs, 0)

        @pl.when(n_pairs % 2 == 1)
        def _():
            pair_of_blocks(2 * (n_pairs - 1))

        k_last = 2 * n_pairs
        left = n_main - k_last

        @pl.when(left == 0)
        def _():
            run(diag_scores(s0_ref, mx0_ref))
            run(diag_softmax(s0_ref, mx0_ref))

        @pl.when(left == 1)
        def _():
            run(diag_scores(s1_ref, mx1_ref), softmax_pv(main(k_last), tk, s0_ref, mx0_ref))
            run(diag_softmax(s1_ref, mx1_ref))

        @pl.when(left == 2)
        def _():
            run(scores(main(k_last + 1), tk, s1_ref, mx1_ref),
                softmax_pv(main(k_last), tk, s0_ref, mx0_ref))
            run(diag_scores(s0_ref, mx0_ref), softmax_pv(main(k_last + 1), tk, s1_ref, mx1_ref))
            run(diag_softmax(s0_ref, mx0_ref))

    def gated_heads(pos):
        rows = slice(pos * tq, (pos + 1) * tq)
        gated = []
        for pair in range(N_HEADS // HEADS_PER_BLOCK):
            outs = []
            for half in range(HEADS_PER_BLOCK):
                acc = acc_ref[pair * HEADS_PER_BLOCK + half]
                outs.append(acc[:HEAD_DIM, :] / acc[HEAD_DIM:HEAD_DIM + 1, :])
            att = jnp.concatenate(outs, axis=0).T
            sl = slice(pair * LANES, (pair + 1) * LANES)
            gated.append((att * ga_ref[0, rows, sl].astype(F32)).astype(BF16))
        return jnp.concatenate(gated, axis=1)

    def project(pos, ya):
        rows = slice(pos * tq, (pos + 1) * tq)
        return _dot(jnp.concatenate([ya, yp_ref[0, rows, :]], axis=1), wob_ref[...]) + bo_ref[...]

    def residual_norm(pos, y):
        rows = slice(pos * tq, (pos + 1) * tq)
        hres = DEEPNORM_ALPHA * x_ref[0, rows, :] + ada_ref[2, pl.ds(bi, 1), :] * y
        mu = jnp.mean(hres, axis=-1, keepdims=True)
        d = hres - mu
        var = jnp.mean(d * d, axis=-1, keepdims=True)
        o_ref[0, rows, :] = d * lax.rsqrt(var + LN_EPS) * lg_ref[...] + lb_ref[...]

    n_pos = tk // tq
    begin_tile(0)

    @pl.when(n_main > 0)
    def _():
        first_scores()

    for pos in range(n_pos):
        attend(pos)
        ya = gated_heads(pos)
        if pos + 1 < n_pos:
            @pl.when(n_main > 0)
            def _():
                y = project(pos, ya)
                begin_tile(pos + 1)
                first_scores()
                residual_norm(pos, y)

            @pl.when(n_main == 0)
            def _():
                residual_norm(pos, project(pos, ya))
                begin_tile(pos + 1)
        else:
            residual_norm(pos, project(pos, ya))


def _attn_call(qt, k, vt, g, ga, yp, x, ada, w_out, b_out, ln_g, ln_b):
    B, n_pairs, _, S = qt.shape
    D = x.shape[2]
    assert KV_TILE % Q_TILE == 0 and S % KV_TILE == 0
    rows = KV_TILE
    tile = lambda width: pl.BlockSpec((1, rows, width), lambda b, i: (b, i, 0))
    whole = lambda width: pl.BlockSpec((1, S, width), lambda b, i: (b, 0, 0))
    const2 = lambda a: pl.BlockSpec(a.shape, lambda b, i: (0, 0))
    score_buf = pltpu.VMEM((N_HEADS, KV_TILE, Q_TILE), F32)
    stat_buf = pltpu.VMEM((N_HEADS, SUBLANES, Q_TILE), F32)
    return pl.pallas_call(
        _attn_kernel,
        name="attention",
        grid=(B, S // rows),
        in_specs=[pl.BlockSpec((1, n_pairs, LANES, rows), lambda b, i: (b, 0, 0, i)),
                  whole(D_ATT),
                  pl.BlockSpec((1, n_pairs, LANES, S), lambda b, i: (b, 0, 0, 0)),
                  whole(LANES), tile(D_ATT), tile(D_POOL), tile(D),
                  pl.BlockSpec(ada.shape, lambda b, i: (0, 0, 0)),
                  _resident(w_out.shape, lambda b, i: (0, 0)),
                  const2(b_out), const2(ln_g), const2(ln_b)],
        out_specs=tile(D),
        out_shape=jax.ShapeDtypeStruct((B, S, D), F32),
        scratch_shapes=[stat_buf, pltpu.VMEM((N_HEADS, PV_ROWS, Q_TILE), F32),
                        score_buf, score_buf, stat_buf, stat_buf,
                        pltpu.VMEM((N_HEADS, 2 * LANES, Q_TILE), BF16),
                        pltpu.VMEM(w_out.shape, BF16)],
        compiler_params=pltpu.CompilerParams(
            dimension_semantics=("arbitrary", "arbitrary"),
            vmem_limit_bytes=VMEM_LIMIT_BYTES),
    )(qt, k, vt, g, ga, yp, x, ada, w_out, b_out, ln_g, ln_b)


def _layer(x, c, w_ada, b_ada, w_in, b_in, w_pool_mix, b_pool_mix, pool_scale,
           w_out, b_out, ln_g, ln_b):
    ada = _ada_call(c, w_ada, b_ada[None, :])
    qt, k, vt, g, ga, yp = _in_call(x, ada, w_in.T, b_in[None, :], w_pool_mix, b_pool_mix,
                                    pool_scale[None, :])
    return _attn_call(qt, k, vt, g, ga, yp, x, ada, w_out, b_out[None, :], ln_g[None, :],
                      ln_b[None, :])


def kernel(x, c, w_ada, b_ada, w_in, b_in, w_pool_mix, b_pool_mix, pool_scale, w_out, b_out,
           ln_g, ln_b):
    for layer in range(w_ada.shape[0]):
        x = _layer(x, c, w_ada[layer], b_ada[layer], w_in[layer], b_in[layer],
                   w_pool_mix[layer], b_pool_mix[layer], pool_scale[layer],
                   w_out[layer], b_out[layer], ln_g[layer], ln_b[layer])
    return x
```

```python
import math

import jax
import jax.numpy as jnp
from jax import lax
from jax.experimental import pallas as pl
from jax.experimental.pallas import tpu as pltpu

D_MODEL = 1024
D_ATT = 512
D_POOL = 512
N_HEADS = 8
HEAD_DIM = 64
POOL_WINDOWS = (2, 4, 8, 16)
POOL_GROUP_DIM = 128
POOL_HISTORY = 16
LN_EPS = 1e-5
DEEPNORM_ALPHA = 2.0 ** 0.25
LOG2E = math.log2(math.e)
NEG_BIG = -1e30

LANES = 128
SUBLANES = 8
BF16_SUBLANES = 16
HEADS_PER_BLOCK = LANES // HEAD_DIM
G_TERMS = 3
CUM_ROWS = 128
PV_ROWS = HEAD_DIM + BF16_SUBLANES

IN_SOURCE = (("q", D_ATT), ("k", D_ATT), ("v", D_ATT), ("f", N_HEADS), ("p", D_POOL),
             ("ga", D_ATT), ("gp", D_POOL))
IN_WIDTH = {"f": LANES, "p": D_POOL, "gp": D_POOL, "q": D_ATT, "v": D_ATT, "k": D_ATT,
            "ga": D_ATT}
IN_OFFSET = dict(zip(IN_WIDTH, (sum(list(IN_WIDTH.values())[:i]) for i in range(len(IN_WIDTH)))))
IN_COLS = sum(IN_WIDTH.values())

SEQ_TILE = 512
Q_TILE = 256
KV_TILE = 512
VMEM_LIMIT_BYTES = 56 * 1024 * 1024

F32 = jnp.float32
BF16 = jnp.bfloat16


def _silu(x):
    return x * jax.nn.sigmoid(x)


def _dot(a, b):
    return jnp.dot(a, b, preferred_element_type=F32)


def _resident(shape, index_map):
    return pl.BlockSpec(shape, index_map, pipeline_mode=pl.Buffered(1))


def _ada_kernel(c_ref, w_ref, b_ref, o_ref, cpad_ref):
    n_rows = c_ref.shape[0]
    cpad_ref[...] = jnp.zeros_like(cpad_ref)
    cpad_ref[0:n_rows, :] = c_ref[...]
    sc = _silu(cpad_ref[...].T)
    for b in range(n_rows):
        o_ref[0, b:b + 1, :] = (jnp.sum(w_ref[...] * sc[:, b:b + 1], axis=0, keepdims=True)
                                + b_ref[...])


def _ada_call(c, w_ada, b_ada):
    n_rows, d = c.shape
    n_chunks = w_ada.shape[1] // d
    return pl.pallas_call(
        _ada_kernel,
        name="ada_vector",
        grid=(n_chunks,),
        in_specs=[
            pl.BlockSpec((n_rows, d), lambda j: (0, 0)),
            pl.BlockSpec((d, d), lambda j: (0, j)),
            pl.BlockSpec((1, d), lambda j: (0, j)),
        ],
        out_specs=pl.BlockSpec((1, n_rows, d), lambda j: (j, 0, 0)),
        out_shape=jax.ShapeDtypeStruct((n_chunks, n_rows, d), F32),
        scratch_shapes=[pltpu.VMEM((LANES, d), F32)],
        compiler_params=pltpu.CompilerParams(dimension_semantics=("parallel",),
                                             vmem_limit_bytes=VMEM_LIMIT_BYTES),
    )(c, w_ada, b_ada)


def _split3(x):
    hi = x.astype(BF16).astype(F32)
    r = x - hi
    mid = r.astype(BF16).astype(F32)
    lo = (r - mid).astype(BF16).astype(F32)
    return hi, mid, lo


def _in_kernel(x_ref, ada_ref, wt_ref, b_ref, wpm_ref, bpm_ref, ps_ref,
               q_ref, k_ref, v_ref, g_ref, ga_ref, yp_ref,
               fcarry_ref, pcarry_ref, wcat_ref, bcat_ref, wmix_ref):
    bi = pl.program_id(0)
    si = pl.program_id(1)
    ts = x_ref.shape[1]

    @pl.when((bi == 0) & (si == 0))
    def _():
        src = 0
        for name, width in IN_SOURCE:
            dst, wide = IN_OFFSET[name], IN_WIDTH[name]
            w_seg = wt_ref[src:src + wide, :].T
            b_seg = b_ref[:, src:src + wide]
            if width < wide:
                w_seg = jnp.where(lax.broadcasted_iota(jnp.int32, w_seg.shape, 1) < width,
                                  w_seg, 0.0)
                b_seg = jnp.where(lax.broadcasted_iota(jnp.int32, b_seg.shape, 1) < width,
                                  b_seg, 0.0)
            wcat_ref[:, dst:dst + wide] = w_seg.astype(BF16)
            bcat_ref[:, dst:dst + wide] = b_seg
            src += width
        gd = POOL_GROUP_DIM
        wmix_ref[...] = jnp.zeros_like(wmix_ref)
        for g in range(len(POOL_WINDOWS)):
            o = (g % 2) * gd
            wmix_ref[g // 2, o:o + gd, o:o + gd] = wpm_ref[g].astype(BF16)

    @pl.when(si == 0)
    def _():
        fcarry_ref[...] = jnp.zeros_like(fcarry_ref)
        pcarry_ref[...] = jnp.zeros_like(pcarry_ref)

    shift = ada_ref[0, pl.ds(bi, 1), :]
    scale = ada_ref[1, pl.ds(bi, 1), :]
    u = (x_ref[0] * (1.0 + scale) + shift).astype(BF16)
    proj = _dot(u, wcat_ref[...]) + bcat_ref[...]
    seg = lambda name: proj[:, IN_OFFSET[name]:IN_OFFSET[name] + IN_WIDTH[name]]

    q = seg("q") * (HEAD_DIM ** -0.5 * LOG2E)
    for blk in range(D_ATT // LANES):
        q_ref[0, blk] = q[:, blk * LANES:(blk + 1) * LANES].T.astype(BF16)

    k_ref[0] = seg("k").astype(BF16)

    v = seg("v")
    for blk in range(D_ATT // LANES):
        v_ref[0, blk] = v[:, blk * LANES:(blk + 1) * LANES].T.astype(BF16)

    fl = seg("f")
    logf = jnp.minimum(fl, 0.0) - jnp.log1p(jnp.exp(-jnp.abs(fl)))
    row = lax.broadcasted_iota(jnp.int32, (CUM_ROWS, CUM_ROWS), 0)
    col = lax.broadcasted_iota(jnp.int32, (CUM_ROWS, CUM_ROWS), 1)
    tri = jnp.where(row >= col, 1.0, 0.0).astype(BF16)
    terms = jnp.concatenate([t.astype(BF16) for t in _split3(logf)], axis=1)
    offset = fcarry_ref[0:1, :]
    cum_blocks = []
    for r in range(ts // CUM_ROWS):
        part = _dot(tri, terms[r * CUM_ROWS:(r + 1) * CUM_ROWS, :])
        local = part[:, :LANES] + part[:, LANES:2 * LANES] + part[:, 2 * LANES:]
        cum_blocks.append(local + offset)
        offset = cum_blocks[-1][CUM_ROWS - 1:CUM_ROWS, :]
    fcarry_ref[...] = jnp.broadcast_to(offset, fcarry_ref.shape)
    cum = jnp.concatenate(cum_blocks, axis=0)

    g_hi, g_mid, g_lo = _split3(cum * (-LOG2E))
    lane = lax.broadcasted_iota(jnp.int32, (ts, LANES), 1)
    g = jnp.where(lane < N_HEADS, g_hi,
                  jnp.where(lane < 2 * N_HEADS, pltpu.roll(g_mid, N_HEADS, axis=1),
                            jnp.where(lane < 3 * N_HEADS, pltpu.roll(g_lo, 2 * N_HEADS, axis=1),
                                      0.0)))
    g_ref[0] = g.astype(BF16)

    p = seg("p")
    pe = jnp.concatenate([pcarry_ref[...], p], axis=0)
    pcarry_ref[...] = p[ts - POOL_HISTORY:, :]
    gp = seg("gp")
    t_glob = si * ts + lax.broadcasted_iota(jnp.int32, (ts, POOL_GROUP_DIM), 0)
    pooled = []
    for g, w in enumerate(POOL_WINDOWS):
        sl = slice(g * POOL_GROUP_DIM, (g + 1) * POOL_GROUP_DIM)
        y = pe[:, sl]
        sh = 1
        while sh < w:
            y = y + pltpu.roll(y, sh, axis=0)
            sh *= 2
        cnt = jnp.minimum(t_glob + 1, w).astype(F32)
        pooled.append((y[POOL_HISTORY:, :] / cnt - p[:, sl]).astype(BF16))
    for j in range(len(POOL_WINDOWS) // 2):
        sl = slice(2 * j * POOL_GROUP_DIM, (2 * j + 2) * POOL_GROUP_DIM)
        bias = jnp.concatenate([bpm_ref[2 * j:2 * j + 1, :], bpm_ref[2 * j + 1:2 * j + 2, :]],
                               axis=1)
        mixed = _dot(jnp.concatenate(pooled[2 * j:2 * j + 2], axis=1), wmix_ref[j]) + bias
        yp_ref[0, :, sl] = (mixed * ps_ref[:, sl] * _silu(gp[:, sl])).astype(BF16)

    ga_ref[0] = _silu(seg("ga")).astype(BF16)


def _in_call(x, ada, w_in_t, b_in, w_pm, b_pm, pool_scale):
    B, S, D = x.shape
    ts = SEQ_TILE
    const2 = lambda b, s: (0, 0)
    const3 = lambda b, s: (0, 0, 0)
    tile = lambda width: pl.BlockSpec((1, ts, width), lambda b, s: (b, s, 0))
    n_pairs = D_ATT // LANES
    tile_t = pl.BlockSpec((1, n_pairs, LANES, ts), lambda b, s: (b, 0, 0, s))
    return pl.pallas_call(
        _in_kernel,
        name="input_stage",
        grid=(B, S // ts),
        in_specs=[tile(D), pl.BlockSpec(ada.shape, const3),
                  _resident(w_in_t.shape, const2), pl.BlockSpec(b_in.shape, const2),
                  pl.BlockSpec(w_pm.shape, const3), pl.BlockSpec(b_pm.shape, const2),
                  pl.BlockSpec(pool_scale.shape, const2)],
        out_specs=[tile_t, tile(D_ATT), tile_t, tile(LANES),
                   tile(D_ATT), tile(D_POOL)],
        out_shape=[jax.ShapeDtypeStruct((B, n_pairs, LANES, S), BF16),
                   jax.ShapeDtypeStruct((B, S, D_ATT), BF16),
                   jax.ShapeDtypeStruct((B, n_pairs, LANES, S), BF16),
                   jax.ShapeDtypeStruct((B, S, LANES), BF16),
                   jax.ShapeDtypeStruct((B, S, D_ATT), BF16),
                   jax.ShapeDtypeStruct((B, S, D_POOL), BF16)],
        scratch_shapes=[pltpu.VMEM((SUBLANES, LANES), F32),
                        pltpu.VMEM((POOL_HISTORY, D_POOL), F32),
                        pltpu.VMEM((D, IN_COLS), BF16),
                        pltpu.VMEM((1, IN_COLS), F32),
                        pltpu.VMEM((len(POOL_WINDOWS) // 2, 2 * POOL_GROUP_DIM,
                                    2 * POOL_GROUP_DIM), BF16)],
        compiler_params=pltpu.CompilerParams(
            dimension_semantics=("arbitrary", "arbitrary"),
            vmem_limit_bytes=VMEM_LIMIT_BYTES),
    )(x, ada, w_in_t, b_in, w_pm, b_pm, pool_scale)


def _attn_kernel(qt_ref, k_ref, vt_ref, g_ref, ga_ref, yp_ref, x_ref, ada_ref,
                 wo_ref, bo_ref, lg_ref, lb_ref, o_ref,
                 m_ref, acc_ref, s0_ref, s1_ref, mx0_ref, mx1_ref, rhs_ref, wob_ref):
    bi = pl.program_id(0)
    n_main = pl.program_id(1)
    tq = Q_TILE
    tk = KV_TILE
    diag_start = pl.multiple_of(n_main * tk, tk)

    row = lax.broadcasted_iota(jnp.int32, (LANES, tq), 0)

    @pl.when((bi == 0) & (n_main == 0))
    def _():
        wob_ref[...] = wo_ref[...].astype(BF16)
        for h in range(N_HEADS):
            pick = (row < G_TERMS * N_HEADS) & (row % N_HEADS == h)
            rhs_ref[h, LANES:, :] = jnp.where(pick, 1.0, 0.0).astype(BF16)

    def begin_tile(pos):
        m_ref[...] = jnp.full_like(m_ref, NEG_BIG)
        acc_ref[...] = jnp.zeros_like(acc_ref)
        for h in range(N_HEADS):
            pair, half = divmod(h, HEADS_PER_BLOCK)
            own = (row // HEAD_DIM) == half
            qt = qt_ref[0, pair, :, pos * tq:(pos + 1) * tq]
            rhs_ref[h, :LANES, :] = jnp.where(own, qt, jnp.zeros((), BF16))

    def scores(start, width, s_ref, mx_ref, diagonal=False):
        def head(h):
            pair = h // HEADS_PER_BLOCK
            kblk = k_ref[0, pl.ds(start, width), pair * LANES:(pair + 1) * LANES]
            gblk = g_ref[0, pl.ds(start, width), :]
            lhs = jnp.concatenate([kblk, gblk], axis=1)
            s = _dot(lhs, rhs_ref[h])
            if diagonal:
                key = lax.broadcasted_iota(jnp.int32, (tq, tq), 0)
                qry = lax.broadcasted_iota(jnp.int32, (tq, tq), 1)
                own = jnp.where(key <= qry, s[width - tq:, :], NEG_BIG)
                s = own if width == tq else jnp.concatenate([s[:width - tq, :], own], axis=0)
            s_ref[h, :width, :] = s
            mx_ref[h] = jnp.max(s.reshape(width // SUBLANES, SUBLANES, tq), axis=0)

        return [lambda h=h: head(h) for h in range(N_HEADS)]

    def softmax_pv(start, width, s_ref, mx_ref):
        def head(h):
            pair, half = divmod(h, HEADS_PER_BLOCK)
            m_prev = m_ref[h]
            m_new = jnp.maximum(m_prev, jnp.max(mx_ref[h], axis=0, keepdims=True))
            m_ref[h] = m_new
            alpha = jnp.exp2(m_prev - m_new)
            p = jnp.exp2(s_ref[h, :width, :] - m_new[0:1, :]).astype(BF16)
            vt = vt_ref[0, pair, half * HEAD_DIM:(half + 1) * HEAD_DIM, pl.ds(start, width)]
            first = lax.broadcasted_iota(jnp.int32, (PV_ROWS - HEAD_DIM, width), 0) == 0
            ones = jnp.where(first, 1.0, 0.0).astype(BF16)
            pv = _dot(jnp.concatenate([vt, ones], axis=0), p)
            acc_ref[h] = acc_ref[h] * alpha[0:1, :] + pv

        return [lambda h=h: head(h) for h in range(N_HEADS)]

    def run(*stages):
        for i in range(max(len(st) for st in stages)):
            for st in stages:
                if i < len(st):
                    st[i]()

    main = lambda k: pl.multiple_of(k * tk, tk)

    def first_scores():
        run(scores(main(0), tk, s0_ref, mx0_ref))

    def attend(pos):
        width = (pos + 1) * tq
        diag_scores = lambda s_ref, mx_ref: scores(diag_start, width, s_ref, mx_ref, True)
        diag_softmax = lambda s_ref, mx_ref: softmax_pv(diag_start, width, s_ref, mx_ref)
        n_pairs = jnp.maximum(n_main - 1, 0) // 2

        def pair_of_blocks(k):
            run(scores(main(k + 1), tk, s1_ref, mx1_ref),
                softmax_pv(main(k), tk, s0_ref, mx0_ref))
            run(scores(main(k + 2), tk, s0_ref, mx0_ref),
                softmax_pv(main(k + 1), tk, s1_ref, mx1_ref))

        def two_pairs(i, carry):
            pair_of_blocks(4 * i)
            pair_of_blocks(4 * i + 2)
            return carry

        lax.fori_loop(0, n_pairs // 2, two_pairs, 0)

        @pl.when(n_pairs % 2 == 1)
        def _():
            pair_of_blocks(2 * (n_pairs - 1))

        k_last = 2 * n_pairs
        left = n_main - k_last

        @pl.when(left == 0)
        def _():
            run(diag_scores(s0_ref, mx0_ref))
            run(diag_softmax(s0_ref, mx0_ref))

        @pl.when(left == 1)
        def _():
            run(diag_scores(s1_ref, mx1_ref), softmax_pv(main(k_last), tk, s0_ref, mx0_ref))
            run(diag_softmax(s1_ref, mx1_ref))

        @pl.when(left == 2)
        def _():
            run(scores(main(k_last + 1), tk, s1_ref, mx1_ref),
                softmax_pv(main(k_last), tk, s0_ref, mx0_ref))
            run(diag_scores(s0_ref, mx0_ref), softmax_pv(main(k_last + 1), tk, s1_ref, mx1_ref))
            run(diag_softmax(s0_ref, mx0_ref))

    def gated_heads(pos):
        rows = slice(pos * tq, (pos + 1) * tq)
        gated = []
        for pair in range(N_HEADS // HEADS_PER_BLOCK):
            outs = []
            for half in range(HEADS_PER_BLOCK):
                acc = acc_ref[pair * HEADS_PER_BLOCK + half]
                outs.append(acc[:HEAD_DIM, :] / acc[HEAD_DIM:HEAD_DIM + 1, :])
            att = jnp.concatenate(outs, axis=0).T
            sl = slice(pair * LANES, (pair + 1) * LANES)
            gated.append((att * ga_ref[0, rows, sl].astype(F32)).astype(BF16))
        return jnp.concatenate(gated, axis=1)

    def project(pos, ya):
        rows = slice(pos * tq, (pos + 1) * tq)
        return _dot(jnp.concatenate([ya, yp_ref[0, rows, :]], axis=1), wob_ref[...]) + bo_ref[...]

    def residual_norm(pos, y):
        rows = slice(pos * tq, (pos + 1) * tq)
        hres = DEEPNORM_ALPHA * x_ref[0, rows, :] + ada_ref[2, pl.ds(bi, 1), :] * y
        mu = jnp.mean(hres, axis=-1, keepdims=True)
        d = hres - mu
        var = jnp.mean(d * d, axis=-1, keepdims=True)
        o_ref[0, rows, :] = d * lax.rsqrt(var + LN_EPS) * lg_ref[...] + lb_ref[...]

    n_pos = tk // tq
    begin_tile(0)

    @pl.when(n_main > 0)
    def _():
        first_scores()

    for pos in range(n_pos):
        attend(pos)
        if pos + 1 < n_pos:
            @pl.when(n_main > 0)
            def _():
                y = project(pos, gated_heads(pos))
                begin_tile(pos + 1)
                first_scores()
                residual_norm(pos, y)

            @pl.when(n_main == 0)
            def _():
                residual_norm(pos, project(pos, gated_heads(pos)))
                begin_tile(pos + 1)
        else:
            residual_norm(pos, project(pos, gated_heads(pos)))


def _attn_call(qt, k, vt, g, ga, yp, x, ada, w_out, b_out, ln_g, ln_b):
    B, n_pairs, _, S = qt.shape
    D = x.shape[2]
    assert KV_TILE % Q_TILE == 0 and S % KV_TILE == 0
    rows = KV_TILE
    tile = lambda width: pl.BlockSpec((1, rows, width), lambda b, i: (b, i, 0))
    whole = lambda width: pl.BlockSpec((1, S, width), lambda b, i: (b, 0, 0))
    const2 = lambda a: pl.BlockSpec(a.shape, lambda b, i: (0, 0))
    score_buf = pltpu.VMEM((N_HEADS, KV_TILE, Q_TILE), F32)
    stat_buf = pltpu.VMEM((N_HEADS, SUBLANES, Q_TILE), F32)
    return pl.pallas_call(
        _attn_kernel,
        name="attention",
        grid=(B, S // rows),
        in_specs=[pl.BlockSpec((1, n_pairs, LANES, rows), lambda b, i: (b, 0, 0, i)),
                  whole(D_ATT),
                  pl.BlockSpec((1, n_pairs, LANES, S), lambda b, i: (b, 0, 0, 0)),
                  whole(LANES), tile(D_ATT), tile(D_POOL), tile(D),
                  pl.BlockSpec(ada.shape, lambda b, i: (0, 0, 0)),
                  _resident(w_out.shape, lambda b, i: (0, 0)),
                  const2(b_out), const2(ln_g), const2(ln_b)],
        out_specs=tile(D),
        out_shape=jax.ShapeDtypeStruct((B, S, D), F32),
        scratch_shapes=[stat_buf, pltpu.VMEM((N_HEADS, PV_ROWS, Q_TILE), F32),
                        score_buf, score_buf, stat_buf, stat_buf,
                        pltpu.VMEM((N_HEADS, 2 * LANES, Q_TILE), BF16),
                        pltpu.VMEM(w_out.shape, BF16)],
        compiler_params=pltpu.CompilerParams(
            dimension_semantics=("arbitrary", "arbitrary"),
            vmem_limit_bytes=VMEM_LIMIT_BYTES),
    )(qt, k, vt, g, ga, yp, x, ada, w_out, b_out, ln_g, ln_b)


def _layer(x, c, w_ada, b_ada, w_in, b_in, w_pool_mix, b_pool_mix, pool_scale,
           w_out, b_out, ln_g, ln_b):
    ada = _ada_call(c, w_ada, b_ada[None, :])
    qt, k, vt, g, ga, yp = _in_call(x, ada, w_in.T, b_in[None, :], w_pool_mix, b_pool_mix,
                                    pool_scale[None, :])
    return _attn_call(qt, k, vt, g, ga, yp, x, ada, w_out, b_out[None, :], ln_g[None, :],
                      ln_b[None, :])


def kernel(x, c, w_ada, b_ada, w_in, b_in, w_pool_mix, b_pool_mix, pool_scale, w_out, b_out,
           ln_g, ln_b):
    for layer in range(w_ada.shape[0]):
        x = _layer(x, c, w_ada[layer], b_ada[layer], w_in[layer], b_in[layer],
                   w_pool_mix[layer], b_pool_mix[layer], pool_scale[layer],
                   w_out[layer], b_out[layer], ln_g[layer], ln_b[layer])
    return x
```

```python
import math

import jax
import jax.numpy as jnp
from jax import lax
from jax.experimental import pallas as pl
from jax.experimental.pallas import tpu as pltpu

D_MODEL = 1024
D_ATT = 512
D_POOL = 512
N_HEADS = 8
HEAD_DIM = 64
POOL_WINDOWS = (2, 4, 8, 16)
POOL_GROUP_DIM = 128
POOL_HISTORY = 16
LN_EPS = 1e-5
DEEPNORM_ALPHA = 2.0 ** 0.25
LOG2E = math.log2(math.e)
NEG_BIG = -1e30

LANES = 128
SUBLANES = 8
BF16_SUBLANES = 16
HEADS_PER_BLOCK = LANES // HEAD_DIM
G_TERMS = 3
CUM_ROWS = 128
PV_ROWS = HEAD_DIM + BF16_SUBLANES

IN_SOURCE = (("q", D_ATT), ("k", D_ATT), ("v", D_ATT), ("f", N_HEADS), ("p", D_POOL),
             ("ga", D_ATT), ("gp", D_POOL))
IN_WIDTH = {"f": LANES, "p": D_POOL, "gp": D_POOL, "q": D_ATT, "v": D_ATT, "k": D_ATT,
            "ga": D_ATT}
IN_OFFSET = dict(zip(IN_WIDTH, (sum(list(IN_WIDTH.values())[:i]) for i in range(len(IN_WIDTH)))))
IN_COLS = sum(IN_WIDTH.values())

SEQ_TILE = 1024
Q_TILE = 256
KV_TILE = 512
VMEM_LIMIT_BYTES = 56 * 1024 * 1024

F32 = jnp.float32
BF16 = jnp.bfloat16


def _silu(x):
    return x * jax.nn.sigmoid(x)


def _dot(a, b):
    return jnp.dot(a, b, preferred_element_type=F32)


def _resident(shape, index_map):
    return pl.BlockSpec(shape, index_map, pipeline_mode=pl.Buffered(1))


def _ada_kernel(c_ref, w_ref, b_ref, o_ref, cpad_ref):
    n_rows = c_ref.shape[0]
    cpad_ref[...] = jnp.zeros_like(cpad_ref)
    cpad_ref[0:n_rows, :] = c_ref[...]
    sc = _silu(cpad_ref[...].T)
    for b in range(n_rows):
        o_ref[0, b:b + 1, :] = (jnp.sum(w_ref[...] * sc[:, b:b + 1], axis=0, keepdims=True)
                                + b_ref[...])


def _ada_call(c, w_ada, b_ada):
    n_rows, d = c.shape
    n_chunks = w_ada.shape[1] // d
    return pl.pallas_call(
        _ada_kernel,
        name="ada_vector",
        grid=(n_chunks,),
        in_specs=[
            pl.BlockSpec((n_rows, d), lambda j: (0, 0)),
            pl.BlockSpec((d, d), lambda j: (0, j)),
            pl.BlockSpec((1, d), lambda j: (0, j)),
        ],
        out_specs=pl.BlockSpec((1, n_rows, d), lambda j: (j, 0, 0)),
        out_shape=jax.ShapeDtypeStruct((n_chunks, n_rows, d), F32),
        scratch_shapes=[pltpu.VMEM((LANES, d), F32)],
        compiler_params=pltpu.CompilerParams(dimension_semantics=("parallel",),
                                             vmem_limit_bytes=VMEM_LIMIT_BYTES),
    )(c, w_ada, b_ada)


def _split3(x):
    hi = x.astype(BF16).astype(F32)
    r = x - hi
    mid = r.astype(BF16).astype(F32)
    lo = (r - mid).astype(BF16).astype(F32)
    return hi, mid, lo


def _in_kernel(x_ref, ada_ref, wt_ref, b_ref, wpm_ref, bpm_ref, ps_ref,
               q_ref, k_ref, v_ref, g_ref, ga_ref, yp_ref,
               fcarry_ref, pcarry_ref, wcat_ref, bcat_ref, wmix_ref):
    bi = pl.program_id(0)
    si = pl.program_id(1)
    ts = x_ref.shape[1]

    @pl.when((bi == 0) & (si == 0))
    def _():
        src = 0
        for name, width in IN_SOURCE:
            dst, wide = IN_OFFSET[name], IN_WIDTH[name]
            w_seg = wt_ref[src:src + wide, :].T
            b_seg = b_ref[:, src:src + wide]
            if width < wide:
                w_seg = jnp.where(lax.broadcasted_iota(jnp.int32, w_seg.shape, 1) < width,
                                  w_seg, 0.0)
                b_seg = jnp.where(lax.broadcasted_iota(jnp.int32, b_seg.shape, 1) < width,
                                  b_seg, 0.0)
            wcat_ref[:, dst:dst + wide] = w_seg.astype(BF16)
            bcat_ref[:, dst:dst + wide] = b_seg
            src += width
        gd = POOL_GROUP_DIM
        wmix_ref[...] = jnp.zeros_like(wmix_ref)
        for g in range(len(POOL_WINDOWS)):
            o = (g % 2) * gd
            wmix_ref[g // 2, o:o + gd, o:o + gd] = wpm_ref[g].astype(BF16)

    @pl.when(si == 0)
    def _():
        fcarry_ref[...] = jnp.zeros_like(fcarry_ref)
        pcarry_ref[...] = jnp.zeros_like(pcarry_ref)

    shift = ada_ref[0, pl.ds(bi, 1), :]
    scale = ada_ref[1, pl.ds(bi, 1), :]
    u = (x_ref[0] * (1.0 + scale) + shift).astype(BF16)
    proj = _dot(u, wcat_ref[...]) + bcat_ref[...]
    seg = lambda name: proj[:, IN_OFFSET[name]:IN_OFFSET[name] + IN_WIDTH[name]]

    q = seg("q") * (HEAD_DIM ** -0.5 * LOG2E)
    for blk in range(D_ATT // LANES):
        q_ref[0, blk] = q[:, blk * LANES:(blk + 1) * LANES].T.astype(BF16)

    k_ref[0] = seg("k").astype(BF16)

    v = seg("v")
    for blk in range(D_ATT // LANES):
        v_ref[0, blk] = v[:, blk * LANES:(blk + 1) * LANES].T.astype(BF16)

    fl = seg("f")
    logf = jnp.minimum(fl, 0.0) - jnp.log1p(jnp.exp(-jnp.abs(fl)))
    row = lax.broadcasted_iota(jnp.int32, (CUM_ROWS, CUM_ROWS), 0)
    col = lax.broadcasted_iota(jnp.int32, (CUM_ROWS, CUM_ROWS), 1)
    tri = jnp.where(row >= col, 1.0, 0.0).astype(BF16)
    terms = jnp.concatenate([t.astype(BF16) for t in _split3(logf)], axis=1)
    offset = fcarry_ref[0:1, :]
    cum_blocks = []
    for r in range(ts // CUM_ROWS):
        part = _dot(tri, terms[r * CUM_ROWS:(r + 1) * CUM_ROWS, :])
        local = part[:, :LANES] + part[:, LANES:2 * LANES] + part[:, 2 * LANES:]
        cum_blocks.append(local + offset)
        offset = cum_blocks[-1][CUM_ROWS - 1:CUM_ROWS, :]
    fcarry_ref[...] = jnp.broadcast_to(offset, fcarry_ref.shape)
    cum = jnp.concatenate(cum_blocks, axis=0)

    g_hi, g_mid, g_lo = _split3(cum * (-LOG2E))
    lane = lax.broadcasted_iota(jnp.int32, (ts, LANES), 1)
    g = jnp.where(lane < N_HEADS, g_hi,
                  jnp.where(lane < 2 * N_HEADS, pltpu.roll(g_mid, N_HEADS, axis=1),
                            jnp.where(lane < 3 * N_HEADS, pltpu.roll(g_lo, 2 * N_HEADS, axis=1),
                                      0.0)))
    g_ref[0] = g.astype(BF16)

    p = seg("p")
    pe = jnp.concatenate([pcarry_ref[...], p], axis=0)
    pcarry_ref[...] = p[ts - POOL_HISTORY:, :]
    gp = seg("gp")
    t_glob = si * ts + lax.broadcasted_iota(jnp.int32, (ts, POOL_GROUP_DIM), 0)
    pooled = []
    for g, w in enumerate(POOL_WINDOWS):
        sl = slice(g * POOL_GROUP_DIM, (g + 1) * POOL_GROUP_DIM)
        y = pe[:, sl]
        sh = 1
        while sh < w:
            y = y + pltpu.roll(y, sh, axis=0)
            sh *= 2
        cnt = jnp.minimum(t_glob + 1, w).astype(F32)
        pooled.append((y[POOL_HISTORY:, :] / cnt - p[:, sl]).astype(BF16))
    for j in range(len(POOL_WINDOWS) // 2):
        sl = slice(2 * j * POOL_GROUP_DIM, (2 * j + 2) * POOL_GROUP_DIM)
        bias = jnp.concatenate([bpm_ref[2 * j:2 * j + 1, :], bpm_ref[2 * j + 1:2 * j + 2, :]],
                               axis=1)
        mixed = _dot(jnp.concatenate(pooled[2 * j:2 * j + 2], axis=1), wmix_ref[j]) + bias
        yp_ref[0, :, sl] = (mixed * ps_ref[:, sl] * _silu(gp[:, sl])).astype(BF16)

    ga_ref[0] = _silu(seg("ga")).astype(BF16)


def _in_call(x, ada, w_in_t, b_in, w_pm, b_pm, pool_scale):
    B, S, D = x.shape
    ts = SEQ_TILE
    const2 = lambda b, s: (0, 0)
    const3 = lambda b, s: (0, 0, 0)
    tile = lambda width: pl.BlockSpec((1, ts, width), lambda b, s: (b, s, 0))
    n_pairs = D_ATT // LANES
    tile_t = pl.BlockSpec((1, n_pairs, LANES, ts), lambda b, s: (b, 0, 0, s))
    return pl.pallas_call(
        _in_kernel,
        name="input_stage",
        grid=(B, S // ts),
        in_specs=[tile(D), pl.BlockSpec(ada.shape, const3),
                  _resident(w_in_t.shape, const2), pl.BlockSpec(b_in.shape, const2),
                  pl.BlockSpec(w_pm.shape, const3), pl.BlockSpec(b_pm.shape, const2),
                  pl.BlockSpec(pool_scale.shape, const2)],
        out_specs=[tile_t, tile(D_ATT), tile_t, tile(LANES),
                   tile(D_ATT), tile(D_POOL)],
        out_shape=[jax.ShapeDtypeStruct((B, n_pairs, LANES, S), BF16),
                   jax.ShapeDtypeStruct((B, S, D_ATT), BF16),
                   jax.ShapeDtypeStruct((B, n_pairs, LANES, S), BF16),
                   jax.ShapeDtypeStruct((B, S, LANES), BF16),
                   jax.ShapeDtypeStruct((B, S, D_ATT), BF16),
                   jax.ShapeDtypeStruct((B, S, D_POOL), BF16)],
        scratch_shapes=[pltpu.VMEM((SUBLANES, LANES), F32),
                        pltpu.VMEM((POOL_HISTORY, D_POOL), F32),
                        pltpu.VMEM((D, IN_COLS), BF16),
                        pltpu.VMEM((1, IN_COLS), F32),
                        pltpu.VMEM((len(POOL_WINDOWS) // 2, 2 * POOL_GROUP_DIM,
                                    2 * POOL_GROUP_DIM), BF16)],
        compiler_params=pltpu.CompilerParams(
            dimension_semantics=("arbitrary", "arbitrary"),
            vmem_limit_bytes=VMEM_LIMIT_BYTES),
    )(x, ada, w_in_t, b_in, w_pm, b_pm, pool_scale)


def _attn_kernel(qt_ref, k_ref, vt_ref, g_ref, ga_ref, yp_ref, x_ref, ada_ref,
                 wo_ref, bo_ref, lg_ref, lb_ref, o_ref,
                 m_ref, acc_ref, s0_ref, s1_ref, mx0_ref, mx1_ref, rhs_ref, wob_ref):
    bi = pl.program_id(0)
    n_main = pl.program_id(1)
    tq = Q_TILE
    tk = KV_TILE
    diag_start = pl.multiple_of(n_main * tk, tk)

    row = lax.broadcasted_iota(jnp.int32, (LANES, tq), 0)

    @pl.when((bi == 0) & (n_main == 0))
    def _():
        wob_ref[...] = wo_ref[...].astype(BF16)
        for h in range(N_HEADS):
            pick = (row < G_TERMS * N_HEADS) & (row % N_HEADS == h)
            rhs_ref[h, LANES:, :] = jnp.where(pick, 1.0, 0.0).astype(BF16)

    def begin_tile(pos):
        m_ref[...] = jnp.full_like(m_ref, NEG_BIG)
        acc_ref[...] = jnp.zeros_like(acc_ref)
        for h in range(N_HEADS):
            pair, half = divmod(h, HEADS_PER_BLOCK)
            own = (row // HEAD_DIM) == half
            qt = qt_ref[0, pair, :, pos * tq:(pos + 1) * tq]
            rhs_ref[h, :LANES, :] = jnp.where(own, qt, jnp.zeros((), BF16))

    def scores(start, width, s_ref, mx_ref, diagonal=False):
        def head(h):
            pair = h // HEADS_PER_BLOCK
            kblk = k_ref[0, pl.ds(start, width), pair * LANES:(pair + 1) * LANES]
            gblk = g_ref[0, pl.ds(start, width), :]
            lhs = jnp.concatenate([kblk, gblk], axis=1)
            s = _dot(lhs, rhs_ref[h])
            if diagonal:
                key = lax.broadcasted_iota(jnp.int32, (tq, tq), 0)
                qry = lax.broadcasted_iota(jnp.int32, (tq, tq), 1)
                own = jnp.where(key <= qry, s[width - tq:, :], NEG_BIG)
                s = own if width == tq else jnp.concatenate([s[:width - tq, :], own], axis=0)
            s_ref[h, :width, :] = s
            mx_ref[h] = jnp.max(s.reshape(width // SUBLANES, SUBLANES, tq), axis=0)

        return [lambda h=h: head(h) for h in range(N_HEADS)]

    def softmax_pv(start, width, s_ref, mx_ref):
        def head(h):
            pair, half = divmod(h, HEADS_PER_BLOCK)
            m_prev = m_ref[h]
            m_new = jnp.maximum(m_prev, jnp.max(mx_ref[h], axis=0, keepdims=True))
            m_ref[h] = m_new
            alpha = jnp.exp2(m_prev - m_new)
            p = jnp.exp2(s_ref[h, :width, :] - m_new[0:1, :]).astype(BF16)
            vt = vt_ref[0, pair, half * HEAD_DIM:(half + 1) * HEAD_DIM, pl.ds(start, width)]
            first = lax.broadcasted_iota(jnp.int32, (PV_ROWS - HEAD_DIM, width), 0) == 0
            ones = jnp.where(first, 1.0, 0.0).astype(BF16)
            pv = _dot(jnp.concatenate([vt, ones], axis=0), p)
            acc_ref[h] = acc_ref[h] * alpha[0:1, :] + pv

        return [lambda h=h: head(h) for h in range(N_HEADS)]

    def run(*stages):
        for i in range(max(len(st) for st in stages)):
            for st in stages:
                if i < len(st):
                    st[i]()

    main = lambda k: pl.multiple_of(k * tk, tk)

    def first_scores():
        run(scores(main(0), tk, s0_ref, mx0_ref))

    def attend(pos):
        width = (pos + 1) * tq
        diag_scores = lambda s_ref, mx_ref: scores(diag_start, width, s_ref, mx_ref, True)
        diag_softmax = lambda s_ref, mx_ref: softmax_pv(diag_start, width, s_ref, mx_ref)
        n_pairs = jnp.maximum(n_main - 1, 0) // 2

        def pair_of_blocks(k):
            run(scores(main(k + 1), tk, s1_ref, mx1_ref),
                softmax_pv(main(k), tk, s0_ref, mx0_ref))
            run(scores(main(k + 2), tk, s0_ref, mx0_ref),
                softmax_pv(main(k + 1), tk, s1_ref, mx1_ref))

        def two_pairs(i, carry):
            pair_of_blocks(4 * i)
            pair_of_blocks(4 * i + 2)
            return carry

        lax.fori_loop(0, n_pairs // 2, two_pairs, 0)

        @pl.when(n_pairs % 2 == 1)
        def _():
            pair_of_blocks(2 * (n_pairs - 1))

        k_last = 2 * n_pairs
        left = n_main - k_last

        @pl.when(left == 0)
        def _():
            run(diag_scores(s0_ref, mx0_ref))
            run(diag_softmax(s0_ref, mx0_ref))

        @pl.when(left == 1)
        def _():
            run(diag_scores(s1_ref, mx1_ref), softmax_pv(main(k_last), tk, s0_ref, mx0_ref))
            run(diag_softmax(s1_ref, mx1_ref))

        @pl.when(left == 2)
        def _():
            run(scores(main(k_last + 1), tk, s1_ref, mx1_ref),
                softmax_pv(main(k_last), tk, s0_ref, mx0_ref))
            run(diag_scores(s0_ref, mx0_ref), softmax_pv(main(k_last + 1), tk, s1_ref, mx1_ref))
            run(diag_softmax(s0_ref, mx0_ref))

    def gated_heads(pos):
        rows = slice(pos * tq, (pos + 1) * tq)
        gated = []
        for pair in range(N_HEADS // HEADS_PER_BLOCK):
            outs = []
            for half in range(HEADS_PER_BLOCK):
                acc = acc_ref[pair * HEADS_PER_BLOCK + half]
                outs.append(acc[:HEAD_DIM, :] / acc[HEAD_DIM:HEAD_DIM + 1, :])
            att = jnp.concatenate(outs, axis=0).T
            sl = slice(pair * LANES, (pair + 1) * LANES)
            gated.append((att * ga_ref[0, rows, sl].astype(F32)).astype(BF16))
        return jnp.concatenate(gated, axis=1)

    def project(pos, ya):
        rows = slice(pos * tq, (pos + 1) * tq)
        return _dot(jnp.concatenate([ya, yp_ref[0, rows, :]], axis=1), wob_ref[...]) + bo_ref[...]

    def residual_norm(pos, y):
        rows = slice(pos * tq, (pos + 1) * tq)
        hres = DEEPNORM_ALPHA * x_ref[0, rows, :] + ada_ref[2, pl.ds(bi, 1), :] * y
        mu = jnp.mean(hres, axis=-1, keepdims=True)
        d = hres - mu
        var = jnp.mean(d * d, axis=-1, keepdims=True)
        o_ref[0, rows, :] = d * lax.rsqrt(var + LN_EPS) * lg_ref[...] + lb_ref[...]

    n_pos = tk // tq
    begin_tile(0)

    @pl.when(n_main > 0)
    def _():
        first_scores()

    for pos in range(n_pos):
        attend(pos)
        if pos + 1 < n_pos:
            @pl.when(n_main > 0)
            def _():
                y = project(pos, gated_heads(pos))
                begin_tile(pos + 1)
                first_scores()
                residual_norm(pos, y)

            @pl.when(n_main == 0)
            def _():
                residual_norm(pos, project(pos, gated_heads(pos)))
                begin_tile(pos + 1)
        else:
            residual_norm(pos, project(pos, gated_heads(pos)))


def _attn_call(qt, k, vt, g, ga, yp, x, ada, w_out, b_out, ln_g, ln_b):
    B, n_pairs, _, S = qt.shape
    D = x.shape[2]
    assert KV_TILE % Q_TILE == 0 and S % KV_TILE == 0
    rows = KV_TILE
    tile = lambda width: pl.BlockSpec((1, rows, width), lambda b, i: (b, i, 0))
    whole = lambda width: pl.BlockSpec((1, S, width), lambda b, i: (b, 0, 0))
    const2 = lambda a: pl.BlockSpec(a.shape, lambda b, i: (0, 0))
    score_buf = pltpu.VMEM((N_HEADS, KV_TILE, Q_TILE), F32)
    stat_buf = pltpu.VMEM((N_HEADS, SUBLANES, Q_TILE), F32)
    return pl.pallas_call(
        _attn_kernel,
        name="attention",
        grid=(B, S // rows),
        in_specs=[pl.BlockSpec((1, n_pairs, LANES, rows), lambda b, i: (b, 0, 0, i)),
                  whole(D_ATT),
                  pl.BlockSpec((1, n_pairs, LANES, S), lambda b, i: (b, 0, 0, 0)),
                  whole(LANES), tile(D_ATT), tile(D_POOL), tile(D),
                  pl.BlockSpec(ada.shape, lambda b, i: (0, 0, 0)),
                  _resident(w_out.shape, lambda b, i: (0, 0)),
                  const2(b_out), const2(ln_g), const2(ln_b)],
        out_specs=tile(D),
        out_shape=jax.ShapeDtypeStruct((B, S, D), F32),
        scratch_shapes=[stat_buf, pltpu.VMEM((N_HEADS, PV_ROWS, Q_TILE), F32),
                        score_buf, score_buf, stat_buf, stat_buf,
                        pltpu.VMEM((N_HEADS, 2 * LANES, Q_TILE), BF16),
                        pltpu.VMEM(w_out.shape, BF16)],
        compiler_params=pltpu.CompilerParams(
            dimension_semantics=("arbitrary", "arbitrary"),
            vmem_limit_bytes=VMEM_LIMIT_BYTES),
    )(qt, k, vt, g, ga, yp, x, ada, w_out, b_out, ln_g, ln_b)


def _layer(x, c, w_ada, b_ada, w_in, b_in, w_pool_mix, b_pool_mix, pool_scale,
           w_out, b_out, ln_g, ln_b):
    ada = _ada_call(c, w_ada, b_ada[None, :])
    qt, k, vt, g, ga, yp = _in_call(x, ada, w_in.T, b_in[None, :], w_pool_mix, b_pool_mix,
                                    pool_scale[None, :])
    return _attn_call(qt, k, vt, g, ga, yp, x, ada, w_out, b_out[None, :], ln_g[None, :],
                      ln_b[None, :])


def kernel(x, c, w_ada, b_ada, w_in, b_in, w_pool_mix, b_pool_mix, pool_scale, w_out, b_out,
           ln_g, ln_b):
    for layer in range(w_ada.shape[0]):
        x = _layer(x, c, w_ada[layer], b_ada[layer], w_in[layer], b_in[layer],
                   w_pool_mix[layer], b_pool_mix[layer], pool_scale[layer],
                   w_out[layer], b_out[layer], ln_g[layer], ln_b[layer])
    return x
```

```python
import math

import jax
import jax.numpy as jnp
from jax import lax
from jax.experimental import pallas as pl
from jax.experimental.pallas import tpu as pltpu

D_MODEL = 1024
D_ATT = 512
D_POOL = 512
N_HEADS = 8
HEAD_DIM = 64
POOL_WINDOWS = (2, 4, 8, 16)
POOL_GROUP_DIM = 128
POOL_HISTORY = 16
LN_EPS = 1e-5
DEEPNORM_ALPHA = 2.0 ** 0.25
LOG2E = math.log2(math.e)
NEG_BIG = -1e30

LANES = 128
SUBLANES = 8
BF16_SUBLANES = 16
HEADS_PER_BLOCK = LANES // HEAD_DIM
G_TERMS = 3
CUM_ROWS = 128
PV_ROWS = HEAD_DIM + BF16_SUBLANES

IN_SOURCE = (("q", D_ATT), ("k", D_ATT), ("v", D_ATT), ("f", N_HEADS), ("p", D_POOL),
             ("ga", D_ATT), ("gp", D_POOL))
IN_WIDTH = {"f": LANES, "p": D_POOL, "gp": D_POOL, "ga": D_ATT, "q": D_ATT, "v": D_ATT,
            "k": D_ATT}
IN_OFFSET = dict(zip(IN_WIDTH, (sum(list(IN_WIDTH.values())[:i]) for i in range(len(IN_WIDTH)))))
IN_COLS = sum(IN_WIDTH.values())

SEQ_TILE = 1024
Q_TILE = 256
KV_TILE = 512
VMEM_LIMIT_BYTES = 56 * 1024 * 1024

F32 = jnp.float32
BF16 = jnp.bfloat16


def _silu(x):
    return x * jax.nn.sigmoid(x)


def _dot(a, b):
    return jnp.dot(a, b, preferred_element_type=F32)


def _resident(shape, index_map):
    return pl.BlockSpec(shape, index_map, pipeline_mode=pl.Buffered(1))


def _ada_kernel(c_ref, w_ref, b_ref, o_ref, cpad_ref):
    n_rows = c_ref.shape[0]
    cpad_ref[...] = jnp.zeros_like(cpad_ref)
    cpad_ref[0:n_rows, :] = c_ref[...]
    sc = _silu(cpad_ref[...].T)
    for b in range(n_rows):
        o_ref[0, b:b + 1, :] = (jnp.sum(w_ref[...] * sc[:, b:b + 1], axis=0, keepdims=True)
                                + b_ref[...])


def _ada_call(c, w_ada, b_ada):
    n_rows, d = c.shape
    n_chunks = w_ada.shape[1] // d
    return pl.pallas_call(
        _ada_kernel,
        name="ada_vector",
        grid=(n_chunks,),
        in_specs=[
            pl.BlockSpec((n_rows, d), lambda j: (0, 0)),
            pl.BlockSpec((d, d), lambda j: (0, j)),
            pl.BlockSpec((1, d), lambda j: (0, j)),
        ],
        out_specs=pl.BlockSpec((1, n_rows, d), lambda j: (j, 0, 0)),
        out_shape=jax.ShapeDtypeStruct((n_chunks, n_rows, d), F32),
        scratch_shapes=[pltpu.VMEM((LANES, d), F32)],
        compiler_params=pltpu.CompilerParams(dimension_semantics=("parallel",),
                                             vmem_limit_bytes=VMEM_LIMIT_BYTES),
    )(c, w_ada, b_ada)


def _split3(x):
    hi = x.astype(BF16).astype(F32)
    r = x - hi
    mid = r.astype(BF16).astype(F32)
    lo = (r - mid).astype(BF16).astype(F32)
    return hi, mid, lo


def _in_kernel(x_ref, ada_ref, wt_ref, b_ref, wpm_ref, bpm_ref, ps_ref,
               q_ref, k_ref, v_ref, g_ref, ga_ref, yp_ref,
               fcarry_ref, pcarry_ref, wcat_ref, bcat_ref, wmix_ref):
    bi = pl.program_id(0)
    si = pl.program_id(1)
    ts = x_ref.shape[1]

    @pl.when((bi == 0) & (si == 0))
    def _():
        src = 0
        for name, width in IN_SOURCE:
            dst, wide = IN_OFFSET[name], IN_WIDTH[name]
            w_seg = wt_ref[src:src + wide, :].T
            b_seg = b_ref[:, src:src + wide]
            if width < wide:
                w_seg = jnp.where(lax.broadcasted_iota(jnp.int32, w_seg.shape, 1) < width,
                                  w_seg, 0.0)
                b_seg = jnp.where(lax.broadcasted_iota(jnp.int32, b_seg.shape, 1) < width,
                                  b_seg, 0.0)
            wcat_ref[:, dst:dst + wide] = w_seg.astype(BF16)
            bcat_ref[:, dst:dst + wide] = b_seg
            src += width
        gd = POOL_GROUP_DIM
        wmix_ref[...] = jnp.zeros_like(wmix_ref)
        for g in range(len(POOL_WINDOWS)):
            o = (g % 2) * gd
            wmix_ref[g // 2, o:o + gd, o:o + gd] = wpm_ref[g].astype(BF16)

    @pl.when(si == 0)
    def _():
        fcarry_ref[...] = jnp.zeros_like(fcarry_ref)
        pcarry_ref[...] = jnp.zeros_like(pcarry_ref)

    shift = ada_ref[0, pl.ds(bi, 1), :]
    scale = ada_ref[1, pl.ds(bi, 1), :]
    u = (x_ref[0] * (1.0 + scale) + shift).astype(BF16)

    def project(*names):
        lo, hi = IN_OFFSET[names[0]], IN_OFFSET[names[-1]] + IN_WIDTH[names[-1]]
        out = _dot(u, wcat_ref[:, lo:hi]) + bcat_ref[:, lo:hi]
        return [out[:, IN_OFFSET[n] - lo:IN_OFFSET[n] - lo + IN_WIDTH[n]] for n in names]

    fl, p, gp = project("f", "p", "gp")
    ga, q, v = project("ga", "q", "v")

    ft = fl.T[:N_HEADS, :]
    logf = jnp.minimum(ft, 0.0) - jnp.log1p(jnp.exp(-jnp.abs(ft)))
    row = lax.broadcasted_iota(jnp.int32, (CUM_ROWS, CUM_ROWS), 0)
    col = lax.broadcasted_iota(jnp.int32, (CUM_ROWS, CUM_ROWS), 1)
    tri = jnp.where(row <= col, 1.0, 0.0).astype(BF16)
    pad = jnp.zeros((N_HEADS, ts), F32)
    terms = jnp.concatenate(list(_split3(logf)) + [pad], axis=0).astype(BF16)
    offset = fcarry_ref[...]
    cum_blocks = []
    for r in range(ts // CUM_ROWS):
        part = _dot(terms[:, r * CUM_ROWS:(r + 1) * CUM_ROWS], tri)
        local = part[:N_HEADS] + part[N_HEADS:2 * N_HEADS] + part[2 * N_HEADS:3 * N_HEADS]
        cum_blocks.append(local + offset)
        offset = jnp.broadcast_to(cum_blocks[-1][:, CUM_ROWS - 1:CUM_ROWS], offset.shape)
    fcarry_ref[...] = offset
    cum = jnp.concatenate(cum_blocks, axis=1)

    pad = jnp.zeros((LANES - G_TERMS * N_HEADS, ts), F32)
    g_t = jnp.concatenate(list(_split3(cum * (-LOG2E))) + [pad], axis=0)
    g_ref[0] = g_t.T.astype(BF16)

    pe = jnp.concatenate([pcarry_ref[...], p], axis=0)
    pcarry_ref[...] = p[ts - POOL_HISTORY:, :]
    t_head = si * ts + lax.broadcasted_iota(jnp.int32, (POOL_HISTORY, POOL_GROUP_DIM), 0)
    pooled = []
    for g, w in enumerate(POOL_WINDOWS):
        sl = slice(g * POOL_GROUP_DIM, (g + 1) * POOL_GROUP_DIM)
        y = pe[:, sl]
        sh = 1
        while sh < w:
            y = y + pltpu.roll(y, sh, axis=0)
            sh *= 2
        head = y[POOL_HISTORY:2 * POOL_HISTORY, :] / jnp.minimum(t_head + 1, w).astype(F32)
        rest = y[2 * POOL_HISTORY:, :]
        rest = rest * (1.0 / w) if w & (w - 1) == 0 else rest / float(w)
        mean = jnp.concatenate([head, rest], axis=0)
        pooled.append((mean - p[:, sl]).astype(BF16))
    mixed = []
    for j in range(len(POOL_WINDOWS) // 2):
        bias = jnp.concatenate([bpm_ref[2 * j:2 * j + 1, :], bpm_ref[2 * j + 1:2 * j + 2, :]],
                               axis=1)
        mixed.append(_dot(jnp.concatenate(pooled[2 * j:2 * j + 2], axis=1), wmix_ref[j]) + bias)

    (k,) = project("k")

    for j, mix in enumerate(mixed):
        sl = slice(2 * j * POOL_GROUP_DIM, (2 * j + 2) * POOL_GROUP_DIM)
        yp_ref[0, :, sl] = (mix * ps_ref[:, sl] * _silu(gp[:, sl])).astype(BF16)

    ga_ref[0] = _silu(ga).astype(BF16)

    q = q * (HEAD_DIM ** -0.5 * LOG2E)
    for blk in range(D_ATT // LANES):
        q_ref[0, blk] = q[:, blk * LANES:(blk + 1) * LANES].T.astype(BF16)

    for blk in range(D_ATT // LANES):
        v_ref[0, blk] = v[:, blk * LANES:(blk + 1) * LANES].T.astype(BF16)

    k_ref[0] = k.astype(BF16)


def _in_call(x, ada, w_in_t, b_in, w_pm, b_pm, pool_scale):
    B, S, D = x.shape
    ts = SEQ_TILE
    const2 = lambda b, s: (0, 0)
    const3 = lambda b, s: (0, 0, 0)
    tile = lambda width: pl.BlockSpec((1, ts, width), lambda b, s: (b, s, 0))
    n_pairs = D_ATT // LANES
    tile_t = pl.BlockSpec((1, n_pairs, LANES, ts), lambda b, s: (b, 0, 0, s))
    return pl.pallas_call(
        _in_kernel,
        name="input_stage",
        grid=(B, S // ts),
        in_specs=[tile(D), pl.BlockSpec(ada.shape, const3),
                  _resident(w_in_t.shape, const2), pl.BlockSpec(b_in.shape, const2),
                  pl.BlockSpec(w_pm.shape, const3), pl.BlockSpec(b_pm.shape, const2),
                  pl.BlockSpec(pool_scale.shape, const2)],
        out_specs=[tile_t, tile(D_ATT), tile_t, tile(LANES),
                   tile(D_ATT), tile(D_POOL)],
        out_shape=[jax.ShapeDtypeStruct((B, n_pairs, LANES, S), BF16),
                   jax.ShapeDtypeStruct((B, S, D_ATT), BF16),
                   jax.ShapeDtypeStruct((B, n_pairs, LANES, S), BF16),
                   jax.ShapeDtypeStruct((B, S, LANES), BF16),
                   jax.ShapeDtypeStruct((B, S, D_ATT), BF16),
                   jax.ShapeDtypeStruct((B, S, D_POOL), BF16)],
        scratch_shapes=[pltpu.VMEM((SUBLANES, LANES), F32),
                        pltpu.VMEM((POOL_HISTORY, D_POOL), F32),
                        pltpu.VMEM((D, IN_COLS), BF16),
                        pltpu.VMEM((1, IN_COLS), F32),
                        pltpu.VMEM((len(POOL_WINDOWS) // 2, 2 * POOL_GROUP_DIM,
                                    2 * POOL_GROUP_DIM), BF16)],
        compiler_params=pltpu.CompilerParams(
            dimension_semantics=("arbitrary", "arbitrary"),
            vmem_limit_bytes=VMEM_LIMIT_BYTES),
    )(x, ada, w_in_t, b_in, w_pm, b_pm, pool_scale)


def _attn_kernel(qt_ref, k_ref, vt_ref, g_ref, ga_ref, yp_ref, x_ref, ada_ref,
                 wo_ref, bo_ref, lg_ref, lb_ref, o_ref,
                 m_ref, acc_ref, s0_ref, s1_ref, mx0_ref, mx1_ref, rhs_ref, wob_ref):
    bi = pl.program_id(0)
    n_main = pl.program_id(1)
    tq = Q_TILE
    tk = KV_TILE
    diag_start = pl.multiple_of(n_main * tk, tk)

    row = lax.broadcasted_iota(jnp.int32, (LANES, tq), 0)

    @pl.when((bi == 0) & (n_main == 0))
    def _():
        wob_ref[...] = wo_ref[...].astype(BF16)
        for h in range(N_HEADS):
            pick = (row < G_TERMS * N_HEADS) & (row % N_HEADS == h)
            rhs_ref[h, LANES:, :] = jnp.where(pick, 1.0, 0.0).astype(BF16)

    def begin_tile(pos):
        m_ref[...] = jnp.full_like(m_ref, NEG_BIG)
        acc_ref[...] = jnp.zeros_like(acc_ref)
        for h in range(N_HEADS):
            pair, half = divmod(h, HEADS_PER_BLOCK)
            own = (row // HEAD_DIM) == half
            qt = qt_ref[0, pair, :, pos * tq:(pos + 1) * tq]
            rhs_ref[h, :LANES, :] = jnp.where(own, qt, jnp.zeros((), BF16))

    def scores(start, width, s_ref, mx_ref, diagonal=False):
        def head(h):
            pair = h // HEADS_PER_BLOCK
            kblk = k_ref[0, pl.ds(start, width), pair * LANES:(pair + 1) * LANES]
            gblk = g_ref[0, pl.ds(start, width), :]
            lhs = jnp.concatenate([kblk, gblk], axis=1)
            s = _dot(lhs, rhs_ref[h])
            if diagonal:
                key = lax.broadcasted_iota(jnp.int32, (tq, tq), 0)
                qry = lax.broadcasted_iota(jnp.int32, (tq, tq), 1)
                own = jnp.where(key <= qry, s[width - tq:, :], NEG_BIG)
                s = own if width == tq else jnp.concatenate([s[:width - tq, :], own], axis=0)
            s_ref[h, :width, :] = s
            mx_ref[h] = jnp.max(s.reshape(width // SUBLANES, SUBLANES, tq), axis=0)

        return [lambda h=h: head(h) for h in range(N_HEADS)]

    def softmax_pv(start, width, s_ref, mx_ref):
        def head(h):
            pair, half = divmod(h, HEADS_PER_BLOCK)
            m_prev = m_ref[h]
            m_new = jnp.maximum(m_prev, jnp.max(mx_ref[h], axis=0, keepdims=True))
            m_ref[h] = m_new
            alpha = jnp.exp2(m_prev - m_new)
            p = jnp.exp2(s_ref[h, :width, :] - m_new[0:1, :]).astype(BF16)
            vt = vt_ref[0, pair, half * HEAD_DIM:(half + 1) * HEAD_DIM, pl.ds(start, width)]
            first = lax.broadcasted_iota(jnp.int32, (PV_ROWS - HEAD_DIM, width), 0) == 0
            ones = jnp.where(first, 1.0, 0.0).astype(BF16)
            pv = _dot(jnp.concatenate([vt, ones], axis=0), p)
            acc_ref[h] = acc_ref[h] * alpha[0:1, :] + pv

        return [lambda h=h: head(h) for h in range(N_HEADS)]

    def run(*stages):
        for i in range(max(len(st) for st in stages)):
            for st in stages:
                if i < len(st):
                    st[i]()

    main = lambda k: pl.multiple_of(k * tk, tk)

    def first_scores():
        run(scores(main(0), tk, s0_ref, mx0_ref))

    def attend(pos):
        width = (pos + 1) * tq
        diag_scores = lambda s_ref, mx_ref: scores(diag_start, width, s_ref, mx_ref, True)
        diag_softmax = lambda s_ref, mx_ref: softmax_pv(diag_start, width, s_ref, mx_ref)
        n_pairs = jnp.maximum(n_main - 1, 0) // 2

        def pair_of_blocks(k):
            run(scores(main(k + 1), tk, s1_ref, mx1_ref),
                softmax_pv(main(k), tk, s0_ref, mx0_ref))
            run(scores(main(k + 2), tk, s0_ref, mx0_ref),
                softmax_pv(main(k + 1), tk, s1_ref, mx1_ref))

        def two_pairs(i, carry):
            pair_of_blocks(4 * i)
            pair_of_blocks(4 * i + 2)
            return carry

        lax.fori_loop(0, n_pairs // 2, two_pairs, 0)

        @pl.when(n_pairs % 2 == 1)
        def _():
            pair_of_blocks(2 * (n_pairs - 1))

        k_last = 2 * n_pairs
        left = n_main - k_last

        @pl.when(left == 0)
        def _():
            run(diag_scores(s0_ref, mx0_ref))
            run(diag_softmax(s0_ref, mx0_ref))

        @pl.when(left == 1)
        def _():
            run(diag_scores(s1_ref, mx1_ref), softmax_pv(main(k_last), tk, s0_ref, mx0_ref))
            run(diag_softmax(s1_ref, mx1_ref))

        @pl.when(left == 2)
        def _():
            run(scores(main(k_last + 1), tk, s1_ref, mx1_ref),
                softmax_pv(main(k_last), tk, s0_ref, mx0_ref))
            run(diag_scores(s0_ref, mx0_ref), softmax_pv(main(k_last + 1), tk, s1_ref, mx1_ref))
            run(diag_softmax(s0_ref, mx0_ref))

    def gated_heads(pos):
        rows = slice(pos * tq, (pos + 1) * tq)
        gated = []
        for pair in range(N_HEADS // HEADS_PER_BLOCK):
            outs = []
            for half in range(HEADS_PER_BLOCK):
                acc = acc_ref[pair * HEADS_PER_BLOCK + half]
                outs.append(acc[:HEAD_DIM, :] / acc[HEAD_DIM:HEAD_DIM + 1, :])
            att = jnp.concatenate(outs, axis=0).T
            sl = slice(pair * LANES, (pair + 1) * LANES)
            gated.append((att * ga_ref[0, rows, sl].astype(F32)).astype(BF16))
        return jnp.concatenate(gated, axis=1)

    def project(pos, ya):
        rows = slice(pos * tq, (pos + 1) * tq)
        return _dot(jnp.concatenate([ya, yp_ref[0, rows, :]], axis=1), wob_ref[...]) + bo_ref[...]

    def residual_norm(pos, y):
        rows = slice(pos * tq, (pos + 1) * tq)
        hres = DEEPNORM_ALPHA * x_ref[0, rows, :] + ada_ref[2, pl.ds(bi, 1), :] * y
        mu = jnp.mean(hres, axis=-1, keepdims=True)
        d = hres - mu
        var = jnp.mean(d * d, axis=-1, keepdims=True)
        o_ref[0, rows, :] = d * lax.rsqrt(var + LN_EPS) * lg_ref[...] + lb_ref[...]

    n_pos = tk // tq
    begin_tile(0)

    @pl.when(n_main > 0)
    def _():
        first_scores()

    for pos in range(n_pos):
        attend(pos)
        if pos + 1 < n_pos:
            @pl.when(n_main > 0)
            def _():
                y = project(pos, gated_heads(pos))
                begin_tile(pos + 1)
                first_scores()
                residual_norm(pos, y)

            @pl.when(n_main == 0)
            def _():
                residual_norm(pos, project(pos, gated_heads(pos)))
                begin_tile(pos + 1)
        else:
            residual_norm(pos, project(pos, gated_heads(pos)))


def _attn_call(qt, k, vt, g, ga, yp, x, ada, w_out, b_out, ln_g, ln_b):
    B, n_pairs, _, S = qt.shape
    D = x.shape[2]
    assert KV_TILE % Q_TILE == 0 and S % KV_TILE == 0
    rows = KV_TILE
    tile = lambda width: pl.BlockSpec((1, rows, width), lambda b, i: (b, i, 0))
    whole = lambda width: pl.BlockSpec((1, S, width), lambda b, i: (b, 0, 0))
    const2 = lambda a: pl.BlockSpec(a.shape, lambda b, i: (0, 0))
    score_buf = pltpu.VMEM((N_HEADS, KV_TILE, Q_TILE), F32)
    stat_buf = pltpu.VMEM((N_HEADS, SUBLANES, Q_TILE), F32)
    return pl.pallas_call(
        _attn_kernel,
        name="attention",
        grid=(B, S // rows),
        in_specs=[pl.BlockSpec((1, n_pairs, LANES, rows), lambda b, i: (b, 0, 0, i)),
                  whole(D_ATT),
                  pl.BlockSpec((1, n_pairs, LANES, S), lambda b, i: (b, 0, 0, 0)),
                  whole(LANES), tile(D_ATT), tile(D_POOL), tile(D),
                  pl.BlockSpec(ada.shape, lambda b, i: (0, 0, 0)),
                  _resident(w_out.shape, lambda b, i: (0, 0)),
                  const2(b_out), const2(ln_g), const2(ln_b)],
        out_specs=tile(D),
        out_shape=jax.ShapeDtypeStruct((B, S, D), F32),
        scratch_shapes=[stat_buf, pltpu.VMEM((N_HEADS, PV_ROWS, Q_TILE), F32),
                        score_buf, score_buf, stat_buf, stat_buf,
                        pltpu.VMEM((N_HEADS, 2 * LANES, Q_TILE), BF16),
                        pltpu.VMEM(w_out.shape, BF16)],
        compiler_params=pltpu.CompilerParams(
            dimension_semantics=("arbitrary", "arbitrary"),
            vmem_limit_bytes=VMEM_LIMIT_BYTES),
    )(qt, k, vt, g, ga, yp, x, ada, w_out, b_out, ln_g, ln_b)


def _layer(x, c, w_ada, b_ada, w_in, b_in, w_pool_mix, b_pool_mix, pool_scale,
           w_out, b_out, ln_g, ln_b):
    ada = _ada_call(c, w_ada, b_ada[None, :])
    qt, k, vt, g, ga, yp = _in_call(x, ada, w_in.T, b_in[None, :], w_pool_mix, b_pool_mix,
                                    pool_scale[None, :])
    return _attn_call(qt, k, vt, g, ga, yp, x, ada, w_out, b_out[None, :], ln_g[None, :],
                      ln_b[None, :])


def kernel(x, c, w_ada, b_ada, w_in, b_in, w_pool_mix, b_pool_mix, pool_scale, w_out, b_out,
           ln_g, ln_b):
    for layer in range(w_ada.shape[0]):
        x = _layer(x, c, w_ada[layer], b_ada[layer], w_in[layer], b_in[layer],
                   w_pool_mix[layer], b_pool_mix[layer], pool_scale[layer],
                   w_out[layer], b_out[layer], ln_g[layer], ln_b[layer])
    return x
```

```python
import math

import jax
import jax.numpy as jnp
from jax import lax
from jax.experimental import pallas as pl
from jax.experimental.pallas import tpu as pltpu

D_MODEL = 1024
D_ATT = 512
D_POOL = 512
N_HEADS = 8
HEAD_DIM = 64
POOL_WINDOWS = (2, 4, 8, 16)
POOL_GROUP_DIM = 128
POOL_HISTORY = 16
LN_EPS = 1e-5
DEEPNORM_ALPHA = 2.0 ** 0.25
LOG2E = math.log2(math.e)
NEG_BIG = -1e30

LANES = 128
SUBLANES = 8
BF16_SUBLANES = 16
HEADS_PER_BLOCK = LANES // HEAD_DIM
G_TERMS = 3
CUM_ROWS = 128
PV_ROWS = HEAD_DIM + BF16_SUBLANES

IN_SOURCE = (("q", D_ATT), ("k", D_ATT), ("v", D_ATT), ("f", N_HEADS), ("p", D_POOL),
             ("ga", D_ATT), ("gp", D_POOL))
IN_WIDTH = {"f": LANES, "p": D_POOL, "gp": D_POOL, "ga": D_ATT, "q": D_ATT, "v": D_ATT,
            "k": D_ATT}
IN_OFFSET = dict(zip(IN_WIDTH, (sum(list(IN_WIDTH.values())[:i]) for i in range(len(IN_WIDTH)))))
IN_COLS = sum(IN_WIDTH.values())

SEQ_TILE = 1024
Q_TILE = 256
KV_TILE = 512
OUT_TILE = 2048
OUT_CHUNK = 256
VMEM_LIMIT_BYTES = 56 * 1024 * 1024

F32 = jnp.float32
BF16 = jnp.bfloat16


def _silu(x):
    return x * jax.nn.sigmoid(x)


def _dot(a, b):
    return jnp.dot(a, b, preferred_element_type=F32)


def _resident(shape, index_map):
    return pl.BlockSpec(shape, index_map, pipeline_mode=pl.Buffered(1))


def _ada_kernel(c_ref, w_ref, b_ref, o_ref, cpad_ref):
    n_rows = c_ref.shape[0]
    cpad_ref[...] = jnp.zeros_like(cpad_ref)
    cpad_ref[0:n_rows, :] = c_ref[...]
    sc = _silu(cpad_ref[...].T)
    for b in range(n_rows):
        o_ref[0, b:b + 1, :] = (jnp.sum(w_ref[...] * sc[:, b:b + 1], axis=0, keepdims=True)
                                + b_ref[...])


def _ada_call(c, w_ada, b_ada):
    n_rows, d = c.shape
    n_chunks = w_ada.shape[1] // d
    return pl.pallas_call(
        _ada_kernel,
        name="ada_vector",
        grid=(n_chunks,),
        in_specs=[
            pl.BlockSpec((n_rows, d), lambda j: (0, 0)),
            pl.BlockSpec((d, d), lambda j: (0, j)),
            pl.BlockSpec((1, d), lambda j: (0, j)),
        ],
        out_specs=pl.BlockSpec((1, n_rows, d), lambda j: (j, 0, 0)),
        out_shape=jax.ShapeDtypeStruct((n_chunks, n_rows, d), F32),
        scratch_shapes=[pltpu.VMEM((LANES, d), F32)],
        compiler_params=pltpu.CompilerParams(dimension_semantics=("parallel",),
                                             vmem_limit_bytes=VMEM_LIMIT_BYTES),
    )(c, w_ada, b_ada)


def _split3(x):
    hi = x.astype(BF16).astype(F32)
    r = x - hi
    mid = r.astype(BF16).astype(F32)
    lo = (r - mid).astype(BF16).astype(F32)
    return hi, mid, lo


def _in_kernel(x_ref, ada_ref, wt_ref, b_ref, wpm_ref, bpm_ref, ps_ref,
               q_ref, k_ref, v_ref, g_ref, ga_ref, yp_ref,
               fcarry_ref, pcarry_ref, wcat_ref, bcat_ref, wmix_ref):
    bi = pl.program_id(0)
    si = pl.program_id(1)
    ts = x_ref.shape[1]

    @pl.when((bi == 0) & (si == 0))
    def _():
        src = 0
        for name, width in IN_SOURCE:
            dst, wide = IN_OFFSET[name], IN_WIDTH[name]
            w_seg = wt_ref[src:src + wide, :].T
            b_seg = b_ref[:, src:src + wide]
            if width < wide:
                w_seg = jnp.where(lax.broadcasted_iota(jnp.int32, w_seg.shape, 1) < width,
                                  w_seg, 0.0)
                b_seg = jnp.where(lax.broadcasted_iota(jnp.int32, b_seg.shape, 1) < width,
                                  b_seg, 0.0)
            wcat_ref[:, dst:dst + wide] = w_seg.astype(BF16)
            bcat_ref[:, dst:dst + wide] = b_seg
            src += width
        gd = POOL_GROUP_DIM
        wmix_ref[...] = jnp.zeros_like(wmix_ref)
        for g in range(len(POOL_WINDOWS)):
            o = (g % 2) * gd
            wmix_ref[g // 2, o:o + gd, o:o + gd] = wpm_ref[g].astype(BF16)

    @pl.when(si == 0)
    def _():
        fcarry_ref[...] = jnp.zeros_like(fcarry_ref)
        pcarry_ref[...] = jnp.zeros_like(pcarry_ref)

    shift = ada_ref[0, pl.ds(bi, 1), :]
    scale = ada_ref[1, pl.ds(bi, 1), :]
    u = (x_ref[0] * (1.0 + scale) + shift).astype(BF16)

    def project(*names):
        lo, hi = IN_OFFSET[names[0]], IN_OFFSET[names[-1]] + IN_WIDTH[names[-1]]
        out = _dot(u, wcat_ref[:, lo:hi]) + bcat_ref[:, lo:hi]
        return [out[:, IN_OFFSET[n] - lo:IN_OFFSET[n] - lo + IN_WIDTH[n]] for n in names]

    fl, p, gp = project("f", "p", "gp")
    ga, q, v = project("ga", "q", "v")

    ft = fl.T[:N_HEADS, :]
    logf = jnp.minimum(ft, 0.0) - jnp.log1p(jnp.exp(-jnp.abs(ft)))
    row = lax.broadcasted_iota(jnp.int32, (CUM_ROWS, CUM_ROWS), 0)
    col = lax.broadcasted_iota(jnp.int32, (CUM_ROWS, CUM_ROWS), 1)
    tri = jnp.where(row <= col, 1.0, 0.0).astype(BF16)
    pad = jnp.zeros((N_HEADS, ts), F32)
    terms = jnp.concatenate(list(_split3(logf)) + [pad], axis=0).astype(BF16)
    offset = fcarry_ref[...]
    cum_blocks = []
    for r in range(ts // CUM_ROWS):
        part = _dot(terms[:, r * CUM_ROWS:(r + 1) * CUM_ROWS], tri)
        local = part[:N_HEADS] + part[N_HEADS:2 * N_HEADS] + part[2 * N_HEADS:3 * N_HEADS]
        cum_blocks.append(local + offset)
        offset = jnp.broadcast_to(cum_blocks[-1][:, CUM_ROWS - 1:CUM_ROWS], offset.shape)
    fcarry_ref[...] = offset
    cum = jnp.concatenate(cum_blocks, axis=1)

    pad = jnp.zeros((LANES - G_TERMS * N_HEADS, ts), F32)
    g_t = jnp.concatenate(list(_split3(cum * (-LOG2E))) + [pad], axis=0)
    g_ref[0] = g_t.T.astype(BF16)

    pe = jnp.concatenate([pcarry_ref[...], p], axis=0)
    pcarry_ref[...] = p[ts - POOL_HISTORY:, :]
    t_head = si * ts + lax.broadcasted_iota(jnp.int32, (POOL_HISTORY, POOL_GROUP_DIM), 0)
    pooled = []
    for g, w in enumerate(POOL_WINDOWS):
        sl = slice(g * POOL_GROUP_DIM, (g + 1) * POOL_GROUP_DIM)
        y = pe[:, sl]
        sh = 1
        while sh < w:
            y = y + pltpu.roll(y, sh, axis=0)
            sh *= 2
        head = y[POOL_HISTORY:2 * POOL_HISTORY, :] / jnp.minimum(t_head + 1, w).astype(F32)
        rest = y[2 * POOL_HISTORY:, :]
        rest = rest * (1.0 / w) if w & (w - 1) == 0 else rest / float(w)
        mean = jnp.concatenate([head, rest], axis=0)
        pooled.append((mean - p[:, sl]).astype(BF16))
    mixed = []
    for j in range(len(POOL_WINDOWS) // 2):
        bias = jnp.concatenate([bpm_ref[2 * j:2 * j + 1, :], bpm_ref[2 * j + 1:2 * j + 2, :]],
                               axis=1)
        mixed.append(_dot(jnp.concatenate(pooled[2 * j:2 * j + 2], axis=1), wmix_ref[j]) + bias)

    (k,) = project("k")

    for j, mix in enumerate(mixed):
        sl = slice(2 * j * POOL_GROUP_DIM, (2 * j + 2) * POOL_GROUP_DIM)
        yp_ref[0, :, sl] = (mix * ps_ref[:, sl] * _silu(gp[:, sl])).astype(BF16)

    ga_ref[0] = _silu(ga).astype(BF16)

    q = q * (HEAD_DIM ** -0.5 * LOG2E)
    for blk in range(D_ATT // LANES):
        q_ref[0, blk] = q[:, blk * LANES:(blk + 1) * LANES].T.astype(BF16)

    for blk in range(D_ATT // LANES):
        v_ref[0, blk] = v[:, blk * LANES:(blk + 1) * LANES].T.astype(BF16)

    k_ref[0] = k.astype(BF16)


def _in_call(x, ada, w_in_t, b_in, w_pm, b_pm, pool_scale):
    B, S, D = x.shape
    ts = SEQ_TILE
    const2 = lambda b, s: (0, 0)
    const3 = lambda b, s: (0, 0, 0)
    tile = lambda width: pl.BlockSpec((1, ts, width), lambda b, s: (b, s, 0))
    n_pairs = D_ATT // LANES
    tile_t = pl.BlockSpec((1, n_pairs, LANES, ts), lambda b, s: (b, 0, 0, s))
    return pl.pallas_call(
        _in_kernel,
        name="input_stage",
        grid=(B, S // ts),
        in_specs=[tile(D), pl.BlockSpec(ada.shape, const3),
                  _resident(w_in_t.shape, const2), pl.BlockSpec(b_in.shape, const2),
                  pl.BlockSpec(w_pm.shape, const3), pl.BlockSpec(b_pm.shape, const2),
                  pl.BlockSpec(pool_scale.shape, const2)],
        out_specs=[tile_t, tile(D_ATT), tile_t, tile(LANES),
                   tile(D_ATT), tile(D_POOL)],
        out_shape=[jax.ShapeDtypeStruct((B, n_pairs, LANES, S), BF16),
                   jax.ShapeDtypeStruct((B, S, D_ATT), BF16),
                   jax.ShapeDtypeStruct((B, n_pairs, LANES, S), BF16),
                   jax.ShapeDtypeStruct((B, S, LANES), BF16),
                   jax.ShapeDtypeStruct((B, S, D_ATT), BF16),
                   jax.ShapeDtypeStruct((B, S, D_POOL), BF16)],
        scratch_shapes=[pltpu.VMEM((SUBLANES, LANES), F32),
                        pltpu.VMEM((POOL_HISTORY, D_POOL), F32),
                        pltpu.VMEM((D, IN_COLS), BF16),
                        pltpu.VMEM((1, IN_COLS), F32),
                        pltpu.VMEM((len(POOL_WINDOWS) // 2, 2 * POOL_GROUP_DIM,
                                    2 * POOL_GROUP_DIM), BF16)],
        compiler_params=pltpu.CompilerParams(
            dimension_semantics=("arbitrary", "arbitrary"),
            vmem_limit_bytes=VMEM_LIMIT_BYTES),
    )(x, ada, w_in_t, b_in, w_pm, b_pm, pool_scale)


def _attn_kernel(qt_ref, k_ref, vt_ref, g_ref, ga_ref, o_ref,
                 m_ref, acc_ref, s0_ref, s1_ref, mx0_ref, mx1_ref, rhs_ref):
    bi = pl.program_id(0)
    n_main = pl.program_id(1)
    tq = Q_TILE
    tk = KV_TILE
    diag_start = pl.multiple_of(n_main * tk, tk)

    row = lax.broadcasted_iota(jnp.int32, (LANES, tq), 0)

    @pl.when((bi == 0) & (n_main == 0))
    def _():
        for h in range(N_HEADS):
            pick = (row < G_TERMS * N_HEADS) & (row % N_HEADS == h)
            rhs_ref[h, LANES:, :] = jnp.where(pick, 1.0, 0.0).astype(BF16)

    def begin_tile(pos):
        m_ref[...] = jnp.full_like(m_ref, NEG_BIG)
        acc_ref[...] = jnp.zeros_like(acc_ref)
        for h in range(N_HEADS):
            pair, half = divmod(h, HEADS_PER_BLOCK)
            own = (row // HEAD_DIM) == half
            qt = qt_ref[0, pair, :, pos * tq:(pos + 1) * tq]
            rhs_ref[h, :LANES, :] = jnp.where(own, qt, jnp.zeros((), BF16))

    def scores(start, width, s_ref, mx_ref, diagonal=False):
        def head(h):
            pair = h // HEADS_PER_BLOCK
            kblk = k_ref[0, pl.ds(start, width), pair * LANES:(pair + 1) * LANES]
            gblk = g_ref[0, pl.ds(start, width), :]
            lhs = jnp.concatenate([kblk, gblk], axis=1)
            s = _dot(lhs, rhs_ref[h])
            if diagonal:
                key = lax.broadcasted_iota(jnp.int32, (tq, tq), 0)
                qry = lax.broadcasted_iota(jnp.int32, (tq, tq), 1)
                own = jnp.where(key <= qry, s[width - tq:, :], NEG_BIG)
                s = own if width == tq else jnp.concatenate([s[:width - tq, :], own], axis=0)
            s_ref[h, :width, :] = s
            mx_ref[h] = jnp.max(s.reshape(width // SUBLANES, SUBLANES, tq), axis=0)

        return [lambda h=h: head(h) for h in range(N_HEADS)]

    def softmax_pv(start, width, s_ref, mx_ref):
        def head(h):
            pair, half = divmod(h, HEADS_PER_BLOCK)
            m_prev = m_ref[h]
            m_new = jnp.maximum(m_prev, jnp.max(mx_ref[h], axis=0, keepdims=True))
            m_ref[h] = m_new
            alpha = jnp.exp2(m_prev - m_new)
            p = jnp.exp2(s_ref[h, :width, :] - m_new[0:1, :]).astype(BF16)
            vt = vt_ref[0, pair, half * HEAD_DIM:(half + 1) * HEAD_DIM, pl.ds(start, width)]
            first = lax.broadcasted_iota(jnp.int32, (PV_ROWS - HEAD_DIM, width), 0) == 0
            ones = jnp.where(first, 1.0, 0.0).astype(BF16)
            pv = _dot(jnp.concatenate([vt, ones], axis=0), p)
            acc_ref[h] = acc_ref[h] * alpha[0:1, :] + pv

        return [lambda h=h: head(h) for h in range(N_HEADS)]

    def run(*stages):
        for i in range(max(len(st) for st in stages)):
            for st in stages:
                if i < len(st):
                    st[i]()

    main = lambda k: pl.multiple_of(k * tk, tk)

    def first_scores():
        run(scores(main(0), tk, s0_ref, mx0_ref))

    def attend(pos):
        width = (pos + 1) * tq
        diag_scores = lambda s_ref, mx_ref: scores(diag_start, width, s_ref, mx_ref, True)
        diag_softmax = lambda s_ref, mx_ref: softmax_pv(diag_start, width, s_ref, mx_ref)
        n_pairs = jnp.maximum(n_main - 1, 0) // 2

        def pair_of_blocks(k):
            run(scores(main(k + 1), tk, s1_ref, mx1_ref),
                softmax_pv(main(k), tk, s0_ref, mx0_ref))
            run(scores(main(k + 2), tk, s0_ref, mx0_ref),
                softmax_pv(main(k + 1), tk, s1_ref, mx1_ref))

        def two_pairs(i, carry):
            pair_of_blocks(4 * i)
            pair_of_blocks(4 * i + 2)
            return carry

        lax.fori_loop(0, n_pairs // 2, two_pairs, 0)

        @pl.when(n_pairs % 2 == 1)
        def _():
            pair_of_blocks(2 * (n_pairs - 1))

        k_last = 2 * n_pairs
        left = n_main - k_last

        @pl.when(left == 0)
        def _():
            run(diag_scores(s0_ref, mx0_ref))
            run(diag_softmax(s0_ref, mx0_ref))

        @pl.when(left == 1)
        def _():
            run(diag_scores(s1_ref, mx1_ref), softmax_pv(main(k_last), tk, s0_ref, mx0_ref))
            run(diag_softmax(s1_ref, mx1_ref))

        @pl.when(left == 2)
        def _():
            run(scores(main(k_last + 1), tk, s1_ref, mx1_ref),
                softmax_pv(main(k_last), tk, s0_ref, mx0_ref))
            run(diag_scores(s0_ref, mx0_ref), softmax_pv(main(k_last + 1), tk, s1_ref, mx1_ref))
            run(diag_softmax(s0_ref, mx0_ref))

    def gated_heads(pos):
        rows = slice(pos * tq, (pos + 1) * tq)
        gated = []
        for pair in range(N_HEADS // HEADS_PER_BLOCK):
            outs = []
            for half in range(HEADS_PER_BLOCK):
                acc = acc_ref[pair * HEADS_PER_BLOCK + half]
                outs.append(acc[:HEAD_DIM, :] / acc[HEAD_DIM:HEAD_DIM + 1, :])
            att = jnp.concatenate(outs, axis=0).T
            sl = slice(pair * LANES, (pair + 1) * LANES)
            gated.append((att * ga_ref[0, rows, sl].astype(F32)).astype(BF16))
        return jnp.concatenate(gated, axis=1)

    def emit(pos):
        o_ref[0, pos * tq:(pos + 1) * tq, :] = gated_heads(pos)

    n_pos = tk // tq
    begin_tile(0)

    @pl.when(n_main > 0)
    def _():
        first_scores()

    for pos in range(n_pos):
        attend(pos)
        if pos + 1 < n_pos:
            @pl.when(n_main > 0)
            def _():
                emit(pos)
                begin_tile(pos + 1)
                first_scores()

            @pl.when(n_main == 0)
            def _():
                emit(pos)
                begin_tile(pos + 1)
        else:
            emit(pos)


def _attn_call(qt, k, vt, g, ga):
    B, n_pairs, _, S = qt.shape
    assert KV_TILE % Q_TILE == 0 and S % KV_TILE == 0
    rows = KV_TILE
    tile = lambda width: pl.BlockSpec((1, rows, width), lambda b, i: (b, i, 0))
    whole = lambda width: pl.BlockSpec((1, S, width), lambda b, i: (b, 0, 0))
    score_buf = pltpu.VMEM((N_HEADS, KV_TILE, Q_TILE), F32)
    stat_buf = pltpu.VMEM((N_HEADS, SUBLANES, Q_TILE), F32)
    return pl.pallas_call(
        _attn_kernel,
        name="attention",
        grid=(B, S // rows),
        in_specs=[pl.BlockSpec((1, n_pairs, LANES, rows), lambda b, i: (b, 0, 0, i)),
                  whole(D_ATT),
                  pl.BlockSpec((1, n_pairs, LANES, S), lambda b, i: (b, 0, 0, 0)),
                  whole(LANES), tile(D_ATT)],
        out_specs=tile(D_ATT),
        out_shape=jax.ShapeDtypeStruct((B, S, D_ATT), BF16),
        scratch_shapes=[stat_buf, pltpu.VMEM((N_HEADS, PV_ROWS, Q_TILE), F32),
                        score_buf, score_buf, stat_buf, stat_buf,
                        pltpu.VMEM((N_HEADS, 2 * LANES, Q_TILE), BF16)],
        compiler_params=pltpu.CompilerParams(
            dimension_semantics=("arbitrary", "arbitrary"),
            vmem_limit_bytes=VMEM_LIMIT_BYTES),
    )(qt, k, vt, g, ga)


def _out_kernel(ya_ref, yp_ref, x_ref, ada_ref, wo_ref, bo_ref, lg_ref, lb_ref, o_ref, wob_ref):
    bi = pl.program_id(0)

    @pl.when((bi == 0) & (pl.program_id(1) == 0))
    def _():
        wob_ref[...] = wo_ref[...].astype(BF16)

    def project(c):
        rows = slice(c * OUT_CHUNK, (c + 1) * OUT_CHUNK)
        y_in = jnp.concatenate([ya_ref[0, rows, :], yp_ref[0, rows, :]], axis=1)
        return _dot(y_in, wob_ref[...]) + bo_ref[...]

    def residual_norm(c, y):
        rows = slice(c * OUT_CHUNK, (c + 1) * OUT_CHUNK)
        hres = DEEPNORM_ALPHA * x_ref[0, rows, :] + ada_ref[2, pl.ds(bi, 1), :] * y
        mu = jnp.mean(hres, axis=-1, keepdims=True)
        d = hres - mu
        var = jnp.mean(d * d, axis=-1, keepdims=True)
        o_ref[0, rows, :] = d * lax.rsqrt(var + LN_EPS) * lg_ref[...] + lb_ref[...]

    n_chunks = x_ref.shape[1] // OUT_CHUNK
    y = project(0)
    for c in range(1, n_chunks):
        y_next = project(c)
        residual_norm(c - 1, y)
        y = y_next
    residual_norm(n_chunks - 1, y)


def _out_call(ya, yp, x, ada, w_out, b_out, ln_g, ln_b):
    B, S, D = x.shape
    tile = lambda width: pl.BlockSpec((1, OUT_TILE, width), lambda b, i: (b, i, 0))
    const2 = lambda a: pl.BlockSpec(a.shape, lambda b, i: (0, 0))
    return pl.pallas_call(
        _out_kernel,
        name="output_stage",
        grid=(B, S // OUT_TILE),
        in_specs=[tile(D_ATT), tile(D_POOL), tile(D),
                  pl.BlockSpec(ada.shape, lambda b, i: (0, 0, 0)),
                  _resident(w_out.shape, lambda b, i: (0, 0)),
                  const2(b_out), const2(ln_g), const2(ln_b)],
        out_specs=tile(D),
        out_shape=jax.ShapeDtypeStruct((B, S, D), F32),
        scratch_shapes=[pltpu.VMEM(w_out.shape, BF16)],
        compiler_params=pltpu.CompilerParams(
            dimension_semantics=("arbitrary", "arbitrary"),
            vmem_limit_bytes=VMEM_LIMIT_BYTES),
    )(ya, yp, x, ada, w_out, b_out, ln_g, ln_b)


def _layer(x, c, w_ada, b_ada, w_in, b_in, w_pool_mix, b_pool_mix, pool_scale,
           w_out, b_out, ln_g, ln_b):
    ada = _ada_call(c, w_ada, b_ada[None, :])
    qt, k, vt, g, ga, yp = _in_call(x, ada, w_in.T, b_in[None, :], w_pool_mix, b_pool_mix,
                                    pool_scale[None, :])
    ya = _attn_call(qt, k, vt, g, ga)
    return _out_call(ya, yp, x, ada, w_out, b_out[None, :], ln_g[None, :], ln_b[None, :])


def kernel(x, c, w_ada, b_ada, w_in, b_in, w_pool_mix, b_pool_mix, pool_scale, w_out, b_out,
           ln_g, ln_b):
    for layer in range(w_ada.shape[0]):
        x = _layer(x, c, w_ada[layer], b_ada[layer], w_in[layer], b_in[layer],
                   w_pool_mix[layer], b_pool_mix[layer], pool_scale[layer],
                   w_out[layer], b_out[layer], ln_g[layer], ln_b[layer])
    return x
```

```python
import math

import jax
import jax.numpy as jnp
from jax import lax
from jax.experimental import pallas as pl
from jax.experimental.pallas import tpu as pltpu

D_MODEL = 1024
D_ATT = 512
D_POOL = 512
N_HEADS = 8
HEAD_DIM = 64
POOL_WINDOWS = (2, 4, 8, 16)
POOL_GROUP_DIM = 128
POOL_HISTORY = 16
LN_EPS = 1e-5
DEEPNORM_ALPHA = 2.0 ** 0.25
LOG2E = math.log2(math.e)
NEG_BIG = -1e30

LANES = 128
SUBLANES = 8
BF16_SUBLANES = 16
HEADS_PER_BLOCK = LANES // HEAD_DIM
G_TERMS = 3
CUM_ROWS = 128
PV_ROWS = HEAD_DIM + BF16_SUBLANES

IN_SOURCE = (("q", D_ATT), ("k", D_ATT), ("v", D_ATT), ("f", N_HEADS), ("p", D_POOL),
             ("ga", D_ATT), ("gp", D_POOL))
IN_WIDTH = {"f": LANES, "p": D_POOL, "gp": D_POOL, "ga": D_ATT, "q": D_ATT, "v": D_ATT,
            "k": D_ATT}
IN_OFFSET = dict(zip(IN_WIDTH, (sum(list(IN_WIDTH.values())[:i]) for i in range(len(IN_WIDTH)))))
IN_COLS = sum(IN_WIDTH.values())

SEQ_TILE = 1024
Q_TILE = 256
KV_TILE = 512
VMEM_LIMIT_BYTES = 56 * 1024 * 1024

F32 = jnp.float32
BF16 = jnp.bfloat16


def _silu(x):
    return x * jax.nn.sigmoid(x)


def _dot(a, b):
    return jnp.dot(a, b, preferred_element_type=F32)


def _resident(shape, index_map):
    return pl.BlockSpec(shape, index_map, pipeline_mode=pl.Buffered(1))


def _ada_kernel(c_ref, w_ref, b_ref, o_ref, cpad_ref, sc_ref):
    n_rows = c_ref.shape[0]

    @pl.when(pl.program_id(0) == 0)
    def _():
        cpad_ref[...] = jnp.zeros_like(cpad_ref)
        cpad_ref[0:n_rows, :] = c_ref[...]
        sc_ref[...] = _silu(cpad_ref[...].T)

    sc = sc_ref[...]
    for b in range(n_rows):
        o_ref[0, b:b + 1, :] = (jnp.sum(w_ref[...] * sc[:, b:b + 1], axis=0, keepdims=True)
                                + b_ref[...])


def _ada_call(c, w_ada, b_ada):
    n_rows, d = c.shape
    n_chunks = w_ada.shape[1] // d
    return pl.pallas_call(
        _ada_kernel,
        name="ada_vector",
        grid=(n_chunks,),
        in_specs=[
            pl.BlockSpec((n_rows, d), lambda j: (0, 0)),
            pl.BlockSpec((d, d), lambda j: (0, j)),
            pl.BlockSpec((1, d), lambda j: (0, j)),
        ],
        out_specs=pl.BlockSpec((1, n_rows, d), lambda j: (j, 0, 0)),
        out_shape=jax.ShapeDtypeStruct((n_chunks, n_rows, d), F32),
        scratch_shapes=[pltpu.VMEM((LANES, d), F32), pltpu.VMEM((d, LANES), F32)],
        compiler_params=pltpu.CompilerParams(dimension_semantics=("arbitrary",),
                                             vmem_limit_bytes=VMEM_LIMIT_BYTES),
    )(c, w_ada, b_ada)


def _split3(x):
    hi = x.astype(BF16).astype(F32)
    r = x - hi
    mid = r.astype(BF16).astype(F32)
    lo = (r - mid).astype(BF16).astype(F32)
    return hi, mid, lo


def _in_kernel(x_ref, ada_ref, wt_ref, b_ref, wpm_ref, bpm_ref, ps_ref,
               q_ref, k_ref, v_ref, g_ref, ga_ref, yp_ref,
               fcarry_ref, pcarry_ref, wcat_ref, bcat_ref, wmix_ref):
    bi = pl.program_id(0)
    si = pl.program_id(1)
    ts = x_ref.shape[1]

    @pl.when((bi == 0) & (si == 0))
    def _():
        src = 0
        for name, width in IN_SOURCE:
            dst, wide = IN_OFFSET[name], IN_WIDTH[name]
            w_seg = wt_ref[src:src + wide, :].T
            b_seg = b_ref[:, src:src + wide]
            if width < wide:
                w_seg = jnp.where(lax.broadcasted_iota(jnp.int32, w_seg.shape, 1) < width,
                                  w_seg, 0.0)
                b_seg = jnp.where(lax.broadcasted_iota(jnp.int32, b_seg.shape, 1) < width,
                                  b_seg, 0.0)
            wcat_ref[:, dst:dst + wide] = w_seg.astype(BF16)
            bcat_ref[:, dst:dst + wide] = b_seg
            src += width
        gd = POOL_GROUP_DIM
        wmix_ref[...] = jnp.zeros_like(wmix_ref)
        for g in range(len(POOL_WINDOWS)):
            o = (g % 2) * gd
            wmix_ref[g // 2, o:o + gd, o:o + gd] = wpm_ref[g].astype(BF16)

    @pl.when(si == 0)
    def _():
        fcarry_ref[...] = jnp.zeros_like(fcarry_ref)
        pcarry_ref[...] = jnp.zeros_like(pcarry_ref)

    shift = ada_ref[0, pl.ds(bi, 1), :]
    scale = ada_ref[1, pl.ds(bi, 1), :]
    u = (x_ref[0] * (1.0 + scale) + shift).astype(BF16)

    def project(*names):
        lo, hi = IN_OFFSET[names[0]], IN_OFFSET[names[-1]] + IN_WIDTH[names[-1]]
        out = _dot(u, wcat_ref[:, lo:hi]) + bcat_ref[:, lo:hi]
        return [out[:, IN_OFFSET[n] - lo:IN_OFFSET[n] - lo + IN_WIDTH[n]] for n in names]

    fl, p, gp = project("f", "p", "gp")
    ga, q, v = project("ga", "q", "v")

    ft = fl.T[:N_HEADS, :]
    logf = jnp.minimum(ft, 0.0) - jnp.log1p(jnp.exp(-jnp.abs(ft)))
    row = lax.broadcasted_iota(jnp.int32, (CUM_ROWS, CUM_ROWS), 0)
    col = lax.broadcasted_iota(jnp.int32, (CUM_ROWS, CUM_ROWS), 1)
    tri = jnp.where(row <= col, 1.0, 0.0).astype(BF16)
    pad = jnp.zeros((N_HEADS, ts), F32)
    terms = jnp.concatenate(list(_split3(logf)) + [pad], axis=0).astype(BF16)
    offset = fcarry_ref[...]
    cum_blocks = []
    for r in range(ts // CUM_ROWS):
        part = _dot(terms[:, r * CUM_ROWS:(r + 1) * CUM_ROWS], tri)
        local = part[:N_HEADS] + part[N_HEADS:2 * N_HEADS] + part[2 * N_HEADS:3 * N_HEADS]
        cum_blocks.append(local + offset)
        offset = jnp.broadcast_to(cum_blocks[-1][:, CUM_ROWS - 1:CUM_ROWS], offset.shape)
    fcarry_ref[...] = offset
    cum = jnp.concatenate(cum_blocks, axis=1)

    pad = jnp.zeros((LANES - G_TERMS * N_HEADS, ts), F32)
    g_t = jnp.concatenate(list(_split3(cum * (-LOG2E))) + [pad], axis=0)
    g_ref[0] = g_t.T.astype(BF16)

    pe = jnp.concatenate([pcarry_ref[...], p], axis=0)
    pcarry_ref[...] = p[ts - POOL_HISTORY:, :]
    t_head = si * ts + lax.broadcasted_iota(jnp.int32, (POOL_HISTORY, POOL_GROUP_DIM), 0)
    pooled = []
    for g, w in enumerate(POOL_WINDOWS):
        sl = slice(g * POOL_GROUP_DIM, (g + 1) * POOL_GROUP_DIM)
        y = pe[:, sl]
        sh = 1
        while sh < w:
            y = y + pltpu.roll(y, sh, axis=0)
            sh *= 2
        head = y[POOL_HISTORY:2 * POOL_HISTORY, :] / jnp.minimum(t_head + 1, w).astype(F32)
        rest = y[2 * POOL_HISTORY:, :]
        rest = rest * (1.0 / w) if w & (w - 1) == 0 else rest / float(w)
        mean = jnp.concatenate([head, rest], axis=0)
        pooled.append((mean - p[:, sl]).astype(BF16))
    mixed = []
    for j in range(len(POOL_WINDOWS) // 2):
        bias = jnp.concatenate([bpm_ref[2 * j:2 * j + 1, :], bpm_ref[2 * j + 1:2 * j + 2, :]],
                               axis=1)
        mixed.append(_dot(jnp.concatenate(pooled[2 * j:2 * j + 2], axis=1), wmix_ref[j]) + bias)

    (k,) = project("k")

    for j, mix in enumerate(mixed):
        sl = slice(2 * j * POOL_GROUP_DIM, (2 * j + 2) * POOL_GROUP_DIM)
        yp_ref[0, :, sl] = (mix * ps_ref[:, sl] * _silu(gp[:, sl])).astype(BF16)

    ga_ref[0] = _silu(ga).astype(BF16)

    q = q * (HEAD_DIM ** -0.5 * LOG2E)
    for blk in range(D_ATT // LANES):
        q_ref[0, blk] = q[:, blk * LANES:(blk + 1) * LANES].T.astype(BF16)

    for blk in range(D_ATT // LANES):
        v_ref[0, blk] = v[:, blk * LANES:(blk + 1) * LANES].T.astype(BF16)

    k_ref[0] = k.astype(BF16)


def _in_call(x, ada, w_in_t, b_in, w_pm, b_pm, pool_scale):
    B, S, D = x.shape
    ts = SEQ_TILE
    const2 = lambda b, s: (0, 0)
    const3 = lambda b, s: (0, 0, 0)
    tile = lambda width: pl.BlockSpec((1, ts, width), lambda b, s: (b, s, 0))
    n_pairs = D_ATT // LANES
    tile_t = pl.BlockSpec((1, n_pairs, LANES, ts), lambda b, s: (b, 0, 0, s))
    return pl.pallas_call(
        _in_kernel,
        name="input_stage",
        grid=(B, S // ts),
        in_specs=[tile(D), pl.BlockSpec(ada.shape, const3),
                  _resident(w_in_t.shape, const2), pl.BlockSpec(b_in.shape, const2),
                  pl.BlockSpec(w_pm.shape, const3), pl.BlockSpec(b_pm.shape, const2),
                  pl.BlockSpec(pool_scale.shape, const2)],
        out_specs=[tile_t, tile(D_ATT), tile_t, tile(LANES),
                   tile(D_ATT), tile(D_POOL)],
        out_shape=[jax.ShapeDtypeStruct((B, n_pairs, LANES, S), BF16),
                   jax.ShapeDtypeStruct((B, S, D_ATT), BF16),
                   jax.ShapeDtypeStruct((B, n_pairs, LANES, S), BF16),
                   jax.ShapeDtypeStruct((B, S, LANES), BF16),
                   jax.ShapeDtypeStruct((B, S, D_ATT), BF16),
                   jax.ShapeDtypeStruct((B, S, D_POOL), BF16)],
        scratch_shapes=[pltpu.VMEM((SUBLANES, LANES), F32),
                        pltpu.VMEM((POOL_HISTORY, D_POOL), F32),
                        pltpu.VMEM((D, IN_COLS), BF16),
                        pltpu.VMEM((1, IN_COLS), F32),
                        pltpu.VMEM((len(POOL_WINDOWS) // 2, 2 * POOL_GROUP_DIM,
                                    2 * POOL_GROUP_DIM), BF16)],
        compiler_params=pltpu.CompilerParams(
            dimension_semantics=("arbitrary", "arbitrary"),
            vmem_limit_bytes=VMEM_LIMIT_BYTES),
    )(x, ada, w_in_t, b_in, w_pm, b_pm, pool_scale)


def _attn_kernel(qt_ref, k_ref, vt_ref, g_ref, ga_ref, yp_ref, x_ref, ada_ref,
                 wo_ref, bo_ref, lg_ref, lb_ref, o_ref,
                 m_ref, acc_ref, s0_ref, s1_ref, mx0_ref, mx1_ref, rhs_ref, wob_ref):
    bi = pl.program_id(0)
    n_main = pl.program_id(1)
    tq = Q_TILE
    tk = KV_TILE
    diag_start = pl.multiple_of(n_main * tk, tk)

    row = lax.broadcasted_iota(jnp.int32, (LANES, tq), 0)

    @pl.when((bi == 0) & (n_main == 0))
    def _():
        wob_ref[...] = wo_ref[...].astype(BF16)
        for h in range(N_HEADS):
            pick = (row < G_TERMS * N_HEADS) & (row % N_HEADS == h)
            rhs_ref[h, LANES:, :] = jnp.where(pick, 1.0, 0.0).astype(BF16)

    def begin_tile(pos):
        m_ref[...] = jnp.full_like(m_ref, NEG_BIG)
        acc_ref[...] = jnp.zeros_like(acc_ref)
        for h in range(N_HEADS):
            pair, half = divmod(h, HEADS_PER_BLOCK)
            own = (row // HEAD_DIM) == half
            qt = qt_ref[0, pair, :, pos * tq:(pos + 1) * tq]
            rhs_ref[h, :LANES, :] = jnp.where(own, qt, jnp.zeros((), BF16))

    def scores(start, width, s_ref, mx_ref, diagonal=False):
        def head(h):
            pair = h // HEADS_PER_BLOCK
            kblk = k_ref[0, pl.ds(start, width), pair * LANES:(pair + 1) * LANES]
            gblk = g_ref[0, pl.ds(start, width), :]
            lhs = jnp.concatenate([kblk, gblk], axis=1)
            s = _dot(lhs, rhs_ref[h])
            if diagonal:
                key = lax.broadcasted_iota(jnp.int32, (tq, tq), 0)
                qry = lax.broadcasted_iota(jnp.int32, (tq, tq), 1)
                own = jnp.where(key <= qry, s[width - tq:, :], NEG_BIG)
                s = own if width == tq else jnp.concatenate([s[:width - tq, :], own], axis=0)
            s_ref[h, :width, :] = s
            mx_ref[h] = jnp.max(s.reshape(width // SUBLANES, SUBLANES, tq), axis=0)

        return [lambda h=h: head(h) for h in range(N_HEADS)]

    def softmax_pv(start, width, s_ref, mx_ref):
        def head(h):
            pair, half = divmod(h, HEADS_PER_BLOCK)
            m_prev = m_ref[h]
            m_new = jnp.maximum(m_prev, jnp.max(mx_ref[h], axis=0, keepdims=True))
            m_ref[h] = m_new
            alpha = jnp.exp2(m_prev - m_new)
            p = jnp.exp2(s_ref[h, :width, :] - m_new[0:1, :]).astype(BF16)
            vt = vt_ref[0, pair, half * HEAD_DIM:(half + 1) * HEAD_DIM, pl.ds(start, width)]
            first = lax.broadcasted_iota(jnp.int32, (PV_ROWS - HEAD_DIM, width), 0) == 0
            ones = jnp.where(first, 1.0, 0.0).astype(BF16)
            pv = _dot(jnp.concatenate([vt, ones], axis=0), p)
            acc_ref[h] = acc_ref[h] * alpha[0:1, :] + pv

        return [lambda h=h: head(h) for h in range(N_HEADS)]

    def run(*stages):
        for i in range(max(len(st) for st in stages)):
            for st in stages:
                if i < len(st):
                    st[i]()

    main = lambda k: pl.multiple_of(k * tk, tk)

    def first_scores():
        run(scores(main(0), tk, s0_ref, mx0_ref))

    def attend(pos):
        width = (pos + 1) * tq
        diag_scores = lambda s_ref, mx_ref: scores(diag_start, width, s_ref, mx_ref, True)
        diag_softmax = lambda s_ref, mx_ref: softmax_pv(diag_start, width, s_ref, mx_ref)
        n_pairs = jnp.maximum(n_main - 1, 0) // 2

        def pair_of_blocks(k):
            run(scores(main(k + 1), tk, s1_ref, mx1_ref),
                softmax_pv(main(k), tk, s0_ref, mx0_ref))
            run(scores(main(k + 2), tk, s0_ref, mx0_ref),
                softmax_pv(main(k + 1), tk, s1_ref, mx1_ref))

        def two_pairs(i, carry):
            pair_of_blocks(4 * i)
            pair_of_blocks(4 * i + 2)
            return carry

        lax.fori_loop(0, n_pairs // 2, two_pairs, 0)

        @pl.when(n_pairs % 2 == 1)
        def _():
            pair_of_blocks(2 * (n_pairs - 1))

        k_last = 2 * n_pairs
        left = n_main - k_last

        @pl.when(left == 0)
        def _():
            run(diag_scores(s0_ref, mx0_ref))
            run(diag_softmax(s0_ref, mx0_ref))

        @pl.when(left == 1)
        def _():
            run(diag_scores(s1_ref, mx1_ref), softmax_pv(main(k_last), tk, s0_ref, mx0_ref))
            run(diag_softmax(s1_ref, mx1_ref))

        @pl.when(left == 2)
        def _():
            run(scores(main(k_last + 1), tk, s1_ref, mx1_ref),
                softmax_pv(main(k_last), tk, s0_ref, mx0_ref))
            run(diag_scores(s0_ref, mx0_ref), softmax_pv(main(k_last + 1), tk, s1_ref, mx1_ref))
            run(diag_softmax(s0_ref, mx0_ref))

    def gated_heads(pos):
        rows = slice(pos * tq, (pos + 1) * tq)
        gated = []
        for pair in range(N_HEADS // HEADS_PER_BLOCK):
            outs = []
            for half in range(HEADS_PER_BLOCK):
                acc = acc_ref[pair * HEADS_PER_BLOCK + half]
                outs.append(acc[:HEAD_DIM, :] / acc[HEAD_DIM:HEAD_DIM + 1, :])
            att = jnp.concatenate(outs, axis=0).T
            sl = slice(pair * LANES, (pair + 1) * LANES)
            gated.append((att * ga_ref[0, rows, sl].astype(F32)).astype(BF16))
        return jnp.concatenate(gated, axis=1)

    def project(pos, ya):
        rows = slice(pos * tq, (pos + 1) * tq)
        return _dot(jnp.concatenate([ya, yp_ref[0, rows, :]], axis=1), wob_ref[...]) + bo_ref[...]

    def residual_norm(pos, y):
        rows = slice(pos * tq, (pos + 1) * tq)
        hres = DEEPNORM_ALPHA * x_ref[0, rows, :] + ada_ref[2, pl.ds(bi, 1), :] * y
        mu = jnp.mean(hres, axis=-1, keepdims=True)
        d = hres - mu
        var = jnp.mean(d * d, axis=-1, keepdims=True)
        o_ref[0, rows, :] = d * lax.rsqrt(var + LN_EPS) * lg_ref[...] + lb_ref[...]

    n_pos = tk // tq
    begin_tile(0)

    @pl.when(n_main > 0)
    def _():
        first_scores()

    for pos in range(n_pos):
        attend(pos)
        if pos + 1 < n_pos:
            @pl.when(n_main > 0)
            def _():
                y = project(pos, gated_heads(pos))
                begin_tile(pos + 1)
                first_scores()
                residual_norm(pos, y)

            @pl.when(n_main == 0)
            def _():
                residual_norm(pos, project(pos, gated_heads(pos)))
                begin_tile(pos + 1)
        else:
            residual_norm(pos, project(pos, gated_heads(pos)))


def _attn_call(qt, k, vt, g, ga, yp, x, ada, w_out, b_out, ln_g, ln_b):
    B, n_pairs, _, S = qt.shape
    D = x.shape[2]
    assert KV_TILE % Q_TILE == 0 and S % KV_TILE == 0
    rows = KV_TILE
    tile = lambda width: pl.BlockSpec((1, rows, width), lambda b, i: (b, i, 0))
    whole = lambda width: pl.BlockSpec((1, S, width), lambda b, i: (b, 0, 0))
    const2 = lambda a: pl.BlockSpec(a.shape, lambda b, i: (0, 0))
    score_buf = pltpu.VMEM((N_HEADS, KV_TILE, Q_TILE), F32)
    stat_buf = pltpu.VMEM((N_HEADS, SUBLANES, Q_TILE), F32)
    return pl.pallas_call(
        _attn_kernel,
        name="attention",
        grid=(B, S // rows),
        in_specs=[pl.BlockSpec((1, n_pairs, LANES, rows), lambda b, i: (b, 0, 0, i)),
                  whole(D_ATT),
                  pl.BlockSpec((1, n_pairs, LANES, S), lambda b, i: (b, 0, 0, 0)),
                  whole(LANES), tile(D_ATT), tile(D_POOL), tile(D),
                  pl.BlockSpec(ada.shape, lambda b, i: (0, 0, 0)),
                  _resident(w_out.shape, lambda b, i: (0, 0)),
                  const2(b_out), const2(ln_g), const2(ln_b)],
        out_specs=tile(D),
        out_shape=jax.ShapeDtypeStruct((B, S, D), F32),
        scratch_shapes=[stat_buf, pltpu.VMEM((N_HEADS, PV_ROWS, Q_TILE), F32),
                        score_buf, score_buf, stat_buf, stat_buf,
                        pltpu.VMEM((N_HEADS, 2 * LANES, Q_TILE), BF16),
                        pltpu.VMEM(w_out.shape, BF16)],
        compiler_params=pltpu.CompilerParams(
            dimension_semantics=("arbitrary", "arbitrary"),
            vmem_limit_bytes=VMEM_LIMIT_BYTES),
    )(qt, k, vt, g, ga, yp, x, ada, w_out, b_out, ln_g, ln_b)


def _layer(x, c, w_ada, b_ada, w_in, b_in, w_pool_mix, b_pool_mix, pool_scale,
           w_out, b_out, ln_g, ln_b):
    ada = _ada_call(c, w_ada, b_ada[None, :])
    qt, k, vt, g, ga, yp = _in_call(x, ada, w_in.T, b_in[None, :], w_pool_mix, b_pool_mix,
                                    pool_scale[None, :])
    return _attn_call(qt, k, vt, g, ga, yp, x, ada, w_out, b_out[None, :], ln_g[None, :],
                      ln_b[None, :])


def kernel(x, c, w_ada, b_ada, w_in, b_in, w_pool_mix, b_pool_mix, pool_scale, w_out, b_out,
           ln_g, ln_b):
    for layer in range(w_ada.shape[0]):
        x = _layer(x, c, w_ada[layer], b_ada[layer], w_in[layer], b_in[layer],
                   w_pool_mix[layer], b_pool_mix[layer], pool_scale[layer],
                   w_out[layer], b_out[layer], ln_g[layer], ln_b[layer])
    return x
```

```python
import math

import jax
import jax.numpy as jnp
from jax import lax
from jax.experimental import pallas as pl
from jax.experimental.pallas import tpu as pltpu

D_MODEL = 1024
D_ATT = 512
D_POOL = 512
N_HEADS = 8
HEAD_DIM = 64
POOL_WINDOWS = (2, 4, 8, 16)
POOL_GROUP_DIM = 128
POOL_HISTORY = 16
LN_EPS = 1e-5
DEEPNORM_ALPHA = 2.0 ** 0.25
LOG2E = math.log2(math.e)
NEG_BIG = -1e30

LANES = 128
SUBLANES = 8
BF16_SUBLANES = 16
HEADS_PER_BLOCK = LANES // HEAD_DIM
G_TERMS = 3
CUM_ROWS = 128
PV_ROWS = HEAD_DIM + BF16_SUBLANES

IN_SOURCE = (("q", D_ATT), ("k", D_ATT), ("v", D_ATT), ("f", N_HEADS), ("p", D_POOL),
             ("ga", D_ATT), ("gp", D_POOL))
IN_WIDTH = {"f": LANES, "p": D_POOL, "gp": D_POOL, "ga": D_ATT, "q": D_ATT, "v": D_ATT,
            "k": D_ATT}
IN_OFFSET = dict(zip(IN_WIDTH, (sum(list(IN_WIDTH.values())[:i]) for i in range(len(IN_WIDTH)))))
IN_COLS = sum(IN_WIDTH.values())

SEQ_TILE = 1024
Q_TILE = 256
KV_TILE = 512
VMEM_LIMIT_BYTES = 56 * 1024 * 1024

F32 = jnp.float32
BF16 = jnp.bfloat16


def _silu(x):
    return x * jax.nn.sigmoid(x)


def _dot(a, b):
    return jnp.dot(a, b, preferred_element_type=F32)


def _resident(shape, index_map):
    return pl.BlockSpec(shape, index_map, pipeline_mode=pl.Buffered(1))


def _ada_kernel(c_ref, w_ref, b_ref, o_ref, cpad_ref):
    n_rows = c_ref.shape[0]
    cpad_ref[...] = jnp.zeros_like(cpad_ref)
    cpad_ref[0:n_rows, :] = c_ref[...]
    sc = _silu(cpad_ref[...].T)
    for b in range(n_rows):
        o_ref[0, b:b + 1, :] = (jnp.sum(w_ref[...] * sc[:, b:b + 1], axis=0, keepdims=True)
                                + b_ref[...])


def _ada_call(c, w_ada, b_ada):
    n_rows, d = c.shape
    n_chunks = w_ada.shape[1] // d
    return pl.pallas_call(
        _ada_kernel,
        name="ada_vector",
        grid=(n_chunks,),
        in_specs=[
            pl.BlockSpec((n_rows, d), lambda j: (0, 0)),
            pl.BlockSpec((d, d), lambda j: (0, j)),
            pl.BlockSpec((1, d), lambda j: (0, j)),
        ],
        out_specs=pl.BlockSpec((1, n_rows, d), lambda j: (j, 0, 0)),
        out_shape=jax.ShapeDtypeStruct((n_chunks, n_rows, d), F32),
        scratch_shapes=[pltpu.VMEM((LANES, d), F32)],
        compiler_params=pltpu.CompilerParams(dimension_semantics=("parallel",),
                                             vmem_limit_bytes=VMEM_LIMIT_BYTES),
    )(c, w_ada, b_ada)


def _split3(x):
    hi = x.astype(BF16).astype(F32)
    r = x - hi
    mid = r.astype(BF16).astype(F32)
    lo = (r - mid).astype(BF16).astype(F32)
    return hi, mid, lo


def _in_kernel(x_ref, ada_ref, wt_ref, b_ref, wpm_ref, bpm_ref, ps_ref,
               q_ref, k_ref, v_ref, g_ref, ga_ref, yp_ref,
               fcarry_ref, pcarry_ref, wcat_ref, bcat_ref, wmix_ref):
    bi = pl.program_id(0)
    si = pl.program_id(1)
    ts = x_ref.shape[1]

    @pl.when((bi == 0) & (si == 0))
    def _():
        src = 0
        for name, width in IN_SOURCE:
            dst, wide = IN_OFFSET[name], IN_WIDTH[name]
            w_seg = wt_ref[src:src + wide, :].T
            b_seg = b_ref[:, src:src + wide]
            if width < wide:
                w_seg = jnp.where(lax.broadcasted_iota(jnp.int32, w_seg.shape, 1) < width,
                                  w_seg, 0.0)
                b_seg = jnp.where(lax.broadcasted_iota(jnp.int32, b_seg.shape, 1) < width,
                                  b_seg, 0.0)
            wcat_ref[:, dst:dst + wide] = w_seg.astype(BF16)
            bcat_ref[:, dst:dst + wide] = b_seg
            src += width
        gd = POOL_GROUP_DIM
        wmix_ref[...] = jnp.zeros_like(wmix_ref)
        for g in range(len(POOL_WINDOWS)):
            o = (g % 2) * gd
            wmix_ref[g // 2, o:o + gd, o:o + gd] = wpm_ref[g].astype(BF16)

    @pl.when(si == 0)
    def _():
        fcarry_ref[...] = jnp.zeros_like(fcarry_ref)
        pcarry_ref[...] = jnp.zeros_like(pcarry_ref)

    shift = ada_ref[0, pl.ds(bi, 1), :]
    scale = ada_ref[1, pl.ds(bi, 1), :]
    u = (x_ref[0] * (1.0 + scale) + shift).astype(BF16)

    def project(*names):
        lo, hi = IN_OFFSET[names[0]], IN_OFFSET[names[-1]] + IN_WIDTH[names[-1]]
        out = _dot(u, wcat_ref[:, lo:hi]) + bcat_ref[:, lo:hi]
        return [out[:, IN_OFFSET[n] - lo:IN_OFFSET[n] - lo + IN_WIDTH[n]] for n in names]

    fl, p, gp = project("f", "p", "gp")
    ga, q, v = project("ga", "q", "v")

    ft = fl.T[:N_HEADS, :]
    logf = jnp.minimum(ft, 0.0) - jnp.log1p(jnp.exp(-jnp.abs(ft)))
    row = lax.broadcasted_iota(jnp.int32, (CUM_ROWS, CUM_ROWS), 0)
    col = lax.broadcasted_iota(jnp.int32, (CUM_ROWS, CUM_ROWS), 1)
    tri = jnp.where(row <= col, 1.0, 0.0).astype(BF16)
    pad = jnp.zeros((N_HEADS, ts), F32)
    terms = jnp.concatenate(list(_split3(logf)) + [pad], axis=0).astype(BF16)
    offset = fcarry_ref[...]
    cum_blocks = []
    for r in range(ts // CUM_ROWS):
        part = _dot(terms[:, r * CUM_ROWS:(r + 1) * CUM_ROWS], tri)
        local = part[:N_HEADS] + part[N_HEADS:2 * N_HEADS] + part[2 * N_HEADS:3 * N_HEADS]
        cum_blocks.append(local + offset)
        offset = jnp.broadcast_to(cum_blocks[-1][:, CUM_ROWS - 1:CUM_ROWS], offset.shape)
    fcarry_ref[...] = offset
    cum = jnp.concatenate(cum_blocks, axis=1)

    pad = jnp.zeros((LANES - G_TERMS * N_HEADS, ts), F32)
    g_t = jnp.concatenate(list(_split3(cum * (-LOG2E))) + [pad], axis=0)
    g_ref[0] = g_t.T.astype(BF16)

    pe = jnp.concatenate([pcarry_ref[...], p], axis=0)
    pcarry_ref[...] = p[ts - POOL_HISTORY:, :]
    t_head = si * ts + lax.broadcasted_iota(jnp.int32, (POOL_HISTORY, POOL_GROUP_DIM), 0)
    pooled = []
    for g, w in enumerate(POOL_WINDOWS):
        sl = slice(g * POOL_GROUP_DIM, (g + 1) * POOL_GROUP_DIM)
        y = pe[:, sl]
        sh = 1
        while sh < w:
            y = y + pltpu.roll(y, sh, axis=0)
            sh *= 2
        head = y[POOL_HISTORY:2 * POOL_HISTORY, :] / jnp.minimum(t_head + 1, w).astype(F32)
        rest = y[2 * POOL_HISTORY:, :]
        rest = rest * (1.0 / w) if w & (w - 1) == 0 else rest / float(w)
        mean = jnp.concatenate([head, rest], axis=0)
        pooled.append((mean - p[:, sl]).astype(BF16))
    mixed = []
    for j in range(len(POOL_WINDOWS) // 2):
        bias = jnp.concatenate([bpm_ref[2 * j:2 * j + 1, :], bpm_ref[2 * j + 1:2 * j + 2, :]],
                               axis=1)
        mixed.append(_dot(jnp.concatenate(pooled[2 * j:2 * j + 2], axis=1), wmix_ref[j]) + bias)

    (k,) = project("k")

    for j, mix in enumerate(mixed):
        sl = slice(2 * j * POOL_GROUP_DIM, (2 * j + 2) * POOL_GROUP_DIM)
        yp_ref[0, :, sl] = (mix * ps_ref[:, sl] * _silu(gp[:, sl])).astype(BF16)

    ga_ref[0] = _silu(ga).astype(BF16)

    q = q * (HEAD_DIM ** -0.5 * LOG2E)
    for blk in range(D_ATT // LANES):
        q_ref[0, blk] = q[:, blk * LANES:(blk + 1) * LANES].T.astype(BF16)

    for blk in range(D_ATT // LANES):
        v_ref[0, blk] = v[:, blk * LANES:(blk + 1) * LANES].T.astype(BF16)

    k_ref[0] = k.astype(BF16)


def _in_call(x, ada, w_in_t, b_in, w_pm, b_pm, pool_scale):
    B, S, D = x.shape
    ts = SEQ_TILE
    const2 = lambda b, s: (0, 0)
    const3 = lambda b, s: (0, 0, 0)
    tile = lambda width: pl.BlockSpec((1, ts, width), lambda b, s: (b, s, 0))
    n_pairs = D_ATT // LANES
    tile_t = pl.BlockSpec((1, n_pairs, LANES, ts), lambda b, s: (b, 0, 0, s))
    return pl.pallas_call(
        _in_kernel,
        name="input_stage",
        grid=(B, S // ts),
        in_specs=[tile(D), pl.BlockSpec(ada.shape, const3),
                  _resident(w_in_t.shape, const2), pl.BlockSpec(b_in.shape, const2),
                  pl.BlockSpec(w_pm.shape, const3), pl.BlockSpec(b_pm.shape, const2),
                  pl.BlockSpec(pool_scale.shape, const2)],
        out_specs=[tile_t, tile(D_ATT), tile_t, tile(LANES),
                   tile(D_ATT), tile(D_POOL)],
        out_shape=[jax.ShapeDtypeStruct((B, n_pairs, LANES, S), BF16),
                   jax.ShapeDtypeStruct((B, S, D_ATT), BF16),
                   jax.ShapeDtypeStruct((B, n_pairs, LANES, S), BF16),
                   jax.ShapeDtypeStruct((B, S, LANES), BF16),
                   jax.ShapeDtypeStruct((B, S, D_ATT), BF16),
                   jax.ShapeDtypeStruct((B, S, D_POOL), BF16)],
        scratch_shapes=[pltpu.VMEM((SUBLANES, LANES), F32),
                        pltpu.VMEM((POOL_HISTORY, D_POOL), F32),
                        pltpu.VMEM((D, IN_COLS), BF16),
                        pltpu.VMEM((1, IN_COLS), F32),
                        pltpu.VMEM((len(POOL_WINDOWS) // 2, 2 * POOL_GROUP_DIM,
                                    2 * POOL_GROUP_DIM), BF16)],
        compiler_params=pltpu.CompilerParams(
            dimension_semantics=("arbitrary", "arbitrary"),
            vmem_limit_bytes=VMEM_LIMIT_BYTES),
    )(x, ada, w_in_t, b_in, w_pm, b_pm, pool_scale)


def _attn_kernel(qt_ref, k_ref, vt_ref, g_ref, ga_ref, yp_ref, x_ref, ada_ref,
                 wo_ref, bo_ref, lg_ref, lb_ref, o_ref,
                 m_ref, acc_ref, s0_ref, s1_ref, mx0_ref, mx1_ref, rhs_ref, wob_ref, ones_ref):
    bi = pl.program_id(0)
    n_main = pl.program_id(1)
    tq = Q_TILE
    tk = KV_TILE
    diag_start = pl.multiple_of(n_main * tk, tk)

    row = lax.broadcasted_iota(jnp.int32, (LANES, tq), 0)

    @pl.when((bi == 0) & (n_main == 0))
    def _():
        wob_ref[...] = wo_ref[...].astype(BF16)
        first = lax.broadcasted_iota(jnp.int32, ones_ref.shape, 0) == 0
        ones_ref[...] = jnp.where(first, 1.0, 0.0).astype(BF16)
        for h in range(N_HEADS):
            pick = (row < G_TERMS * N_HEADS) & (row % N_HEADS == h)
            rhs_ref[h, LANES:, :] = jnp.where(pick, 1.0, 0.0).astype(BF16)

    def begin_tile(pos):
        m_ref[...] = jnp.full_like(m_ref, NEG_BIG)
        acc_ref[...] = jnp.zeros_like(acc_ref)
        for h in range(N_HEADS):
            pair, half = divmod(h, HEADS_PER_BLOCK)
            own = (row // HEAD_DIM) == half
            qt = qt_ref[0, pair, :, pos * tq:(pos + 1) * tq]
            rhs_ref[h, :LANES, :] = jnp.where(own, qt, jnp.zeros((), BF16))

    def scores(start, width, s_ref, mx_ref, diagonal=False):
        def head(h):
            pair = h // HEADS_PER_BLOCK
            kblk = k_ref[0, pl.ds(start, width), pair * LANES:(pair + 1) * LANES]
            gblk = g_ref[0, pl.ds(start, width), :]
            lhs = jnp.concatenate([kblk, gblk], axis=1)
            s = _dot(lhs, rhs_ref[h])
            if diagonal:
                key = lax.broadcasted_iota(jnp.int32, (tq, tq), 0)
                qry = lax.broadcasted_iota(jnp.int32, (tq, tq), 1)
                own = jnp.where(key <= qry, s[width - tq:, :], NEG_BIG)
                s = own if width == tq else jnp.concatenate([s[:width - tq, :], own], axis=0)
            s_ref[h, :width, :] = s
            mx_ref[h] = jnp.max(s.reshape(width // SUBLANES, SUBLANES, tq), axis=0)

        return [lambda h=h: head(h) for h in range(N_HEADS)]

    def softmax_pv(start, width, s_ref, mx_ref):
        def head(h):
            pair, half = divmod(h, HEADS_PER_BLOCK)
            m_prev = m_ref[h]
            m_new = jnp.maximum(m_prev, jnp.max(mx_ref[h], axis=0, keepdims=True))
            m_ref[h] = m_new
            alpha = jnp.exp2(m_prev - m_new)
            p = jnp.exp2(s_ref[h, :width, :] - m_new[0:1, :]).astype(BF16)
            vt = vt_ref[0, pair, half * HEAD_DIM:(half + 1) * HEAD_DIM, pl.ds(start, width)]
            ones = ones_ref[:, :width]
            pv = _dot(jnp.concatenate([vt, ones], axis=0), p)
            acc_ref[h] = acc_ref[h] * alpha[0:1, :] + pv

        return [lambda h=h: head(h) for h in range(N_HEADS)]

    def run(*stages):
        for i in range(max(len(st) for st in stages)):
            for st in stages:
                if i < len(st):
                    st[i]()

    main = lambda k: pl.multiple_of(k * tk, tk)

    def first_scores():
        run(scores(main(0), tk, s0_ref, mx0_ref))

    def attend(pos):
        width = (pos + 1) * tq
        diag_scores = lambda s_ref, mx_ref: scores(diag_start, width, s_ref, mx_ref, True)
        diag_softmax = lambda s_ref, mx_ref: softmax_pv(diag_start, width, s_ref, mx_ref)
        n_pairs = jnp.maximum(n_main - 1, 0) // 2

        def pair_of_blocks(k):
            run(scores(main(k + 1), tk, s1_ref, mx1_ref),
                softmax_pv(main(k), tk, s0_ref, mx0_ref))
            run(scores(main(k + 2), tk, s0_ref, mx0_ref),
                softmax_pv(main(k + 1), tk, s1_ref, mx1_ref))

        def two_pairs(i, carry):
            pair_of_blocks(4 * i)
            pair_of_blocks(4 * i + 2)
            return carry

        lax.fori_loop(0, n_pairs // 2, two_pairs, 0)

        @pl.when(n_pairs % 2 == 1)
        def _():
            pair_of_blocks(2 * (n_pairs - 1))

        k_last = 2 * n_pairs
        left = n_main - k_last

        @pl.when(left == 0)
        def _():
            run(diag_scores(s0_ref, mx0_ref))
            run(diag_softmax(s0_ref, mx0_ref))

        @pl.when(left == 1)
        def _():
            run(diag_scores(s1_ref, mx1_ref), softmax_pv(main(k_last), tk, s0_ref, mx0_ref))
            run(diag_softmax(s1_ref, mx1_ref))

        @pl.when(left == 2)
        def _():
            run(scores(main(k_last + 1), tk, s1_ref, mx1_ref),
                softmax_pv(main(k_last), tk, s0_ref, mx0_ref))
            run(diag_scores(s0_ref, mx0_ref), softmax_pv(main(k_last + 1), tk, s1_ref, mx1_ref))
            run(diag_softmax(s0_ref, mx0_ref))

    def gated_heads(pos):
        rows = slice(pos * tq, (pos + 1) * tq)
        gated = []
        for pair in range(N_HEADS // HEADS_PER_BLOCK):
            outs = []
            for half in range(HEADS_PER_BLOCK):
                acc = acc_ref[pair * HEADS_PER_BLOCK + half]
                outs.append(acc[:HEAD_DIM, :] / acc[HEAD_DIM:HEAD_DIM + 1, :])
            att = jnp.concatenate(outs, axis=0).T
            sl = slice(pair * LANES, (pair + 1) * LANES)
            gated.append((att * ga_ref[0, rows, sl].astype(F32)).astype(BF16))
        return jnp.concatenate(gated, axis=1)

    def project(pos, ya):
        rows = slice(pos * tq, (pos + 1) * tq)
        return _dot(jnp.concatenate([ya, yp_ref[0, rows, :]], axis=1), wob_ref[...]) + bo_ref[...]

    def residual_norm(pos, y):
        rows = slice(pos * tq, (pos + 1) * tq)
        hres = DEEPNORM_ALPHA * x_ref[0, rows, :] + ada_ref[2, pl.ds(bi, 1), :] * y
        mu = jnp.mean(hres, axis=-1, keepdims=True)
        d = hres - mu
        var = jnp.mean(d * d, axis=-1, keepdims=True)
        o_ref[0, rows, :] = d * lax.rsqrt(var + LN_EPS) * lg_ref[...] + lb_ref[...]

    n_pos = tk // tq
    begin_tile(0)

    @pl.when(n_main > 0)
    def _():
        first_scores()

    for pos in range(n_pos):
        attend(pos)
        if pos + 1 < n_pos:
            @pl.when(n_main > 0)
            def _():
                y = project(pos, gated_heads(pos))
                begin_tile(pos + 1)
                first_scores()
                residual_norm(pos, y)

            @pl.when(n_main == 0)
            def _():
                residual_norm(pos, project(pos, gated_heads(pos)))
                begin_tile(pos + 1)
        else:
            residual_norm(pos, project(pos, gated_heads(pos)))


def _attn_call(qt, k, vt, g, ga, yp, x, ada, w_out, b_out, ln_g, ln_b):
    B, n_pairs, _, S = qt.shape
    D = x.shape[2]
    assert KV_TILE % Q_TILE == 0 and S % KV_TILE == 0
    rows = KV_TILE
    tile = lambda width: pl.BlockSpec((1, rows, width), lambda b, i: (b, i, 0))
    whole = lambda width: pl.BlockSpec((1, S, width), lambda b, i: (b, 0, 0))
    const2 = lambda a: pl.BlockSpec(a.shape, lambda b, i: (0, 0))
    score_buf = pltpu.VMEM((N_HEADS, KV_TILE, Q_TILE), F32)
    stat_buf = pltpu.VMEM((N_HEADS, SUBLANES, Q_TILE), F32)
    return pl.pallas_call(
        _attn_kernel,
        name="attention",
        grid=(B, S // rows),
        in_specs=[pl.BlockSpec((1, n_pairs, LANES, rows), lambda b, i: (b, 0, 0, i)),
                  whole(D_ATT),
                  pl.BlockSpec((1, n_pairs, LANES, S), lambda b, i: (b, 0, 0, 0)),
                  whole(LANES), tile(D_ATT), tile(D_POOL), tile(D),
                  pl.BlockSpec(ada.shape, lambda b, i: (0, 0, 0)),
                  _resident(w_out.shape, lambda b, i: (0, 0)),
                  const2(b_out), const2(ln_g), const2(ln_b)],
        out_specs=tile(D),
        out_shape=jax.ShapeDtypeStruct((B, S, D), F32),
        scratch_shapes=[stat_buf, pltpu.VMEM((N_HEADS, PV_ROWS, Q_TILE), F32),
                        score_buf, score_buf, stat_buf, stat_buf,
                        pltpu.VMEM((N_HEADS, 2 * LANES, Q_TILE), BF16),
                        pltpu.VMEM(w_out.shape, BF16),
                        pltpu.VMEM((PV_ROWS - HEAD_DIM, KV_TILE), BF16)],
        compiler_params=pltpu.CompilerParams(
            dimension_semantics=("arbitrary", "arbitrary"),
            vmem_limit_bytes=VMEM_LIMIT_BYTES),
    )(qt, k, vt, g, ga, yp, x, ada, w_out, b_out, ln_g, ln_b)


def _layer(x, c, w_ada, b_ada, w_in, b_in, w_pool_mix, b_pool_mix, pool_scale,
           w_out, b_out, ln_g, ln_b):
    ada = _ada_call(c, w_ada, b_ada[None, :])
    qt, k, vt, g, ga, yp = _in_call(x, ada, w_in.T, b_in[None, :], w_pool_mix, b_pool_mix,
                                    pool_scale[None, :])
    return _attn_call(qt, k, vt, g, ga, yp, x, ada, w_out, b_out[None, :], ln_g[None, :],
                      ln_b[None, :])


def kernel(x, c, w_ada, b_ada, w_in, b_in, w_pool_mix, b_pool_mix, pool_scale, w_out, b_out,
           ln_g, ln_b):
    for layer in range(w_ada.shape[0]):
        x = _layer(x, c, w_ada[layer], b_ada[layer], w_in[layer], b_in[layer],
                   w_pool_mix[layer], b_pool_mix[layer], pool_scale[layer],
                   w_out[layer], b_out[layer], ln_g[layer], ln_b[layer])
    return x
```
